```python
import math
import jax
import jax.numpy as jnp
from jax import lax
import numpy as np

D_MODEL = 1024
BATCH = 8
SEQ = 2048
DEPTH = 2

N_BRANCHES = 3
BRANCH_WIDTH = 512
HGRN_HEADS = 4
HGRN_DK = 128
HGRN_DV = 128
DSA_HEADS = 4
DSA_HEAD_DIM = 128
IDX_HEADS = 4
IDX_DIM = 64
DSA_MAX_TOPK = 256
QBLOCK = 128
GLA_HEADS = 4
GLA_DK = 64
GLA_DV = 128
GLA_GATE_RANK = 16
GLA_TAU = 16.0
CHUNK = 64
PEER_HEADS = 8
PEER_DQ = 256
N_KEYS = 128
N_EXPERTS = N_KEYS * N_KEYS
PEER_TOPK = 16
PEER_BLOCK = 128
EPS = 1e-6

IN_SPLITS = (
    HGRN_HEADS * HGRN_DK,
    HGRN_HEADS * HGRN_DK,
    HGRN_HEADS * HGRN_DV,
    HGRN_HEADS * HGRN_DV,
    DSA_HEADS * DSA_HEAD_DIM,
    DSA_HEADS * DSA_HEAD_DIM,
    DSA_HEADS * DSA_HEAD_DIM,
    IDX_HEADS * IDX_DIM,
    IDX_DIM,
    IDX_HEADS,
    GLA_HEADS * GLA_DK,
    GLA_HEADS * GLA_DK,
    GLA_HEADS * GLA_DV,
    GLA_GATE_RANK,
    GLA_HEADS * GLA_DV,
    N_BRANCHES * D_MODEL,
)
IN_COLS = sum(IN_SPLITS)

kernel_name = "hybrid_hgrn2_dsa_gla_peer_adaln"


def rms_norm(x, gain):
    xf = x.astype(jnp.float32)
    y = xf * lax.rsqrt(jnp.mean(xf * xf, axis=-1, keepdims=True) + EPS)
    return (y * gain.astype(jnp.float32)).astype(x.dtype)


def modulate(x, gain, shift, scale):
    return rms_norm(x, gain) * (1.0 + scale[:, None, :]) + shift[:, None, :]


def split_heads(t, n_heads):
    return t.reshape(*t.shape[:-1], n_heads, t.shape[-1] // n_heads)


def chunk_gated_scan(q, k, v, log_decay, scale):
    B, S, H, dk = q.shape
    dv = v.shape[-1]
    n_chunks = S // CHUNK

    def to_chunks(t):
        t = t.astype(jnp.float32).reshape(B, n_chunks, CHUNK, H, t.shape[-1])
        return jnp.transpose(t, (1, 0, 3, 2, 4))

    qc, kc, vc, gc = to_chunks(q * scale), to_chunks(k), to_chunks(v), to_chunks(log_decay)
    tri = jnp.tril(jnp.ones((CHUNK, CHUNK), dtype=bool))

    def step(state, inp):
        qb, kb, vb, gb = inp
        b = jnp.cumsum(gb, axis=2)
        diff = b[:, :, :, None, :] - b[:, :, None, :, :]
        decay = jnp.exp(jnp.where(tri[:, :, None], diff, -jnp.inf))
        scores = jnp.einsum('bhtk,bhsk,bhtsk->bhts', qb, kb, decay)
        o = jnp.einsum('bhts,bhsv->bhtv', scores, vb)
        o = o + jnp.einsum('bhtk,bhkv->bhtv', qb * jnp.exp(b), state)
        b_last = b[:, :, -1:, :]
        new_state = jnp.exp(b_last[:, :, 0, :])[..., None] * state + jnp.einsum(
            'bhsk,bhsv->bhkv', kb * jnp.exp(b_last - b), vb)
        return new_state, o

    state0 = jnp.zeros((B, H, dk, dv), jnp.float32)
    _, out = lax.scan(step, state0, (qc, kc, vc, gc))
    out = jnp.transpose(out, (1, 0, 3, 2, 4)).reshape(B, S, H, dv)
    return out.astype(v.dtype)


def hgrn2_mixer(q, f_logits, i, g, lower_bound, out_gain):
    B, S, _ = q.shape
    f = lower_bound + (1.0 - lower_bound) * jax.nn.sigmoid(f_logits.astype(jnp.float32))
    log_f = jnp.log(f)
    key = 1.0 - f
    o = chunk_gated_scan(split_heads(q, HGRN_HEADS), split_heads(key, HGRN_HEADS),
                         split_heads(i, HGRN_HEADS), split_heads(log_f, HGRN_HEADS), HGRN_DK ** -0.5)
    o = rms_norm(o, out_gain) * jax.nn.silu(split_heads(g, HGRN_HEADS))
    return o.reshape(B, S, HGRN_HEADS * HGRN_DV)


def gla_mixer(q, k, v, gate_code, g, gate_up, gate_bias, out_gain):
    B, S, _ = q.shape
    log_alpha = jax.nn.log_sigmoid((gate_code @ gate_up + gate_bias).astype(jnp.float32)) / GLA_TAU
    o = chunk_gated_scan(split_heads(q, GLA_HEADS), split_heads(k, GLA_HEADS),
                         split_heads(v, GLA_HEADS), split_heads(log_alpha, GLA_HEADS), GLA_DK ** -0.5)
    o = rms_norm(o, out_gain) * jax.nn.silu(split_heads(g, GLA_HEADS))
    return o.reshape(B, S, GLA_HEADS * GLA_DV)


def dsa_mixer(q, k, v, q_idx, k_idx, w_idx, q_gain, k_gain):
    B, S, _ = q.shape
    topk = min(DSA_MAX_TOPK, S // 4)
    n_blocks = S // QBLOCK
    q = rms_norm(split_heads(q, DSA_HEADS), q_gain)
    k = rms_norm(split_heads(k, DSA_HEADS), k_gain)
    v = split_heads(v, DSA_HEADS)
    q_idx = split_heads(q_idx, IDX_HEADS)
    w_idx = w_idx * (IDX_HEADS ** -0.5)
    key_pos = jnp.arange(S)
    batch_ix = jnp.arange(B)[:, None, None]
    attn_scale = DSA_HEAD_DIM ** -0.5

    def to_blocks(t):
        return jnp.moveaxis(t.reshape(B, n_blocks, QBLOCK, *t.shape[2:]), 1, 0)

    def block(args):
        blk, qb, qib, wb = args
        q_pos = blk * QBLOCK + jnp.arange(QBLOCK)
        causal = key_pos[None, :] <= q_pos[:, None]
        idx_logits = jnp.einsum('bqhd,bsd->bqhs', qib, k_idx).astype(jnp.float32) * (IDX_DIM ** -0.5)
        index_score = jnp.einsum('bqh,bqhs->bqs', wb.astype(jnp.float32), jax.nn.relu(idx_logits))
        index_score = jnp.where(causal[None], index_score, -jnp.inf)
        _, sel = lax.top_k(index_score, topk)
        valid = sel <= q_pos[None, :, None]
        k_sel = k[batch_ix, sel]
        v_sel = v[batch_ix, sel]
        s = jnp.einsum('bqhd,bqkhd->bhqk', qb, k_sel).astype(jnp.float32) * attn_scale
        s = jnp.where(valid[:, None], s, -jnp.inf)
        p = jax.nn.softmax(s, axis=-1).astype(v.dtype)
        return jnp.einsum('bhqk,bqkhd->bqhd', p, v_sel)

    out = lax.map(block, (jnp.arange(n_blocks), to_blocks(q), to_blocks(q_idx), to_blocks(w_idx)))
    return jnp.moveaxis(out, 0, 1).reshape(B, S, DSA_HEADS * DSA_HEAD_DIM)


def peer_ffn(h, w_query, sub_keys, expert_u, expert_v):
    B, S, D = h.shape
    tokens = h.reshape(-1, PEER_BLOCK, D)

    def block(hb):
        q = (hb @ w_query).reshape(PEER_BLOCK, PEER_HEADS, 2, PEER_DQ // 2)
        s = jnp.einsum('thpd,hpnd->thpn', q, sub_keys).astype(jnp.float32)
        s_top, i_top = lax.top_k(s, PEER_TOPK)
        cand = (s_top[:, :, 0, :, None] + s_top[:, :, 1, None, :]).reshape(PEER_BLOCK, PEER_HEADS, PEER_TOPK * PEER_TOPK)
        cand_idx = (i_top[:, :, 0, :, None] * N_KEYS + i_top[:, :, 1, None, :]).reshape(PEER_BLOCK, PEER_HEADS, PEER_TOPK * PEER_TOPK)
        best, pos = lax.top_k(cand, PEER_TOPK)
        expert_idx = jnp.take_along_axis(cand_idx, pos, axis=-1)
        gate = jax.nn.softmax(best, axis=-1).astype(hb.dtype)
        act = jax.nn.gelu(jnp.einsum('td,thkd->thk', hb, expert_u[expert_idx]), approximate=False)
        return jnp.einsum('thk,thkd->td', gate * act, expert_v[expert_idx])

    return lax.map(block, tokens).reshape(B, S, D)


def setup_inputs(seed: int = 0) -> dict:
    key = jax.random.key(seed)
    ks = jax.random.split(key, 20)

    def nrm(k, shape, scale):
        return jax.random.normal(k, shape, jnp.float32) * scale

    D = D_MODEL
    return {
        'x': nrm(ks[0], (BATCH, SEQ, D), 1.0),
        'c': nrm(ks[1], (BATCH, D), 1.0),
        'w_ada': nrm(ks[2], (DEPTH, D, 6 * D), D ** -0.5),
        'b_ada': nrm(ks[3], (DEPTH, 6 * D), 0.02),
        'norm_mix': 1.0 + nrm(ks[4], (DEPTH, D), 0.05),
        'norm_ffn': 1.0 + nrm(ks[5], (DEPTH, D), 0.05),
        'w_in': nrm(ks[6], (DEPTH, D, IN_COLS), D ** -0.5),
        'hgrn_lb_logits': nrm(ks[7], (DEPTH, HGRN_HEADS * HGRN_DK), 0.5),
        'hgrn_out_norm': 1.0 + nrm(ks[8], (DEPTH, HGRN_DV), 0.05),
        'dsa_q_norm': 1.0 + nrm(ks[9], (DEPTH, DSA_HEAD_DIM), 0.05),
        'dsa_k_norm': 1.0 + nrm(ks[10], (DEPTH, DSA_HEAD_DIM), 0.05),
        'gla_gate_up': nrm(ks[11], (DEPTH, GLA_GATE_RANK, GLA_HEADS * GLA_DK), GLA_GATE_RANK ** -0.5),
        'gla_gate_bias': nrm(ks[12], (DEPTH, GLA_HEADS * GLA_DK), 0.1),
        'gla_out_norm': 1.0 + nrm(ks[13], (DEPTH, GLA_DV), 0.05),
        'w_branch': nrm(ks[14], (DEPTH, N_BRANCHES, BRANCH_WIDTH, D), BRANCH_WIDTH ** -0.5),
        'w_out': nrm(ks[15], (DEPTH, D, D), D ** -0.5),
        'peer_w_query': nrm(ks[16], (DEPTH, D, PEER_HEADS * PEER_DQ), D ** -0.5),
        'peer_sub_keys': nrm(ks[17], (DEPTH, PEER_HEADS, 2, N_KEYS, PEER_DQ // 2), (PEER_DQ // 2) ** -0.5),
        'peer_u': nrm(ks[18], (DEPTH, N_EXPERTS, D), D ** -0.5),
        'peer_v': nrm(ks[19], (DEPTH, N_EXPERTS, D), PEER_HEADS ** -0.5),
    }


def reference(x, c, w_ada, b_ada, norm_mix, norm_ffn, w_in, hgrn_lb_logits, hgrn_out_norm,
              dsa_q_norm, dsa_k_norm, gla_gate_up, gla_gate_bias, gla_out_norm, w_branch, w_out,
              peer_w_query, peer_sub_keys, peer_u, peer_v):
    B, S, D = x.shape
    lb_p = jax.nn.softmax(hgrn_lb_logits.astype(jnp.float32), axis=0)
    lb_cum = jnp.cumsum(lb_p, axis=0)
    lower_bounds = lb_cum - lb_cum[0:1]
    split_points = np.cumsum(IN_SPLITS)[:-1].tolist()
    silu_c = jax.nn.silu(c)

    for l in range(DEPTH):
        mod = silu_c @ w_ada[l] + b_ada[l]
        shift_m, scale_m, gate_m, shift_f, scale_f, gate_f = jnp.split(mod, 6, axis=-1)

        h = modulate(x, norm_mix[l], shift_m, scale_m)
        z = h @ w_in[l]
        (a_q, a_f, a_i, a_g,
         b_q, b_k, b_v, b_qi, b_ki, b_wi,
         c_q, c_k, c_v, c_code, c_g,
         merge_logits) = jnp.split(z, split_points, axis=-1)

        y_a = hgrn2_mixer(a_q, a_f, a_i, a_g, lower_bounds[l], hgrn_out_norm[l])
        y_b = dsa_mixer(b_q, b_k, b_v, b_qi, b_ki, b_wi, dsa_q_norm[l], dsa_k_norm[l])
        y_c = gla_mixer(c_q, c_k, c_v, c_code, c_g, gla_gate_up[l], gla_gate_bias[l], gla_out_norm[l])

        branches = jnp.stack([y_a, y_b, y_c], axis=2)
        up = jnp.einsum('bsgk,gkd->bsgd', branches, w_branch[l])
        gates = jax.nn.sigmoid(merge_logits.reshape(B, S, N_BRANCHES, D))
        mixed = jnp.sum(gates * up, axis=2) @ w_out[l]
        x = x + gate_m[:, None, :] * mixed

        h2 = modulate(x, norm_ffn[l], shift_f, scale_f)
        x = x + gate_f[:, None, :] * peer_ffn(h2, peer_w_query[l], peer_sub_keys[l], peer_u[l], peer_v[l])

    return x
```

```python
import functools

import numpy as np
import jax
import jax.numpy as jnp
from jax import lax
from jax.experimental import pallas as pl
from jax.experimental.pallas import tpu as pltpu

F32, BF16, I32 = jnp.float32, jnp.bfloat16, jnp.int32

D_MODEL = 1024
HEADS = 4
HEAD_DIM = 128
BRANCH_WIDTH = HEADS * HEAD_DIM
IDX_HEADS = 4
IDX_DIM = 64
DSA_MAX_TOPK = 256
QBLOCK = 128
GLA_DK = 64
GLA_GATE_RANK = 16
GLA_TAU = 16.0
PEER_HEADS = 8
N_KEYS = 128
PEER_TOPK = 16
EPS = 1e-6

LANES = 128
SUBLANES = 8
VMEM_LIMIT_BYTES = 56 * 1024 * 1024

NEG_BIG = -1e30
INT_MIN = np.int32(-2 ** 31)

W_A = 4 * BRANCH_WIDTH
W_B = 3 * BRANCH_WIDTH
W_I = 3 * LANES
W_C = 4 * BRANCH_WIDTH + LANES
W_M = 3 * D_MODEL
W_PACK = W_A + W_B + W_I + W_C + W_M


def _dot(a, b):
    return jnp.dot(a, b, preferred_element_type=F32)


def _dot_nt(a, b):
    return lax.dot_general(a, b, (((1,), (1,)), ((), ())), preferred_element_type=F32)


def _dot_tn(a, b):
    return lax.dot_general(a, b, (((0,), (0,)), ((), ())), preferred_element_type=F32)


def _split2(x):
    hi = x.astype(BF16)
    lo = (x - hi.astype(F32)).astype(BF16)
    return hi, lo


def _split3(x):
    hi = x.astype(BF16)
    r = x - hi.astype(F32)
    mid = r.astype(BF16)
    lo = (r - mid.astype(F32)).astype(BF16)
    return hi, mid, lo


def _params(*sem):
    return pltpu.CompilerParams(dimension_semantics=sem, vmem_limit_bytes=VMEM_LIMIT_BYTES)


def _modulate(x, gain, shift, scale):
    ms = jnp.mean(x * x, axis=-1, keepdims=True)
    return x * lax.rsqrt(ms + EPS) * gain * (1.0 + scale) + shift


def _ada_kernel(c_ref, w_ref, b_ref, o_ref):
    c = c_ref[...]
    sc = c * jax.nn.sigmoid(c)
    a_hi, a_lo = _split2(sc)
    w_hi, w_lo = _split2(w_ref[0])
    o_ref[0] = _dot(a_hi, w_hi) + _dot(a_hi, w_lo) + _dot(a_lo, w_hi) + b_ref[0]


def _ada(c, w_ada, b_ada):
    depth, d, n = w_ada.shape
    bsz = c.shape[0]
    tn = 1536
    return pl.pallas_call(
        _ada_kernel,
        grid=(depth, n // tn),
        in_specs=[
            pl.BlockSpec((bsz, d), lambda l, j: (0, 0)),
            pl.BlockSpec((1, d, tn), lambda l, j: (l, 0, j)),
            pl.BlockSpec((1, 1, tn), lambda l, j: (l, 0, j)),
        ],
        out_specs=pl.BlockSpec((1, bsz, tn), lambda l, j: (l, 0, j)),
        out_shape=jax.ShapeDtypeStruct((depth, bsz, n), F32),
        compiler_params=_params("parallel", "parallel"),
        name="ada",
    )(c, w_ada, b_ada.reshape(depth, 1, n))


def _inproj_kernel(x_ref, gain_ref, shift_ref, scale_ref, w_ref, *out_refs):
    h = _modulate(x_ref[...], gain_ref[...], shift_ref[0], scale_ref[0]).astype(BF16)
    off = 0
    for o_ref in out_refs:
        width = o_ref.shape[1]
        for c0 in range(0, width, 512):
            c1 = min(c0 + 512, width)
            o_ref[:, c0:c1] = _dot(h, w_ref[:, off + c0:off + c1]).astype(o_ref.dtype)
        off += width


def _inproj(x2, gain, shift, scale, w_pack, seq):
    t, d = x2.shape
    tm = 256
    per_b = seq // tm
    widths = (W_A, W_B, W_I, W_C, W_M)
    dtypes = (F32, BF16, F32, F32, BF16)
    return pl.pallas_call(
        _inproj_kernel,
        grid=(t // tm,),
        in_specs=[
            pl.BlockSpec((tm, d), lambda i: (i, 0)),
            pl.BlockSpec((1, d), lambda i: (0, 0)),
            pl.BlockSpec((1, 1, d), lambda i: (i // per_b, 0, 0)),
            pl.BlockSpec((1, 1, d), lambda i: (i // per_b, 0, 0)),
            pl.BlockSpec((d, W_PACK), lambda i: (0, 0), pipeline_mode=pl.Buffered(1)),
        ],
        out_specs=[pl.BlockSpec((tm, w), lambda i: (i, 0)) for w in widths],
        out_shape=[jax.ShapeDtypeStruct((t, w), dt) for w, dt in zip(widths, dtypes)],
        compiler_params=_params("parallel"),
        name="inproj",
    )(x2, gain, shift, scale, w_pack)


def _group_row_bcast(x, group, row):
    n, w = x.shape
    if group >= SUBLANES:
        x3 = x.reshape(n // group, group, w)
        return jnp.broadcast_to(x3[:, row:row + 1, :], x3.shape).reshape(n, w)
    x3 = x.reshape(n // SUBLANES, SUBLANES, w)
    sub = lax.broadcasted_iota(I32, x3.shape, 1)
    n_groups = SUBLANES // group
    res = None
    for g in reversed(range(n_groups)):
        r = g * group + row
        bc = jnp.broadcast_to(x3[:, r:r + 1, :], x3.shape)
        res = bc if res is None else jnp.where(sub < (g + 1) * group, bc, res)
    return res.reshape(n, w)


def _scan_kernel(mode, layer, *refs):
    if mode == "hgrn":
        q_ref, f_ref, v_ref, g_ref, lbl_ref, gain_ref, lv_ref, tri_ref, y_ref, st_ref = refs
    else:
        q_ref, k_ref, v_ref, g_ref, code_ref, gup_ref, gb_ref, gain_ref, lv_ref, tri_ref, y_ref, st_ref = refs
    lt = q_ref.shape[0]
    n_levels = lt.bit_length() - 1

    @pl.when(pl.program_id(2) == 0)
    def _():
        st_ref[...] = jnp.zeros_like(st_ref)

    if mode == "hgrn":
        lbl = lbl_ref[...]
        e = jnp.exp(lbl - jnp.max(lbl, axis=0, keepdims=True))
        p = e / jnp.sum(e, axis=0, keepdims=True)
        lb = jnp.zeros((1, p.shape[1]), F32)
        for l2 in range(1, layer + 1):
            lb = lb + p[l2:l2 + 1, :]
        f = lb + (1.0 - lb) * jax.nn.sigmoid(f_ref[...])
        lg = jnp.log(f)
        kk = 1.0 - f
        q = q_ref[...] * (HEAD_DIM ** -0.5)
    else:
        c_hi, c_lo = _split2(code_ref[...])
        u_hi, u_lo = _split2(gup_ref[...])
        z = _dot(c_hi, u_hi) + _dot(c_hi, u_lo) + _dot(c_lo, u_hi) + gb_ref[...]
        lg = (jnp.minimum(z, 0.0) - jnp.log1p(jnp.exp(-jnp.abs(z)))) * (1.0 / GLA_TAU)
        kk = k_ref[...]
        q = q_ref[...] * (GLA_DK ** -0.5)

    tri = tri_ref[...]
    g_hi, g_mid, g_lo = _split3(lg)
    b = _dot(tri, g_hi) + _dot(tri, g_mid) + _dot(tri, g_lo)

    lv = lv_ref[...]
    s = jnp.where(lv == -1, _dot_nt(q.astype(BF16), kk.astype(BF16)), 0.0)
    for l in range(n_levels):
        n = 1 << l
        bref = _group_row_bcast(b, 2 * n, n - 1)
        ql = (q * jnp.exp(jnp.minimum(b - bref, 0.0))).astype(BF16)
        kl = (kk * jnp.exp(jnp.minimum(bref - b, 0.0))).astype(BF16)
        s = jnp.where(lv == l, _dot_nt(ql, kl), s)

    vb = v_ref[...].astype(BF16)
    st = st_ref[...]
    o = _dot(s.astype(BF16), vb) + _dot_nt((q * jnp.exp(b)).astype(BF16), st.astype(BF16))
    b_last = b[lt - 1:lt, :]
    kd = (kk * jnp.exp(b_last - b)).astype(BF16)
    st_ref[...] = st * jnp.exp(b_last) + _dot_tn(vb, kd)

    ms = jnp.mean(o * o, axis=-1, keepdims=True)
    g = g_ref[...]
    y = o * lax.rsqrt(ms + EPS) * gain_ref[...] * (g * jax.nn.sigmoid(g))
    y_ref[...] = y.astype(y_ref.dtype)


def _scan_consts(lt):
    idx = np.arange(lt)
    x = idx[:, None] ^ idx[None, :]
    lvl = np.floor(np.log2(np.maximum(x, 1))).astype(np.int32)
    lv = np.where(idx[None, :] < idx[:, None], lvl, np.where(x == 0, -1, -2)).astype(np.int32)
    tri = (idx[None, :] <= idx[:, None]).astype(np.float32)
    return jnp.asarray(lv), jnp.asarray(tri, dtype=BF16)


def _scan(mode, layer, z, extra, out_gain, bsz, seq):
    t = z.shape[0]
    lt = 256
    per_b = seq // lt
    lv, tri = _scan_consts(lt)

    def col(cb):
        return pl.BlockSpec((lt, HEAD_DIM), lambda b, h, c: (b * per_b + c, cb * HEADS + h))

    const2 = lambda b, h, c: (0, 0)
    head_col = lambda b, h, c: (0, h)
    in_specs = [col(0), col(1), col(2), col(3)]
    args = [z, z, z, z]
    if mode == "hgrn":
        (lb_logits,) = extra
        in_specs.append(pl.BlockSpec((lb_logits.shape[0], HEAD_DIM), head_col))
        args.append(lb_logits)
    else:
        gup, gb = extra
        in_specs += [pl.BlockSpec((lt, LANES), lambda b, h, c: (b * per_b + c, 4 * HEADS)),
                     pl.BlockSpec((LANES, HEAD_DIM), head_col),
                     pl.BlockSpec((1, HEAD_DIM), head_col)]
        args += [z, gup, gb]
    in_specs += [pl.BlockSpec((1, HEAD_DIM), const2), pl.BlockSpec((lt, lt), const2), pl.BlockSpec((lt, lt), const2)]
    args += [out_gain, lv, tri]
    return pl.pallas_call(
        functools.partial(_scan_kernel, mode, layer),
        grid=(bsz, HEADS, per_b),
        in_specs=in_specs,
        out_specs=pl.BlockSpec((lt, HEAD_DIM), lambda b, h, c: (b * per_b + c, h)),
        out_shape=jax.ShapeDtypeStruct((t, BRANCH_WIDTH), BF16),
        scratch_shapes=[pltpu.VMEM((HEAD_DIM, HEAD_DIM), F32)],
        compiler_params=_params("parallel", "parallel", "arbitrary"),
        name="scan_" + mode,
    )(*args)


_KCH = 256


def _dsa_kernel(topk, q_ref, k_ref, v_ref, ziq_ref, zik_ref, qg_ref, kg_ref, stri_ref, o_ref,
                kn_ref, vt_ref, kih_ref, kil_ref, keys_ref, bias_ref):
    j = pl.program_id(1)
    seq = k_ref.shape[0]

    @pl.when(j == 0)
    def _prep():
        def body(c, carry):
            r0 = pl.multiple_of(c * _KCH, _KCH)
            kc = k_ref[pl.ds(r0, _KCH), :].astype(F32)
            for h in range(HEADS):
                kh = kc[:, h * HEAD_DIM:(h + 1) * HEAD_DIM]
                ms = jnp.mean(kh * kh, axis=-1, keepdims=True)
                kn_ref[pl.ds(r0, _KCH), h * HEAD_DIM:(h + 1) * HEAD_DIM] = (
                    kh * lax.rsqrt(ms + EPS) * kg_ref[...]).astype(BF16)
            vt_ref[c] = v_ref[pl.ds(r0, _KCH), :].astype(F32).T.astype(BF16)
            ki = zik_ref[pl.ds(r0, _KCH), :]
            hi = ki.astype(BF16)
            kih_ref[pl.ds(r0, _KCH), :] = hi
            kil_ref[pl.ds(r0, _KCH), :] = (ki - hi.astype(F32)).astype(BF16)
            return carry
        lax.fori_loop(0, seq // _KCH, body, 0)

    n_ch = (j * QBLOCK + QBLOCK + _KCH - 1) // _KCH
    ziq = ziq_ref[...]
    w_t = ziq[:, 2 * LANES:3 * LANES].T
    qi = []
    for h in range(IDX_HEADS):
        qh = ziq[:, h * IDX_DIM:(h + 1) * IDX_DIM] * (IDX_DIM ** -0.5)
        qi.append(_split2(qh))
    w_rows = [w_t[IDX_DIM + h:IDX_DIM + h + 1, :] * (IDX_HEADS ** -0.5) for h in range(IDX_HEADS)]
    q_pos = j * QBLOCK + lax.broadcasted_iota(I32, (_KCH, QBLOCK), 1)
    row_i = lax.broadcasted_iota(I32, (_KCH, QBLOCK), 0)

    def idx_body(c, carry):
        r0 = pl.multiple_of(c * _KCH, _KCH)
        k_hi = kih_ref[pl.ds(r0, _KCH), :][:, :IDX_DIM]
        k_lo = kil_ref[pl.ds(r0, _KCH), :][:, :IDX_DIM]
        score = jnp.zeros((_KCH, QBLOCK), F32)
        for h in range(IDX_HEADS):
            q_hi, q_lo = qi[h]
            logit = _dot_nt(k_hi, q_hi) + _dot_nt(k_hi, q_lo) + _dot_nt(k_lo, q_hi)
            score = score + w_rows[h] * jnp.maximum(logit, 0.0)
        score = jnp.where(score == 0.0, 0.0, score)
        bits = lax.bitcast_convert_type(score, I32)
        key = jnp.where(bits < 0, bits ^ np.int32(0x7FFFFFFF), bits)
        keys_ref[pl.ds(r0, _KCH), :] = jnp.where(r0 + row_i <= q_pos, key, INT_MIN)
        return carry
    lax.fori_loop(0, n_ch, idx_body, 0)

    def count(pred):
        def body(c, acc):
            r0 = pl.multiple_of(c * _KCH, _KCH)
            m = jnp.where(pred(keys_ref[pl.ds(r0, _KCH), :]), 1, 0).astype(I32)
            return acc + jnp.sum(m.reshape(_KCH // SUBLANES, SUBLANES, QBLOCK), axis=0)
        acc = lax.fori_loop(0, n_ch, body, jnp.zeros((SUBLANES, QBLOCK), I32))
        return jnp.sum(acc, axis=0, keepdims=True)

    def bisect(it, thr):
        cand = thr ^ jnp.left_shift(jnp.int32(1), 31 - it)
        return jnp.where(count(lambda x: x >= cand) >= topk, cand, thr)
    thr = lax.fori_loop(0, 32, bisect, jnp.full((1, QBLOCK), INT_MIN, I32))
    need = (topk - count(lambda x: x > thr)).astype(F32)

    def sel_body(c, seen):
        r0 = pl.multiple_of(c * _KCH, _KCH)
        x = keys_ref[pl.ds(r0, _KCH), :]
        eq = jnp.where(x == thr, 1.0, 0.0)
        rank = seen + _dot(stri_ref[...], eq.astype(BF16))
        tie_ok = jnp.where(x == thr, jnp.where(rank < need, 0.0, NEG_BIG), NEG_BIG)
        bias = jnp.where(x > thr, 0.0, tie_ok)
        bias_ref[pl.ds(r0, _KCH), :] = jnp.where(x == INT_MIN, NEG_BIG, bias)
        return seen + jnp.sum(eq, axis=0, keepdims=True)
    lax.fori_loop(0, n_ch, sel_body, jnp.zeros((1, QBLOCK), F32))

    qf = q_ref[...].astype(F32)
    for h in range(HEADS):
        qh = qf[:, h * HEAD_DIM:(h + 1) * HEAD_DIM]
        ms = jnp.mean(qh * qh, axis=-1, keepdims=True)
        qn = (qh * lax.rsqrt(ms + EPS) * qg_ref[...] * (HEAD_DIM ** -0.5)).astype(BF16)

        def att_body(c, carry):
            m, l, acc = carry
            r0 = pl.multiple_of(c * _KCH, _KCH)
            kc = kn_ref[pl.ds(r0, _KCH), h * HEAD_DIM:(h + 1) * HEAD_DIM]
            sc = _dot_nt(kc, qn) + bias_ref[pl.ds(r0, _KCH), :]
            m_new = jnp.maximum(m, jnp.max(sc, axis=0, keepdims=True))
            p = jnp.exp(sc - m_new)
            alpha = jnp.exp(m - m_new)
            l = alpha * l + jnp.sum(p, axis=0, keepdims=True)
            vt = vt_ref[c][h * HEAD_DIM:(h + 1) * HEAD_DIM, :]
            acc = alpha * acc + _dot(vt, p.astype(BF16))
            return m_new, l, acc
        init = (jnp.full((1, QBLOCK), NEG_BIG, F32), jnp.zeros((1, QBLOCK), F32), jnp.zeros((HEAD_DIM, QBLOCK), F32))
        _, l, acc = lax.fori_loop(0, n_ch, att_body, init)
        o_ref[:, h * HEAD_DIM:(h + 1) * HEAD_DIM] = (acc / l).T.astype(o_ref.dtype)


def _dsa(zb, zi, q_gain, k_gain, bsz, seq):
    t = zb.shape[0]
    topk = min(DSA_MAX_TOPK, seq // 4)
    nq = seq // QBLOCK
    idx = np.arange(_KCH)
    stri = jnp.asarray((idx[None, :] < idx[:, None]).astype(np.float32), dtype=BF16)
    const2 = lambda b, j: (0, 0)
    return pl.pallas_call(
        functools.partial(_dsa_kernel, topk),
        grid=(bsz, nq),
        in_specs=[
            pl.BlockSpec((QBLOCK, BRANCH_WIDTH), lambda b, j: (b * nq + j, 0)),
            pl.BlockSpec((seq, BRANCH_WIDTH), lambda b, j: (b, 1)),
            pl.BlockSpec((seq, BRANCH_WIDTH), lambda b, j: (b, 2)),
            pl.BlockSpec((QBLOCK, W_I), lambda b, j: (b * nq + j, 0)),
            pl.BlockSpec((seq, LANES), lambda b, j: (b, 2)),
            pl.BlockSpec((1, HEAD_DIM), const2),
            pl.BlockSpec((1, HEAD_DIM), const2),
            pl.BlockSpec((_KCH, _KCH), const2),
        ],
        out_specs=pl.BlockSpec((QBLOCK, BRANCH_WIDTH), lambda b, j: (b * nq + j, 0)),
        out_shape=jax.ShapeDtypeStruct((t, BRANCH_WIDTH), BF16),
        scratch_shapes=[
            pltpu.VMEM((seq, BRANCH_WIDTH), BF16),
            pltpu.VMEM((seq // _KCH, BRANCH_WIDTH, _KCH), BF16),
            pltpu.VMEM((seq, LANES), BF16),
            pltpu.VMEM((seq, LANES), BF16),
            pltpu.VMEM((seq, QBLOCK), I32),
            pltpu.VMEM((seq, QBLOCK), F32),
        ],
        compiler_params=_params("parallel", "arbitrary"),
        name="dsa",
    )(zb, zb, zb, zi, zi, q_gain, k_gain, stri)


def _merge_kernel(ya_ref, yb_ref, yc_ref, zm_ref, x_ref, gm_ref, wb_ref, wo_ref, o_ref):
    d = x_ref.shape[1]
    mixed = None
    for g, y_ref in enumerate((ya_ref, yb_ref, yc_ref)):
        up = _dot(y_ref[...], wb_ref[g])
        term = jax.nn.sigmoid(zm_ref[:, g * d:(g + 1) * d].astype(F32)) * up
        mixed = term if mixed is None else mixed + term
    o_ref[...] = x_ref[...] + gm_ref[0] * _dot(mixed.astype(BF16), wo_ref[...])


def _merge(ya, yb, yc, zm, x2, gate_m, w_branch, w_out, seq):
    t, d = x2.shape
    tm = 256
    per_b = seq // tm
    row = lambda i: (i, 0)
    return pl.pallas_call(
        _merge_kernel,
        grid=(t // tm,),
        in_specs=[
            pl.BlockSpec((tm, BRANCH_WIDTH), row), pl.BlockSpec((tm, BRANCH_WIDTH), row),
            pl.BlockSpec((tm, BRANCH_WIDTH), row), pl.BlockSpec((tm, W_M), row), pl.BlockSpec((tm, d), row),
            pl.BlockSpec((1, 1, d), lambda i: (i // per_b, 0, 0)),
            pl.BlockSpec(w_branch.shape, lambda i: (0, 0, 0)),
            pl.BlockSpec(w_out.shape, lambda i: (0, 0)),
        ],
        out_specs=pl.BlockSpec((tm, d), row),
        out_shape=jax.ShapeDtypeStruct((t, d), F32),
        compiler_params=_params("parallel"),
        name="merge",
    )(ya, yb, yc, zm, x2, gate_m, w_branch, w_out)


_CAND_ROWS = 72


def _cand_layout():
    flat = np.zeros((_CAND_ROWS,), np.int32)
    live = np.zeros((_CAND_ROWS,), bool)
    r = 0
    for b in range(16):
        flat[r], live[r] = b, True
        r += 1
    for a in (1, 2, 3):
        for b in range(8):
            flat[r], live[r] = a * 16 + b, True
            r += 1
    for a in range(16):
        flat[r], live[r] = a * 16, a >= 4
        r += 1
    for a in range(8):
        flat[r], live[r] = a * 16 + 1, a >= 4
        r += 1
    for a in range(8):
        flat[r], live[r] = a * 16 + 2, a == 4
        r += 1
    flat = np.where(live, flat, 1000 + np.arange(_CAND_ROWS))
    return flat.astype(np.int32), live


def _top16(s, row_i, vals_ref):
    work = s
    rank = jnp.full(s.shape, 127.0, F32)
    for a in range(PEER_TOPK):
        m = jnp.max(work, axis=0, keepdims=True)
        idx = jnp.min(jnp.where(work == m, row_i, N_KEYS), axis=0, keepdims=True)
        hit = row_i == idx
        work = jnp.where(hit, -jnp.inf, work)
        rank = jnp.where(hit, float(a), rank)
        vals_ref[a:a + 1, :] = m
    return rank


def _route_kernel(x_ref, gain_ref, shift_ref, scale_ref, wq_ref, sk_ref, flat_ref, live_ref,
                  ht_ref, a1_ref, cnt_ref, b2_ref, r2_ref, qt_ref, v1_ref, v2_ref):
    tt = x_ref.shape[0]
    h2 = _modulate(x_ref[...], gain_ref[...], shift_ref[0], scale_ref[0])
    ht = h2.T.astype(BF16)
    ht_ref[...] = ht
    qt_ref[...] = _dot(wq_ref[...], ht)
    row_i = lax.broadcasted_iota(I32, (N_KEYS, tt), 0)
    flat = flat_ref[...]
    live = live_ref[...] > 0.0

    for h in range(PEER_HEADS):
        s1 = _dot(sk_ref[2 * h], qt_ref[(2 * h) * N_KEYS:(2 * h + 1) * N_KEYS, :].astype(BF16))
        r1 = _top16(s1, row_i, v1_ref)
        a1_ref[h] = jnp.exp(s1 - v1_ref[0:1, :])
        s2 = _dot(sk_ref[2 * h + 1], qt_ref[(2 * h + 1) * N_KEYS:(2 * h + 2) * N_KEYS, :].astype(BF16))
        r2 = _top16(s2, row_i, v2_ref)
        r2_ref[h] = r2
        b2u = jnp.exp(s2 - v2_ref[0:1, :])

        v1 = v1_ref[...]
        v2 = v2_ref[...]
        cand = jnp.concatenate(
            [v1[0:1, :] + v2] + [v1[a:a + 1, :] + v2[0:8, :] for a in (1, 2, 3)]
            + [v1 + v2[0:1, :], v1[0:8, :] + v2[1:2, :], v1[0:8, :] + v2[2:3, :]], axis=0)
        cand = jnp.where(live, cand, -jnp.inf)
        top = v1[0:1, :] + v2[0:1, :]
        ex = jnp.exp(cand - top)
        sel = jnp.zeros(cand.shape, F32)
        work = cand
        for _ in range(PEER_TOPK):
            m = jnp.max(work, axis=0, keepdims=True)
            fi = jnp.min(jnp.where(work == m, flat, 9999), axis=0, keepdims=True)
            hit = flat == fi
            work = jnp.where(hit, -jnp.inf, work)
            sel = jnp.where(hit, 1.0, sel)
        z = jnp.sum(sel * ex, axis=0, keepdims=True)
        b2_ref[h] = b2u / z

        zeros8 = jnp.zeros((8, tt), F32)
        cnt_hi = (sel[40:56, :] + jnp.concatenate([sel[56:64, :], zeros8], axis=0)
                  + jnp.concatenate([sel[64:72, :], zeros8], axis=0))
        cnt_lo = [jnp.sum(sel[0:16, :], axis=0, keepdims=True)] + [
            jnp.sum(sel[16 + 8 * (a - 1):24 + 8 * (a - 1), :], axis=0, keepdims=True) for a in (1, 2, 3)]
        cnt = jnp.zeros(s1.shape, F32)
        for a in range(PEER_TOPK):
            ca = cnt_lo[a] if a < 4 else cnt_hi[a:a + 1, :]
            cnt = jnp.where(r1 == float(a), ca, cnt)
        cnt_ref[h] = cnt


def _route(x2, gain, shift, scale, wq_t, sub_keys, seq):
    t, d = x2.shape
    tt = 256
    per_b = seq // tt
    flat_np, live_np = _cand_layout()
    flat = jnp.asarray(np.broadcast_to(flat_np[:, None], (_CAND_ROWS, tt)).copy())
    live = jnp.asarray(np.broadcast_to(live_np[:, None], (_CAND_ROWS, tt)).astype(np.float32))
    dense = jax.ShapeDtypeStruct((PEER_HEADS, N_KEYS, t), F32)
    dense_spec = pl.BlockSpec((PEER_HEADS, N_KEYS, tt), lambda i: (0, 0, i))
    const2 = lambda i: (0, 0)
    return pl.pallas_call(
        _route_kernel,
        grid=(t // tt,),
        in_specs=[
            pl.BlockSpec((tt, d), lambda i: (i, 0)),
            pl.BlockSpec((1, d), const2),
            pl.BlockSpec((1, 1, d), lambda i: (i // per_b, 0, 0)),
            pl.BlockSpec((1, 1, d), lambda i: (i // per_b, 0, 0)),
            pl.BlockSpec(wq_t.shape, const2),
            pl.BlockSpec(sub_keys.shape, lambda i: (0, 0, 0)),
            pl.BlockSpec((_CAND_ROWS, tt), const2),
            pl.BlockSpec((_CAND_ROWS, tt), const2),
        ],
        out_specs=[pl.BlockSpec((d, tt), lambda i: (0, i)), dense_spec, dense_spec, dense_spec, dense_spec],
        out_shape=[jax.ShapeDtypeStruct((d, t), BF16), dense, dense, dense, dense],
        scratch_shapes=[
            pltpu.VMEM((PEER_HEADS * 2 * N_KEYS, tt), F32),
            pltpu.VMEM((PEER_TOPK, tt), F32),
            pltpu.VMEM((PEER_TOPK, tt), F32),
        ],
        compiler_params=_params("parallel"),
        name="peer_route",
    )(x2, gain, shift, scale, wq_t, sub_keys, flat, live)


def _peer_kernel(ht_ref, u_ref, vt_ref, a1_ref, cnt_ref, b2_ref, r2_ref, x_ref, gf_ref, o_ref,
                 acc_ref, at_ref, gt_ref):
    e = pl.program_id(1)
    te, tt = at_ref.shape
    n_i = te // N_KEYS

    @pl.when(e == 0)
    def _():
        acc_ref[...] = jnp.zeros_like(acc_ref)

    at_ref[...] = _dot(u_ref[...], ht_ref[...])

    assert n_i == SUBLANES
    i0 = pl.multiple_of(e * n_i, SUBLANES)
    for tb in range(tt // LANES):
        lanes = slice(tb * LANES, (tb + 1) * LANES)
        a1_rows = [a1_ref[h, pl.ds(i0, n_i), lanes] for h in range(PEER_HEADS)]
        cnt_rows = [cnt_ref[h, pl.ds(i0, n_i), lanes] for h in range(PEER_HEADS)]
        for ii in range(n_i):
            rows = slice(ii * N_KEYS, (ii + 1) * N_KEYS)
            a = at_ref[rows, lanes]
            act = 0.5 * a * (1.0 + lax.erf(a * np.float32(np.sqrt(0.5))))
            w = jnp.zeros((N_KEYS, LANES), F32)
            for h in range(PEER_HEADS):
                sel = r2_ref[h, :, lanes] < cnt_rows[h][ii:ii + 1, :]
                w = w + jnp.where(sel, a1_rows[h][ii:ii + 1, :] * b2_ref[h, :, lanes], 0.0)
            gt_ref[rows, lanes] = (w * act).astype(BF16)

    acc_ref[...] += _dot(vt_ref[...], gt_ref[...])

    @pl.when(e == pl.num_programs(1) - 1)
    def _():
        o_ref[...] = x_ref[...] + gf_ref[0] * acc_ref[...].T


def _peer(ht, u, vt, a1, cnt, b2, r2, x2, gate_f, seq):
    t, d = x2.shape
    n_exp = u.shape[0]
    tt = 512
    te = 1024
    per_b = seq // tt
    dense_spec = pl.BlockSpec((PEER_HEADS, N_KEYS, tt), lambda i, e: (0, 0, i))
    return pl.pallas_call(
        _peer_kernel,
        grid=(t // tt, n_exp // te),
        in_specs=[
            pl.BlockSpec((d, tt), lambda i, e: (0, i)),
            pl.BlockSpec((te, d), lambda i, e: (e, 0)),
            pl.BlockSpec((d, te), lambda i, e: (0, e)),
            dense_spec, dense_spec, dense_spec, dense_spec,
            pl.BlockSpec((tt, d), lambda i, e: (i, 0)),
            pl.BlockSpec((1, 1, d), lambda i, e: (i // per_b, 0, 0)),
        ],
        out_specs=pl.BlockSpec((tt, d), lambda i, e: (i, 0)),
        out_shape=jax.ShapeDtypeStruct((t, d), F32),
        scratch_shapes=[
            pltpu.VMEM((d, tt), F32),
            pltpu.VMEM((te, tt), F32),
            pltpu.VMEM((te, tt), BF16),
        ],
        compiler_params=_params("parallel", "arbitrary"),
        name="peer_experts",
    )(ht, u, vt, a1, cnt, b2, r2, x2, gate_f)


def _pad_heads(w, width):
    d = w.shape[0]
    w = w.reshape(d, HEADS, width)
    return jnp.pad(w, ((0, 0), (0, 0), (0, HEAD_DIM - width))).reshape(d, HEADS * HEAD_DIM)


def _pack_w_in(w):
    d = w.shape[0]
    o = 0
    a = w[:, o:o + W_A]; o += W_A
    b = w[:, o:o + W_B]; o += W_B
    n_idx = IDX_HEADS * IDX_DIM + IDX_DIM + IDX_HEADS
    i = jnp.pad(w[:, o:o + n_idx], ((0, 0), (0, W_I - n_idx))); o += n_idx
    cq = _pad_heads(w[:, o:o + HEADS * GLA_DK], GLA_DK); o += HEADS * GLA_DK
    ck = _pad_heads(w[:, o:o + HEADS * GLA_DK], GLA_DK); o += HEADS * GLA_DK
    cv = w[:, o:o + BRANCH_WIDTH]; o += BRANCH_WIDTH
    code = jnp.pad(w[:, o:o + GLA_GATE_RANK], ((0, 0), (0, LANES - GLA_GATE_RANK))); o += GLA_GATE_RANK
    cg = w[:, o:o + BRANCH_WIDTH]; o += BRANCH_WIDTH
    m = w[:, o:o + W_M]
    return jnp.concatenate([a, b, i, cq, ck, cv, cg, code, m], axis=1).astype(BF16)


def kernel(x, c, w_ada, b_ada, norm_mix, norm_ffn, w_in, hgrn_lb_logits, hgrn_out_norm, dsa_q_norm, dsa_k_norm,
           gla_gate_up, gla_gate_bias, gla_out_norm, w_branch, w_out, peer_w_query, peer_sub_keys, peer_u, peer_v):
    bsz, seq, d = x.shape
    depth = w_in.shape[0]
    t = bsz * seq
    x2 = x.reshape(t, d)
    mod = _ada(c, w_ada, b_ada)

    for l in range(depth):
        shift_m, scale_m, gate_m, shift_f, scale_f, gate_f = [
            mod[l, :, k * d:(k + 1) * d].reshape(bsz, 1, d) for k in range(6)]
        za, zb, zi, zc, zm = _inproj(x2, norm_mix[l].reshape(1, d), shift_m, scale_m, _pack_w_in(w_in[l]), seq)

        ya = _scan("hgrn", l, za, (hgrn_lb_logits,), hgrn_out_norm[l].reshape(1, HEAD_DIM), bsz, seq)
        yb = _dsa(zb, zi, dsa_q_norm[l].reshape(1, HEAD_DIM), dsa_k_norm[l].reshape(1, HEAD_DIM), bsz, seq)
        gup = jnp.pad(_pad_heads(gla_gate_up[l], GLA_DK), ((0, LANES - GLA_GATE_RANK), (0, 0)))
        gb = _pad_heads(gla_gate_bias[l].reshape(1, HEADS * GLA_DK), GLA_DK)
        yc = _scan("gla", l, zc, (gup, gb), gla_out_norm[l].reshape(1, HEAD_DIM), bsz, seq)

        x2 = _merge(ya, yb, yc, zm, x2, gate_m, w_branch[l].astype(BF16), w_out[l].astype(BF16), seq)

        ht, a1, cnt, b2, r2 = _route(
            x2, norm_ffn[l].reshape(1, d), shift_f, scale_f, peer_w_query[l].T.astype(BF16),
            peer_sub_keys[l].reshape(PEER_HEADS * 2, N_KEYS, -1).astype(BF16), seq)
        x2 = _peer(ht, peer_u[l].astype(BF16), peer_v[l].T.astype(BF16), a1, cnt, b2, r2, x2, gate_f, seq)

    return x2.reshape(bsz, seq, d)
```

```python
import functools

import numpy as np
import jax
import jax.numpy as jnp
from jax import lax
from jax.experimental import pallas as pl
from jax.experimental.pallas import tpu as pltpu

F32, BF16, I32 = jnp.float32, jnp.bfloat16, jnp.int32

D_MODEL = 1024
HEADS = 4
HEAD_DIM = 128
BRANCH_WIDTH = HEADS * HEAD_DIM
IDX_HEADS = 4
IDX_DIM = 64
DSA_MAX_TOPK = 256
QBLOCK = 128
GLA_DK = 64
GLA_GATE_RANK = 16
GLA_TAU = 16.0
PEER_HEADS = 8
N_KEYS = 128
PEER_TOPK = 16
PEER_TE = 1024
EPS = 1e-6

LANES = 128
SUBLANES = 8
VMEM_LIMIT_BYTES = 56 * 1024 * 1024

NEG_BIG = -1e30
INT_MIN = np.int32(-2 ** 31)

W_A = 4 * BRANCH_WIDTH
W_B = 3 * BRANCH_WIDTH
W_I = 3 * LANES
W_C = 4 * BRANCH_WIDTH + LANES
W_M = 3 * D_MODEL
W_PACK = W_A + W_B + W_I + W_C + W_M


def _dot(a, b):
    return jnp.dot(a, b, preferred_element_type=F32)


def _dot_nt(a, b):
    return lax.dot_general(a, b, (((1,), (1,)), ((), ())), preferred_element_type=F32)


def _dot_tn(a, b):
    return lax.dot_general(a, b, (((0,), (0,)), ((), ())), preferred_element_type=F32)


def _split2(x):
    hi = x.astype(BF16)
    lo = (x - hi.astype(F32)).astype(BF16)
    return hi, lo


def _split3(x):
    hi = x.astype(BF16)
    r = x - hi.astype(F32)
    mid = r.astype(BF16)
    lo = (r - mid.astype(F32)).astype(BF16)
    return hi, mid, lo


def _params(*sem):
    return pltpu.CompilerParams(dimension_semantics=sem, vmem_limit_bytes=VMEM_LIMIT_BYTES)


def _modulate(x, gain, shift, scale):
    ms = jnp.mean(x * x, axis=-1, keepdims=True)
    return x * lax.rsqrt(ms + EPS) * gain * (1.0 + scale) + shift


def _ada_kernel(c_ref, w_ref, b_ref, o_ref):
    c = c_ref[...]
    sc = c * jax.nn.sigmoid(c)
    a_hi, a_lo = _split2(sc)
    w_hi, w_lo = _split2(w_ref[0])
    o_ref[0] = _dot(a_hi, w_hi) + _dot(a_hi, w_lo) + _dot(a_lo, w_hi) + b_ref[0]


def _ada(c, w_ada, b_ada):
    depth, d, n = w_ada.shape
    bsz = c.shape[0]
    tn = 1536
    return pl.pallas_call(
        _ada_kernel,
        grid=(depth, n // tn),
        in_specs=[
            pl.BlockSpec((bsz, d), lambda l, j: (0, 0)),
            pl.BlockSpec((1, d, tn), lambda l, j: (l, 0, j)),
            pl.BlockSpec((1, 1, tn), lambda l, j: (l, 0, j)),
        ],
        out_specs=pl.BlockSpec((1, bsz, tn), lambda l, j: (l, 0, j)),
        out_shape=jax.ShapeDtypeStruct((depth, bsz, n), F32),
        compiler_params=_params("parallel", "parallel"),
        name="ada",
    )(c, w_ada, b_ada.reshape(depth, 1, n))


def _inproj_kernel(x_ref, gain_ref, shift_ref, scale_ref, w_ref, *out_refs):
    h = _modulate(x_ref[...], gain_ref[...], shift_ref[0], scale_ref[0]).astype(BF16)
    off = 0
    for o_ref in out_refs:
        width = o_ref.shape[1]
        for c0 in range(0, width, 512):
            c1 = min(c0 + 512, width)
            o_ref[:, c0:c1] = _dot(h, w_ref[:, off + c0:off + c1]).astype(o_ref.dtype)
        off += width


def _inproj(x2, gain, shift, scale, w_pack, seq):
    t, d = x2.shape
    tm = 256
    per_b = seq // tm
    widths = (W_A, W_B, W_I, W_C, W_M)
    dtypes = (F32, BF16, F32, F32, BF16)
    return pl.pallas_call(
        _inproj_kernel,
        grid=(t // tm,),
        in_specs=[
            pl.BlockSpec((tm, d), lambda i: (i, 0)),
            pl.BlockSpec((1, d), lambda i: (0, 0)),
            pl.BlockSpec((1, 1, d), lambda i: (i // per_b, 0, 0)),
            pl.BlockSpec((1, 1, d), lambda i: (i // per_b, 0, 0)),
            pl.BlockSpec((d, W_PACK), lambda i: (0, 0), pipeline_mode=pl.Buffered(1)),
        ],
        out_specs=[pl.BlockSpec((tm, w), lambda i: (i, 0)) for w in widths],
        out_shape=[jax.ShapeDtypeStruct((t, w), dt) for w, dt in zip(widths, dtypes)],
        compiler_params=_params("parallel"),
        name="inproj",
    )(x2, gain, shift, scale, w_pack)


def _group_row_bcast(x, group, row):
    n, w = x.shape
    if group >= SUBLANES:
        x3 = x.reshape(n // group, group, w)
        return jnp.broadcast_to(x3[:, row:row + 1, :], x3.shape).reshape(n, w)
    x3 = x.reshape(n // SUBLANES, SUBLANES, w)
    sub = lax.broadcasted_iota(I32, x3.shape, 1)
    n_groups = SUBLANES // group
    res = None
    for g in reversed(range(n_groups)):
        r = g * group + row
        bc = jnp.broadcast_to(x3[:, r:r + 1, :], x3.shape)
        res = bc if res is None else jnp.where(sub < (g + 1) * group, bc, res)
    return res.reshape(n, w)


def _scan_kernel(mode, layer, *refs):
    if mode == "hgrn":
        q_ref, f_ref, v_ref, g_ref, lbl_ref, gain_ref, lv_ref, tri_ref, y_ref, st_ref = refs
    else:
        q_ref, k_ref, v_ref, g_ref, code_ref, gup_ref, gb_ref, gain_ref, lv_ref, tri_ref, y_ref, st_ref = refs
    lt = q_ref.shape[0]
    n_levels = lt.bit_length() - 1

    @pl.when(pl.program_id(2) == 0)
    def _():
        st_ref[...] = jnp.zeros_like(st_ref)

    if mode == "hgrn":
        lbl = lbl_ref[...]
        e = jnp.exp(lbl - jnp.max(lbl, axis=0, keepdims=True))
        p = e / jnp.sum(e, axis=0, keepdims=True)
        lb = jnp.zeros((1, p.shape[1]), F32)
        for l2 in range(1, layer + 1):
            lb = lb + p[l2:l2 + 1, :]
        f = lb + (1.0 - lb) * jax.nn.sigmoid(f_ref[...])
        lg = jnp.log(f)
        kk = 1.0 - f
        q = q_ref[...] * (HEAD_DIM ** -0.5)
    else:
        c_hi, c_lo = _split2(code_ref[...])
        u_hi, u_lo = _split2(gup_ref[...])
        z = _dot(c_hi, u_hi) + _dot(c_hi, u_lo) + _dot(c_lo, u_hi) + gb_ref[...]
        lg = (jnp.minimum(z, 0.0) - jnp.log1p(jnp.exp(-jnp.abs(z)))) * (1.0 / GLA_TAU)
        kk = k_ref[...]
        q = q_ref[...] * (GLA_DK ** -0.5)

    tri = tri_ref[...]
    g_hi, g_mid, g_lo = _split3(lg)
    b = _dot(tri, g_hi) + _dot(tri, g_mid) + _dot(tri, g_lo)

    lv = lv_ref[...]
    s = jnp.where(lv == -1, _dot_nt(q.astype(BF16), kk.astype(BF16)), 0.0)
    for l in range(n_levels):
        n = 1 << l
        bref = _group_row_bcast(b, 2 * n, n - 1)
        ql = (q * jnp.exp(jnp.minimum(b - bref, 0.0))).astype(BF16)
        kl = (kk * jnp.exp(jnp.minimum(bref - b, 0.0))).astype(BF16)
        s = jnp.where(lv == l, _dot_nt(ql, kl), s)

    vb = v_ref[...].astype(BF16)
    st = st_ref[...]
    o = _dot(s.astype(BF16), vb) + _dot_nt((q * jnp.exp(b)).astype(BF16), st.astype(BF16))
    b_last = b[lt - 1:lt, :]
    kd = (kk * jnp.exp(b_last - b)).astype(BF16)
    st_ref[...] = st * jnp.exp(b_last) + _dot_tn(vb, kd)

    ms = jnp.mean(o * o, axis=-1, keepdims=True)
    g = g_ref[...]
    y = o * lax.rsqrt(ms + EPS) * gain_ref[...] * (g * jax.nn.sigmoid(g))
    y_ref[...] = y.astype(y_ref.dtype)


def _scan_consts(lt):
    idx = np.arange(lt)
    x = idx[:, None] ^ idx[None, :]
    lvl = np.floor(np.log2(np.maximum(x, 1))).astype(np.int32)
    lv = np.where(idx[None, :] < idx[:, None], lvl, np.where(x == 0, -1, -2)).astype(np.int32)
    tri = (idx[None, :] <= idx[:, None]).astype(np.float32)
    return jnp.asarray(lv), jnp.asarray(tri, dtype=BF16)


def _scan(mode, layer, z, extra, out_gain, bsz, seq):
    t = z.shape[0]
    lt = 256
    per_b = seq // lt
    lv, tri = _scan_consts(lt)

    def col(cb):
        return pl.BlockSpec((lt, HEAD_DIM), lambda b, h, c: (b * per_b + c, cb * HEADS + h))

    const2 = lambda b, h, c: (0, 0)
    head_col = lambda b, h, c: (0, h)
    in_specs = [col(0), col(1), col(2), col(3)]
    args = [z, z, z, z]
    if mode == "hgrn":
        (lb_logits,) = extra
        in_specs.append(pl.BlockSpec((lb_logits.shape[0], HEAD_DIM), head_col))
        args.append(lb_logits)
    else:
        gup, gb = extra
        in_specs += [pl.BlockSpec((lt, LANES), lambda b, h, c: (b * per_b + c, 4 * HEADS)),
                     pl.BlockSpec((LANES, HEAD_DIM), head_col),
                     pl.BlockSpec((1, HEAD_DIM), head_col)]
        args += [z, gup, gb]
    in_specs += [pl.BlockSpec((1, HEAD_DIM), const2), pl.BlockSpec((lt, lt), const2), pl.BlockSpec((lt, lt), const2)]
    args += [out_gain, lv, tri]
    return pl.pallas_call(
        functools.partial(_scan_kernel, mode, layer),
        grid=(bsz, HEADS, per_b),
        in_specs=in_specs,
        out_specs=pl.BlockSpec((lt, HEAD_DIM), lambda b, h, c: (b * per_b + c, h)),
        out_shape=jax.ShapeDtypeStruct((t, BRANCH_WIDTH), BF16),
        scratch_shapes=[pltpu.VMEM((HEAD_DIM, HEAD_DIM), F32)],
        compiler_params=_params("parallel", "parallel", "arbitrary"),
        name="scan_" + mode,
    )(*args)


_KCH = 256


def _dsa_kernel(topk, q_ref, k_ref, v_ref, ziq_ref, zik_ref, qg_ref, kg_ref, stri_ref, o_ref,
                kn_ref, vt_ref, kih_ref, kil_ref, keys_ref, bias_ref, acc_ref):
    j = pl.program_id(1)
    seq = k_ref.shape[0]

    @pl.when(j == 0)
    def _prep():
        def body(c, carry):
            r0 = pl.multiple_of(c * _KCH, _KCH)
            kc = k_ref[pl.ds(r0, _KCH), :].astype(F32)
            for h in range(HEADS):
                kh = kc[:, h * HEAD_DIM:(h + 1) * HEAD_DIM]
                ms = jnp.mean(kh * kh, axis=-1, keepdims=True)
                kn_ref[pl.ds(r0, _KCH), h * HEAD_DIM:(h + 1) * HEAD_DIM] = (
                    kh * lax.rsqrt(ms + EPS) * kg_ref[...]).astype(BF16)
            vt_ref[c] = v_ref[pl.ds(r0, _KCH), :].astype(F32).T.astype(BF16)
            ki = zik_ref[pl.ds(r0, _KCH), :]
            hi = ki.astype(BF16)
            kih_ref[pl.ds(r0, _KCH), :] = hi
            kil_ref[pl.ds(r0, _KCH), :] = (ki - hi.astype(F32)).astype(BF16)
            return carry
        lax.fori_loop(0, seq // _KCH, body, 0)

    n_ch = (j * QBLOCK + QBLOCK + _KCH - 1) // _KCH
    ziq = ziq_ref[...]
    w_t = ziq[:, 2 * LANES:3 * LANES].T
    qi = jnp.concatenate([ziq[:, h * IDX_DIM:(h + 1) * IDX_DIM] for h in range(IDX_HEADS)], axis=0)
    qi_hi, qi_lo = _split2(qi * (IDX_DIM ** -0.5))
    w_rows = [w_t[IDX_DIM + h:IDX_DIM + h + 1, :] * (IDX_HEADS ** -0.5) for h in range(IDX_HEADS)]
    q_pos = j * QBLOCK + lax.broadcasted_iota(I32, (_KCH, QBLOCK), 1)
    row_i = lax.broadcasted_iota(I32, (_KCH, QBLOCK), 0)

    def idx_body(c, carry):
        r0 = pl.multiple_of(c * _KCH, _KCH)
        k_hi = kih_ref[pl.ds(r0, _KCH), :][:, :IDX_DIM]
        k_lo = kil_ref[pl.ds(r0, _KCH), :][:, :IDX_DIM]
        logit = _dot_nt(k_hi, qi_hi) + _dot_nt(k_hi, qi_lo) + _dot_nt(k_lo, qi_hi)
        score = jnp.zeros((_KCH, QBLOCK), F32)
        for h in range(IDX_HEADS):
            score = score + w_rows[h] * jnp.maximum(logit[:, h * QBLOCK:(h + 1) * QBLOCK], 0.0)
        score = jnp.where(score == 0.0, 0.0, score)
        bits = lax.bitcast_convert_type(score, I32)
        key = jnp.where(bits < 0, bits ^ np.int32(0x7FFFFFFF), bits)
        keys_ref[pl.ds(r0, _KCH), :] = jnp.where(r0 + row_i <= q_pos, key, INT_MIN)
        return carry
    lax.fori_loop(0, n_ch, idx_body, 0)

    def count(pred):
        def body(c, acc):
            r0 = pl.multiple_of(c * _KCH, _KCH)
            m = jnp.where(pred(keys_ref[pl.ds(r0, _KCH), :]), 1, 0).astype(I32)
            return acc + jnp.sum(m.reshape(_KCH // SUBLANES, SUBLANES, QBLOCK), axis=0)
        acc = lax.fori_loop(0, n_ch, body, jnp.zeros((SUBLANES, QBLOCK), I32))
        return jnp.sum(acc, axis=0, keepdims=True)

    def bisect(it, thr):
        cand = thr ^ jnp.left_shift(jnp.int32(1), 31 - it)
        return jnp.where(count(lambda x: x >= cand) >= topk, cand, thr)
    thr = lax.fori_loop(0, 32, bisect, jnp.full((1, QBLOCK), INT_MIN, I32))
    need = (topk - count(lambda x: x > thr)).astype(F32)

    def sel_body(c, seen):
        r0 = pl.multiple_of(c * _KCH, _KCH)
        x = keys_ref[pl.ds(r0, _KCH), :]
        eq = jnp.where(x == thr, 1.0, 0.0)
        rank = seen + _dot(stri_ref[...], eq.astype(BF16))
        tie_ok = jnp.where(x == thr, jnp.where(rank < need, 0.0, NEG_BIG), NEG_BIG)
        bias = jnp.where(x > thr, 0.0, tie_ok)
        bias_ref[pl.ds(r0, _KCH), :] = jnp.where(x == INT_MIN, NEG_BIG, bias)
        return seen + jnp.sum(eq, axis=0, keepdims=True)
    lax.fori_loop(0, n_ch, sel_body, jnp.zeros((1, QBLOCK), F32))

    qf = q_ref[...].astype(F32)
    qn = []
    for h in range(HEADS):
        qh = qf[:, h * HEAD_DIM:(h + 1) * HEAD_DIM]
        ms = jnp.mean(qh * qh, axis=-1, keepdims=True)
        qn.append((qh * lax.rsqrt(ms + EPS) * qg_ref[...] * (HEAD_DIM ** -0.5)).astype(BF16))
    acc_ref[...] = jnp.zeros_like(acc_ref)

    def att_body(c, carry):
        r0 = pl.multiple_of(c * _KCH, _KCH)
        bias = bias_ref[pl.ds(r0, _KCH), :]
        out = []
        for h in range(HEADS):
            m, l = carry[h]
            kc = kn_ref[pl.ds(r0, _KCH), h * HEAD_DIM:(h + 1) * HEAD_DIM]
            sc = _dot_nt(kc, qn[h]) + bias
            m_new = jnp.maximum(m, jnp.max(sc, axis=0, keepdims=True))
            p = jnp.exp(sc - m_new)
            alpha = jnp.exp(m - m_new)
            vt = vt_ref[c, h * HEAD_DIM:(h + 1) * HEAD_DIM, :]
            acc_ref[h] = alpha * acc_ref[h] + _dot(vt, p.astype(BF16))
            out.append((m_new, alpha * l + jnp.sum(p, axis=0, keepdims=True)))
        return tuple(out)
    init = tuple((jnp.full((1, QBLOCK), NEG_BIG, F32), jnp.zeros((1, QBLOCK), F32)) for _ in range(HEADS))
    stats = lax.fori_loop(0, n_ch, att_body, init)
    for h in range(HEADS):
        o_ref[:, h * HEAD_DIM:(h + 1) * HEAD_DIM] = (acc_ref[h] / stats[h][1]).T.astype(o_ref.dtype)


def _dsa(zb, zi, q_gain, k_gain, bsz, seq):
    t = zb.shape[0]
    topk = min(DSA_MAX_TOPK, seq // 4)
    nq = seq // QBLOCK
    idx = np.arange(_KCH)
    stri = jnp.asarray((idx[None, :] < idx[:, None]).astype(np.float32), dtype=BF16)
    const2 = lambda b, j: (0, 0)
    return pl.pallas_call(
        functools.partial(_dsa_kernel, topk),
        grid=(bsz, nq),
        in_specs=[
            pl.BlockSpec((QBLOCK, BRANCH_WIDTH), lambda b, j: (b * nq + j, 0)),
            pl.BlockSpec((seq, BRANCH_WIDTH), lambda b, j: (b, 1)),
            pl.BlockSpec((seq, BRANCH_WIDTH), lambda b, j: (b, 2)),
            pl.BlockSpec((QBLOCK, W_I), lambda b, j: (b * nq + j, 0)),
            pl.BlockSpec((seq, LANES), lambda b, j: (b, 2)),
            pl.BlockSpec((1, HEAD_DIM), const2),
            pl.BlockSpec((1, HEAD_DIM), const2),
            pl.BlockSpec((_KCH, _KCH), const2),
        ],
        out_specs=pl.BlockSpec((QBLOCK, BRANCH_WIDTH), lambda b, j: (b * nq + j, 0)),
        out_shape=jax.ShapeDtypeStruct((t, BRANCH_WIDTH), BF16),
        scratch_shapes=[
            pltpu.VMEM((seq, BRANCH_WIDTH), BF16),
            pltpu.VMEM((seq // _KCH, BRANCH_WIDTH, _KCH), BF16),
            pltpu.VMEM((seq, LANES), BF16),
            pltpu.VMEM((seq, LANES), BF16),
            pltpu.VMEM((seq, QBLOCK), I32),
            pltpu.VMEM((seq, QBLOCK), F32),
            pltpu.VMEM((HEADS, HEAD_DIM, QBLOCK), F32),
        ],
        compiler_params=_params("parallel", "arbitrary"),
        name="dsa",
    )(zb, zb, zb, zi, zi, q_gain, k_gain, stri)


def _merge_kernel(ya_ref, yb_ref, yc_ref, zm_ref, x_ref, gm_ref, wb_ref, wo_ref, o_ref):
    d = x_ref.shape[1]
    mixed = None
    for g, y_ref in enumerate((ya_ref, yb_ref, yc_ref)):
        up = _dot(y_ref[...], wb_ref[g])
        term = jax.nn.sigmoid(zm_ref[:, g * d:(g + 1) * d].astype(F32)) * up
        mixed = term if mixed is None else mixed + term
    o_ref[...] = x_ref[...] + gm_ref[0] * _dot(mixed.astype(BF16), wo_ref[...])


def _merge(ya, yb, yc, zm, x2, gate_m, w_branch, w_out, seq):
    t, d = x2.shape
    tm = 256
    per_b = seq // tm
    row = lambda i: (i, 0)
    return pl.pallas_call(
        _merge_kernel,
        grid=(t // tm,),
        in_specs=[
            pl.BlockSpec((tm, BRANCH_WIDTH), row), pl.BlockSpec((tm, BRANCH_WIDTH), row),
            pl.BlockSpec((tm, BRANCH_WIDTH), row), pl.BlockSpec((tm, W_M), row), pl.BlockSpec((tm, d), row),
            pl.BlockSpec((1, 1, d), lambda i: (i // per_b, 0, 0)),
            pl.BlockSpec(w_branch.shape, lambda i: (0, 0, 0)),
            pl.BlockSpec(w_out.shape, lambda i: (0, 0)),
        ],
        out_specs=pl.BlockSpec((tm, d), row),
        out_shape=jax.ShapeDtypeStruct((t, d), F32),
        compiler_params=_params("parallel"),
        name="merge",
    )(ya, yb, yc, zm, x2, gate_m, w_branch, w_out)


_CAND_ROWS = 72


def _cand_layout():
    flat = np.zeros((_CAND_ROWS,), np.int32)
    live = np.zeros((_CAND_ROWS,), bool)
    r = 0
    for b in range(16):
        flat[r], live[r] = b, True
        r += 1
    for a in (1, 2, 3):
        for b in range(8):
            flat[r], live[r] = a * 16 + b, True
            r += 1
    for a in range(16):
        flat[r], live[r] = a * 16, a >= 4
        r += 1
    for a in range(8):
        flat[r], live[r] = a * 16 + 1, a >= 4
        r += 1
    for a in range(8):
        flat[r], live[r] = a * 16 + 2, a == 4
        r += 1
    flat = np.where(live, flat, 1000 + np.arange(_CAND_ROWS))
    return flat.astype(np.int32), live


def _top16(s, row_i, vals_ref):
    work = s
    rank = jnp.full(s.shape, 127.0, F32)
    for a in range(PEER_TOPK):
        m = jnp.max(work, axis=0, keepdims=True)
        idx = jnp.min(jnp.where(work == m, row_i, N_KEYS), axis=0, keepdims=True)
        hit = row_i == idx
        work = jnp.where(hit, -jnp.inf, work)
        rank = jnp.where(hit, float(a), rank)
        vals_ref[a:a + 1, :] = m
    return rank


def _route_kernel(x_ref, gain_ref, shift_ref, scale_ref, wq_ref, sk_ref, flat_ref, live_ref,
                  ht_ref, a1_ref, cnt_ref, b2_ref, r2_ref, qt_ref, v1_ref, v2_ref):
    tt = x_ref.shape[0]
    h2 = _modulate(x_ref[...], gain_ref[...], shift_ref[0], scale_ref[0])
    ht = h2.T.astype(BF16)
    ht_ref[...] = ht
    qt_ref[...] = _dot(wq_ref[...], ht)
    row_i = lax.broadcasted_iota(I32, (N_KEYS, tt), 0)
    flat = flat_ref[...]
    live = live_ref[...] > 0.0

    for h in range(PEER_HEADS):
        s1 = _dot(sk_ref[2 * h], qt_ref[(2 * h) * N_KEYS:(2 * h + 1) * N_KEYS, :].astype(BF16))
        r1 = _top16(s1, row_i, v1_ref)
        a1_ref[h] = jnp.exp(s1 - v1_ref[0:1, :])
        s2 = _dot(sk_ref[2 * h + 1], qt_ref[(2 * h + 1) * N_KEYS:(2 * h + 2) * N_KEYS, :].astype(BF16))
        r2 = _top16(s2, row_i, v2_ref)
        r2_ref[h] = r2
        b2u = jnp.exp(s2 - v2_ref[0:1, :])

        v1 = v1_ref[...]
        v2 = v2_ref[...]
        cand = jnp.concatenate(
            [v1[0:1, :] + v2] + [v1[a:a + 1, :] + v2[0:8, :] for a in (1, 2, 3)]
            + [v1 + v2[0:1, :], v1[0:8, :] + v2[1:2, :], v1[0:8, :] + v2[2:3, :]], axis=0)
        cand = jnp.where(live, cand, -jnp.inf)
        top = v1[0:1, :] + v2[0:1, :]
        ex = jnp.exp(cand - top)
        sel = jnp.zeros(cand.shape, F32)
        work = cand
        for _ in range(PEER_TOPK):
            m = jnp.max(work, axis=0, keepdims=True)
            fi = jnp.min(jnp.where(work == m, flat, 9999), axis=0, keepdims=True)
            hit = flat == fi
            work = jnp.where(hit, -jnp.inf, work)
            sel = jnp.where(hit, 1.0, sel)
        z = jnp.sum(sel * ex, axis=0, keepdims=True)
        b2_ref[h] = b2u * (0.5 / z)

        zeros8 = jnp.zeros((8, tt), F32)
        cnt_hi = (sel[40:56, :] + jnp.concatenate([sel[56:64, :], zeros8], axis=0)
                  + jnp.concatenate([sel[64:72, :], zeros8], axis=0))
        cnt_lo = [jnp.sum(sel[0:16, :], axis=0, keepdims=True)] + [
            jnp.sum(sel[16 + 8 * (a - 1):24 + 8 * (a - 1), :], axis=0, keepdims=True) for a in (1, 2, 3)]
        cnt = jnp.zeros(s1.shape, F32)
        for a in range(PEER_TOPK):
            ca = cnt_lo[a] if a < 4 else cnt_hi[a:a + 1, :]
            cnt = jnp.where(r1 == float(a), ca, cnt)
        cnt_ref[h] = cnt


def _route(x2, gain, shift, scale, wq_t, sub_keys, seq):
    t, d = x2.shape
    tt = 256
    per_b = seq // tt
    flat_np, live_np = _cand_layout()
    flat = jnp.asarray(np.broadcast_to(flat_np[:, None], (_CAND_ROWS, tt)).copy())
    live = jnp.asarray(np.broadcast_to(live_np[:, None], (_CAND_ROWS, tt)).astype(np.float32))
    dense = jax.ShapeDtypeStruct((PEER_HEADS, N_KEYS, t), F32)
    dense_spec = pl.BlockSpec((PEER_HEADS, N_KEYS, tt), lambda i: (0, 0, i))
    const2 = lambda i: (0, 0)
    return pl.pallas_call(
        _route_kernel,
        grid=(t // tt,),
        in_specs=[
            pl.BlockSpec((tt, d), lambda i: (i, 0)),
            pl.BlockSpec((1, d), const2),
            pl.BlockSpec((1, 1, d), lambda i: (i // per_b, 0, 0)),
            pl.BlockSpec((1, 1, d), lambda i: (i // per_b, 0, 0)),
            pl.BlockSpec(wq_t.shape, const2),
            pl.BlockSpec(sub_keys.shape, lambda i: (0, 0, 0)),
            pl.BlockSpec((_CAND_ROWS, tt), const2),
            pl.BlockSpec((_CAND_ROWS, tt), const2),
        ],
        out_specs=[pl.BlockSpec((d, tt), lambda i: (0, i)), dense_spec, dense_spec, dense_spec, dense_spec],
        out_shape=[jax.ShapeDtypeStruct((d, t), BF16), dense, dense, dense, dense],
        scratch_shapes=[
            pltpu.VMEM((PEER_HEADS * 2 * N_KEYS, tt), F32),
            pltpu.VMEM((PEER_TOPK, tt), F32),
            pltpu.VMEM((PEER_TOPK, tt), F32),
        ],
        compiler_params=_params("parallel"),
        name="peer_route",
    )(x2, gain, shift, scale, wq_t, sub_keys, flat, live)


_JB = 4
_IB = 4
_MXU_ROWS = 256


def _peer_kernel(n_tiles, ht_ref, u_ref, vt_ref, a1_ref, cnt_ref, b2_ref, r2_ref, x_ref, gf_ref, o_ref,
                 acc_ref, at_ref, gt_ref, bc_ref):
    s = pl.program_id(1)
    _, te, tt = at_ref.shape
    n_i = te // N_KEYS
    assert n_i == SUBLANES

    def mxu_pieces(slot):
        pieces = []
        for r in range(0, te, _MXU_ROWS):
            def piece_a(r=r):
                at_ref[slot, r:r + _MXU_ROWS, :] = _dot(u_ref[r:r + _MXU_ROWS, :], ht_ref[...])
            pieces.append(piece_a)
        for r in range(0, acc_ref.shape[0], _MXU_ROWS):
            def piece_c(r=r):
                acc_ref[r:r + _MXU_ROWS, :] += _dot(vt_ref[r:r + _MXU_ROWS, :], gt_ref[slot])
            pieces.append(piece_c)
        return pieces

    def bcast_rows():
        i0 = pl.multiple_of((s - 1) * n_i, SUBLANES)
        for h in range(PEER_HEADS):
            a1g = a1_ref[h, pl.ds(i0, n_i), :]
            cng = cnt_ref[h, pl.ds(i0, n_i), :]
            for ii in range(n_i):
                bc_ref[2 * (h * n_i + ii)] = jnp.broadcast_to(a1g[ii:ii + 1, :], (SUBLANES, tt))
                bc_ref[2 * (h * n_i + ii) + 1] = jnp.broadcast_to(cng[ii:ii + 1, :], (SUBLANES, tt))

    def vpu_blocks(slot):
        def block(tb, jb, ib):
            lanes = slice(tb * LANES, (tb + 1) * LANES)
            j0 = jb * SUBLANES * _JB
            w = [[None] * _JB for _ in range(_IB)]
            for h in range(PEER_HEADS):
                r2 = [r2_ref[h, j0 + SUBLANES * k:j0 + SUBLANES * (k + 1), lanes] for k in range(_JB)]
                b2 = [b2_ref[h, j0 + SUBLANES * k:j0 + SUBLANES * (k + 1), lanes] for k in range(_JB)]
                for di in range(_IB):
                    ii = ib * _IB + di
                    a1v = bc_ref[2 * (h * n_i + ii), :, lanes]
                    cv = bc_ref[2 * (h * n_i + ii) + 1, :, lanes]
                    for k in range(_JB):
                        term = jnp.where(r2[k] < cv, a1v * b2[k], 0.0)
                        w[di][k] = term if w[di][k] is None else w[di][k] + term
            for di in range(_IB):
                r0 = (ib * _IB + di) * N_KEYS + j0
                rows = slice(r0, r0 + SUBLANES * _JB)
                a = at_ref[slot, rows, lanes]
                act2 = a + a * lax.erf(a * np.float32(np.sqrt(0.5)))
                gt_ref[slot, rows, lanes] = (jnp.concatenate(w[di], axis=0) * act2).astype(BF16)
        return [functools.partial(block, tb, jb, ib) for tb in range(tt // LANES)
                for jb in range(N_KEYS // (SUBLANES * _JB)) for ib in range(n_i // _IB)]

    @pl.when(s == 0)
    def _():
        acc_ref[...] = jnp.zeros_like(acc_ref)
        gt_ref[1] = jnp.zeros((te, tt), BF16)
        at_ref[0] = _dot(u_ref[...], ht_ref[...])

    for parity in (0, 1):
        @pl.when(jnp.logical_and(jnp.logical_and(s >= 1, s <= n_tiles), s % 2 == parity))
        def _():
            bcast_rows()
            mxu = mxu_pieces(parity)
            vpu = vpu_blocks(1 - parity)
            per = len(vpu) // len(mxu)
            for k, blk in enumerate(vpu):
                if k % per == 0:
                    mxu[k // per]()
                blk()

    @pl.when(s == n_tiles + 1)
    def _():
        acc = acc_ref[...] + _dot(vt_ref[...], gt_ref[(n_tiles + 1) % 2])
        o_ref[...] = x_ref[...] + gf_ref[0] * acc.T


def _peer(ht, u, vt, a1, cnt, b2, r2, x2, gate_f, seq):
    t, d = x2.shape
    n_tiles = u.shape[0] // PEER_TE
    tt = 512
    per_b = seq // tt
    f32_spec = pl.BlockSpec((PEER_HEADS, N_KEYS, tt), lambda i, s: (0, 0, i))
    return pl.pallas_call(
        functools.partial(_peer_kernel, n_tiles),
        grid=(t // tt, n_tiles + 2),
        in_specs=[
            pl.BlockSpec((d, tt), lambda i, s: (0, i)),
            pl.BlockSpec((PEER_TE, d), lambda i, s: (jnp.minimum(s, n_tiles - 1), 0)),
            pl.BlockSpec((d, PEER_TE), lambda i, s: (0, jnp.clip(s - 2, 0, n_tiles - 1))),
            f32_spec, f32_spec, f32_spec, f32_spec,
            pl.BlockSpec((tt, d), lambda i, s: (i, 0)),
            pl.BlockSpec((1, 1, d), lambda i, s: (i // per_b, 0, 0)),
        ],
        out_specs=pl.BlockSpec((tt, d), lambda i, s: (i, 0)),
        out_shape=jax.ShapeDtypeStruct((t, d), F32),
        scratch_shapes=[
            pltpu.VMEM((d, tt), F32),
            pltpu.VMEM((2, PEER_TE, tt), F32),
            pltpu.VMEM((2, PEER_TE, tt), BF16),
            pltpu.VMEM((2 * PEER_HEADS * PEER_TE // N_KEYS, SUBLANES, tt), F32),
        ],
        compiler_params=_params("parallel", "arbitrary"),
        name="peer_experts",
    )(ht, u, vt, a1, cnt, b2, r2, x2, gate_f)


def _tcast_kernel(x_ref, o_ref):
    o_ref[...] = x_ref[...].T.astype(o_ref.dtype)


def _transpose_cast(x, dtype):
    r, c = x.shape
    tr, tc = min(r, 1024), min(c, 1024)
    return pl.pallas_call(
        _tcast_kernel,
        grid=(r // tr, c // tc),
        in_specs=[pl.BlockSpec((tr, tc), lambda i, j: (i, j))],
        out_specs=pl.BlockSpec((tc, tr), lambda i, j: (j, i)),
        out_shape=jax.ShapeDtypeStruct((c, r), dtype),
        compiler_params=_params("parallel", "parallel"),
        name="transpose_cast",
    )(x)


def _pad_heads(w, width):
    d = w.shape[0]
    w = w.reshape(d, HEADS, width)
    return jnp.pad(w, ((0, 0), (0, 0), (0, HEAD_DIM - width))).reshape(d, HEADS * HEAD_DIM)


def _pack_w_in(w):
    d = w.shape[0]
    o = 0
    a = w[:, o:o + W_A]; o += W_A
    b = w[:, o:o + W_B]; o += W_B
    n_idx = IDX_HEADS * IDX_DIM + IDX_DIM + IDX_HEADS
    i = jnp.pad(w[:, o:o + n_idx], ((0, 0), (0, W_I - n_idx))); o += n_idx
    cq = _pad_heads(w[:, o:o + HEADS * GLA_DK], GLA_DK); o += HEADS * GLA_DK
    ck = _pad_heads(w[:, o:o + HEADS * GLA_DK], GLA_DK); o += HEADS * GLA_DK
    cv = w[:, o:o + BRANCH_WIDTH]; o += BRANCH_WIDTH
    code = jnp.pad(w[:, o:o + GLA_GATE_RANK], ((0, 0), (0, LANES - GLA_GATE_RANK))); o += GLA_GATE_RANK
    cg = w[:, o:o + BRANCH_WIDTH]; o += BRANCH_WIDTH
    m = w[:, o:o + W_M]
    return jnp.concatenate([a, b, i, cq, ck, cv, cg, code, m], axis=1).astype(BF16)


def kernel(x, c, w_ada, b_ada, norm_mix, norm_ffn, w_in, hgrn_lb_logits, hgrn_out_norm, dsa_q_norm, dsa_k_norm,
           gla_gate_up, gla_gate_bias, gla_out_norm, w_branch, w_out, peer_w_query, peer_sub_keys, peer_u, peer_v):
    bsz, seq, d = x.shape
    depth = w_in.shape[0]
    t = bsz * seq
    x2 = x.reshape(t, d)
    mod = _ada(c, w_ada, b_ada)

    for l in range(depth):
        shift_m, scale_m, gate_m, shift_f, scale_f, gate_f = [
            mod[l, :, k * d:(k + 1) * d].reshape(bsz, 1, d) for k in range(6)]
        za, zb, zi, zc, zm = _inproj(x2, norm_mix[l].reshape(1, d), shift_m, scale_m, _pack_w_in(w_in[l]), seq)

        ya = _scan("hgrn", l, za, (hgrn_lb_logits,), hgrn_out_norm[l].reshape(1, HEAD_DIM), bsz, seq)
        yb = _dsa(zb, zi, dsa_q_norm[l].reshape(1, HEAD_DIM), dsa_k_norm[l].reshape(1, HEAD_DIM), bsz, seq)
        gup = jnp.pad(_pad_heads(gla_gate_up[l], GLA_DK), ((0, LANES - GLA_GATE_RANK), (0, 0)))
        gb = _pad_heads(gla_gate_bias[l].reshape(1, HEADS * GLA_DK), GLA_DK)
        yc = _scan("gla", l, zc, (gup, gb), gla_out_norm[l].reshape(1, HEAD_DIM), bsz, seq)

        x2 = _merge(ya, yb, yc, zm, x2, gate_m, w_branch[l].astype(BF16), w_out[l].astype(BF16), seq)

        ht, a1, cnt, b2, r2 = _route(
            x2, norm_ffn[l].reshape(1, d), shift_f, scale_f, _transpose_cast(peer_w_query[l], BF16),
            peer_sub_keys[l].reshape(PEER_HEADS * 2, N_KEYS, -1).astype(BF16), seq)
        x2 = _peer(ht, peer_u[l].astype(BF16), _transpose_cast(peer_v[l], BF16), a1, cnt, b2, r2, x2, gate_f, seq)

    return x2.reshape(bsz, seq, d)
```

```python
import functools

import numpy as np
import jax
import jax.numpy as jnp
from jax import lax
from jax.experimental import pallas as pl
from jax.experimental.pallas import tpu as pltpu

F32, BF16, I32 = jnp.float32, jnp.bfloat16, jnp.int32

D_MODEL = 1024
HEADS = 4
HEAD_DIM = 128
BRANCH_WIDTH = HEADS * HEAD_DIM
IDX_HEADS = 4
IDX_DIM = 64
DSA_MAX_TOPK = 256
QBLOCK = 128
GLA_DK = 64
GLA_GATE_RANK = 16
GLA_TAU = 16.0
PEER_HEADS = 8
N_KEYS = 128
PEER_TOPK = 16
PEER_TE = 1024
EPS = 1e-6

LANES = 128
SUBLANES = 8
VMEM_LIMIT_BYTES = 56 * 1024 * 1024

NEG_BIG = -1e30
INT_MIN = np.int32(-2 ** 31)

W_A = 4 * BRANCH_WIDTH
W_B = 3 * BRANCH_WIDTH
W_I = 3 * LANES
W_C = 4 * BRANCH_WIDTH + LANES
W_M = 3 * D_MODEL
W_PACK = W_A + W_B + W_I + W_C + W_M


def _dot(a, b):
    return jnp.dot(a, b, preferred_element_type=F32)


def _dot_nt(a, b):
    return lax.dot_general(a, b, (((1,), (1,)), ((), ())), preferred_element_type=F32)


def _dot_tn(a, b):
    return lax.dot_general(a, b, (((0,), (0,)), ((), ())), preferred_element_type=F32)


def _split2(x):
    hi = x.astype(BF16)
    lo = (x - hi.astype(F32)).astype(BF16)
    return hi, lo


def _split3(x):
    hi = x.astype(BF16)
    r = x - hi.astype(F32)
    mid = r.astype(BF16)
    lo = (r - mid.astype(F32)).astype(BF16)
    return hi, mid, lo


def _params(*sem):
    return pltpu.CompilerParams(dimension_semantics=sem, vmem_limit_bytes=VMEM_LIMIT_BYTES)


def _modulate(x, gain, shift, scale):
    ms = jnp.mean(x * x, axis=-1, keepdims=True)
    return x * lax.rsqrt(ms + EPS) * gain * (1.0 + scale) + shift


def _ada_kernel(c_ref, w_ref, b_ref, o_ref):
    c = c_ref[...]
    sc = c * jax.nn.sigmoid(c)
    a_hi, a_lo = _split2(sc)
    w_hi, w_lo = _split2(w_ref[0])
    o_ref[0] = _dot(a_hi, w_hi) + _dot(a_hi, w_lo) + _dot(a_lo, w_hi) + b_ref[0]


def _ada(c, w_ada, b_ada):
    depth, d, n = w_ada.shape
    bsz = c.shape[0]
    tn = 1536
    return pl.pallas_call(
        _ada_kernel,
        grid=(depth, n // tn),
        in_specs=[
            pl.BlockSpec((bsz, d), lambda l, j: (0, 0)),
            pl.BlockSpec((1, d, tn), lambda l, j: (l, 0, j)),
            pl.BlockSpec((1, 1, tn), lambda l, j: (l, 0, j)),
        ],
        out_specs=pl.BlockSpec((1, bsz, tn), lambda l, j: (l, 0, j)),
        out_shape=jax.ShapeDtypeStruct((depth, bsz, n), F32),
        compiler_params=_params("parallel", "parallel"),
        name="ada",
    )(c, w_ada, b_ada.reshape(depth, 1, n))


def _inproj_kernel(x_ref, gain_ref, shift_ref, scale_ref, w_ref, *out_refs):
    h = _modulate(x_ref[...], gain_ref[...], shift_ref[0], scale_ref[0]).astype(BF16)
    off = 0
    for o_ref in out_refs:
        width = o_ref.shape[1]
        for c0 in range(0, width, 512):
            c1 = min(c0 + 512, width)
            o_ref[:, c0:c1] = _dot(h, w_ref[:, off + c0:off + c1]).astype(o_ref.dtype)
        off += width


def _inproj(x2, gain, shift, scale, w_pack, layer, seq):
    t, d = x2.shape
    tm = 256
    per_b = seq // tm
    widths = (W_A, W_B, W_I, W_C, W_M)
    dtypes = (F32, BF16, F32, F32, BF16)
    return pl.pallas_call(
        _inproj_kernel,
        grid=(t // tm,),
        in_specs=[
            pl.BlockSpec((tm, d), lambda i: (i, 0)),
            pl.BlockSpec((1, d), lambda i: (0, 0)),
            pl.BlockSpec((1, 1, d), lambda i: (i // per_b, 0, 0)),
            pl.BlockSpec((1, 1, d), lambda i: (i // per_b, 0, 0)),
            pl.BlockSpec((None, d, W_PACK), lambda i: (layer, 0, 0), pipeline_mode=pl.Buffered(1)),
        ],
        out_specs=[pl.BlockSpec((tm, w), lambda i: (i, 0)) for w in widths],
        out_shape=[jax.ShapeDtypeStruct((t, w), dt) for w, dt in zip(widths, dtypes)],
        compiler_params=_params("parallel"),
        name="inproj",
    )(x2, gain, shift, scale, w_pack)


def _group_row_bcast(x, group, row):
    n, w = x.shape
    if group >= SUBLANES:
        x3 = x.reshape(n // group, group, w)
        return jnp.broadcast_to(x3[:, row:row + 1, :], x3.shape).reshape(n, w)
    x3 = x.reshape(n // SUBLANES, SUBLANES, w)
    sub = lax.broadcasted_iota(I32, x3.shape, 1)
    n_groups = SUBLANES // group
    res = None
    for g in reversed(range(n_groups)):
        r = g * group + row
        bc = jnp.broadcast_to(x3[:, r:r + 1, :], x3.shape)
        res = bc if res is None else jnp.where(sub < (g + 1) * group, bc, res)
    return res.reshape(n, w)


def _scan_kernel(mode, layer, *refs):
    if mode == "hgrn":
        q_ref, f_ref, v_ref, g_ref, lbl_ref, gain_ref, lv_ref, tri_ref, y_ref, st_ref = refs
    else:
        q_ref, k_ref, v_ref, g_ref, code_ref, gup_ref, gb_ref, gain_ref, lv_ref, tri_ref, y_ref, st_ref = refs
    lt = q_ref.shape[0]
    n_levels = lt.bit_length() - 1

    @pl.when(pl.program_id(2) == 0)
    def _():
        st_ref[...] = jnp.zeros_like(st_ref)

    if mode == "hgrn":
        lbl = lbl_ref[...]
        e = jnp.exp(lbl - jnp.max(lbl, axis=0, keepdims=True))
        p = e / jnp.sum(e, axis=0, keepdims=True)
        lb = jnp.zeros((1, p.shape[1]), F32)
        for l2 in range(1, layer + 1):
            lb = lb + p[l2:l2 + 1, :]
        f = lb + (1.0 - lb) * jax.nn.sigmoid(f_ref[...])
        lg = jnp.log(f)
        kk = 1.0 - f
        q = q_ref[...] * (HEAD_DIM ** -0.5)
    else:
        c_hi, c_lo = _split2(code_ref[...])
        u_hi, u_lo = _split2(gup_ref[...])
        z = _dot(c_hi, u_hi) + _dot(c_hi, u_lo) + _dot(c_lo, u_hi) + gb_ref[...]
        lg = (jnp.minimum(z, 0.0) - jnp.log1p(jnp.exp(-jnp.abs(z)))) * (1.0 / GLA_TAU)
        kk = k_ref[...]
        q = q_ref[...] * (GLA_DK ** -0.5)

    tri = tri_ref[...]
    g_hi, g_mid, g_lo = _split3(lg)
    b = _dot(tri, g_hi) + _dot(tri, g_mid) + _dot(tri, g_lo)

    lv = lv_ref[...]
    s = jnp.where(lv == -1, _dot_nt(q.astype(BF16), kk.astype(BF16)), 0.0)
    for l in range(n_levels):
        n = 1 << l
        bref = _group_row_bcast(b, 2 * n, n - 1)
        ql = (q * jnp.exp(jnp.minimum(b - bref, 0.0))).astype(BF16)
        kl = (kk * jnp.exp(jnp.minimum(bref - b, 0.0))).astype(BF16)
        s = jnp.where(lv == l, _dot_nt(ql, kl), s)

    vb = v_ref[...].astype(BF16)
    st = st_ref[...]
    o = _dot(s.astype(BF16), vb) + _dot_nt((q * jnp.exp(b)).astype(BF16), st.astype(BF16))
    b_last = b[lt - 1:lt, :]
    kd = (kk * jnp.exp(b_last - b)).astype(BF16)
    st_ref[...] = st * jnp.exp(b_last) + _dot_tn(vb, kd)

    ms = jnp.mean(o * o, axis=-1, keepdims=True)
    g = g_ref[...]
    y = o * lax.rsqrt(ms + EPS) * gain_ref[...] * (g * jax.nn.sigmoid(g))
    y_ref[...] = y.astype(y_ref.dtype)


def _scan_consts(lt):
    idx = np.arange(lt)
    x = idx[:, None] ^ idx[None, :]
    lvl = np.floor(np.log2(np.maximum(x, 1))).astype(np.int32)
    lv = np.where(idx[None, :] < idx[:, None], lvl, np.where(x == 0, -1, -2)).astype(np.int32)
    tri = (idx[None, :] <= idx[:, None]).astype(np.float32)
    return jnp.asarray(lv), jnp.asarray(tri, dtype=BF16)


def _scan(mode, layer, z, extra, out_gain, bsz, seq):
    t = z.shape[0]
    lt = 256
    per_b = seq // lt
    lv, tri = _scan_consts(lt)

    def col(cb):
        return pl.BlockSpec((lt, HEAD_DIM), lambda b, h, c: (b * per_b + c, cb * HEADS + h))

    const2 = lambda b, h, c: (0, 0)
    head_col = lambda b, h, c: (0, h)
    in_specs = [col(0), col(1), col(2), col(3)]
    args = [z, z, z, z]
    if mode == "hgrn":
        (lb_logits,) = extra
        in_specs.append(pl.BlockSpec((lb_logits.shape[0], HEAD_DIM), head_col))
        args.append(lb_logits)
    else:
        gup, gb = extra
        in_specs += [pl.BlockSpec((lt, LANES), lambda b, h, c: (b * per_b + c, 4 * HEADS)),
                     pl.BlockSpec((LANES, HEAD_DIM), head_col),
                     pl.BlockSpec((1, HEAD_DIM), head_col)]
        args += [z, gup, gb]
    in_specs += [pl.BlockSpec((1, HEAD_DIM), const2), pl.BlockSpec((lt, lt), const2), pl.BlockSpec((lt, lt), const2)]
    args += [out_gain, lv, tri]
    return pl.pallas_call(
        functools.partial(_scan_kernel, mode, layer),
        grid=(bsz, HEADS, per_b),
        in_specs=in_specs,
        out_specs=pl.BlockSpec((lt, HEAD_DIM), lambda b, h, c: (b * per_b + c, h)),
        out_shape=jax.ShapeDtypeStruct((t, BRANCH_WIDTH), BF16),
        scratch_shapes=[pltpu.VMEM((HEAD_DIM, HEAD_DIM), F32)],
        compiler_params=_params("parallel", "parallel", "arbitrary"),
        name="scan_" + mode,
    )(*args)


_KCH = 256


def _dsa_kernel(topk, q_ref, k_ref, v_ref, ziq_ref, zik_ref, qg_ref, kg_ref, stri_ref, o_ref,
                kn_ref, vt_ref, kih_ref, kil_ref, keys_ref, bias_ref, acc_ref):
    j = pl.program_id(1)
    seq = k_ref.shape[0]

    @pl.when(j == 0)
    def _prep():
        def body(c, carry):
            r0 = pl.multiple_of(c * _KCH, _KCH)
            kc = k_ref[pl.ds(r0, _KCH), :].astype(F32)
            for h in range(HEADS):
                kh = kc[:, h * HEAD_DIM:(h + 1) * HEAD_DIM]
                ms = jnp.mean(kh * kh, axis=-1, keepdims=True)
                kn_ref[pl.ds(r0, _KCH), h * HEAD_DIM:(h + 1) * HEAD_DIM] = (
                    kh * lax.rsqrt(ms + EPS) * kg_ref[...]).astype(BF16)
            vt_ref[c] = v_ref[pl.ds(r0, _KCH), :].astype(F32).T.astype(BF16)
            ki = zik_ref[pl.ds(r0, _KCH), :]
            hi = ki.astype(BF16)
            kih_ref[pl.ds(r0, _KCH), :] = hi
            kil_ref[pl.ds(r0, _KCH), :] = (ki - hi.astype(F32)).astype(BF16)
            return carry
        lax.fori_loop(0, seq // _KCH, body, 0)

    n_ch = (j * QBLOCK + QBLOCK + _KCH - 1) // _KCH
    ziq = ziq_ref[...]
    w_t = ziq[:, 2 * LANES:3 * LANES].T
    qi = jnp.concatenate([ziq[:, h * IDX_DIM:(h + 1) * IDX_DIM] for h in range(IDX_HEADS)], axis=0)
    qi_hi, qi_lo = _split2(qi * (IDX_DIM ** -0.5))
    w_rows = [w_t[IDX_DIM + h:IDX_DIM + h + 1, :] * (IDX_HEADS ** -0.5) for h in range(IDX_HEADS)]
    q_pos = j * QBLOCK + lax.broadcasted_iota(I32, (_KCH, QBLOCK), 1)
    row_i = lax.broadcasted_iota(I32, (_KCH, QBLOCK), 0)

    def idx_body(c, carry):
        r0 = pl.multiple_of(c * _KCH, _KCH)
        k_hi = kih_ref[pl.ds(r0, _KCH), :][:, :IDX_DIM]
        k_lo = kil_ref[pl.ds(r0, _KCH), :][:, :IDX_DIM]
        logit = _dot_nt(k_hi, qi_hi) + _dot_nt(k_hi, qi_lo) + _dot_nt(k_lo, qi_hi)
        score = jnp.zeros((_KCH, QBLOCK), F32)
        for h in range(IDX_HEADS):
            score = score + w_rows[h] * jnp.maximum(logit[:, h * QBLOCK:(h + 1) * QBLOCK], 0.0)
        score = jnp.where(score == 0.0, 0.0, score)
        bits = lax.bitcast_convert_type(score, I32)
        key = jnp.where(bits < 0, bits ^ np.int32(0x7FFFFFFF), bits)
        keys_ref[pl.ds(r0, _KCH), :] = jnp.where(r0 + row_i <= q_pos, key, INT_MIN)
        return carry
    lax.fori_loop(0, n_ch, idx_body, 0)

    def count(pred):
        def body(c, acc):
            r0 = pl.multiple_of(c * _KCH, _KCH)
            m = jnp.where(pred(keys_ref[pl.ds(r0, _KCH), :]), 1, 0).astype(I32)
            return acc + jnp.sum(m.reshape(_KCH // SUBLANES, SUBLANES, QBLOCK), axis=0)
        acc = lax.fori_loop(0, n_ch, body, jnp.zeros((SUBLANES, QBLOCK), I32))
        return jnp.sum(acc, axis=0, keepdims=True)

    def bisect(it, thr):
        cand = thr ^ jnp.left_shift(jnp.int32(1), 31 - it)
        return jnp.where(count(lambda x: x >= cand) >= topk, cand, thr)
    thr = lax.fori_loop(0, 32, bisect, jnp.full((1, QBLOCK), INT_MIN, I32))
    need = (topk - count(lambda x: x > thr)).astype(F32)

    def sel_body(c, seen):
        r0 = pl.multiple_of(c * _KCH, _KCH)
        x = keys_ref[pl.ds(r0, _KCH), :]
        eq = jnp.where(x == thr, 1.0, 0.0)
        rank = seen + _dot(stri_ref[...], eq.astype(BF16))
        tie_ok = jnp.where(x == thr, jnp.where(rank < need, 0.0, NEG_BIG), NEG_BIG)
        bias = jnp.where(x > thr, 0.0, tie_ok)
        bias_ref[pl.ds(r0, _KCH), :] = jnp.where(x == INT_MIN, NEG_BIG, bias)
        return seen + jnp.sum(eq, axis=0, keepdims=True)
    lax.fori_loop(0, n_ch, sel_body, jnp.zeros((1, QBLOCK), F32))

    qf = q_ref[...].astype(F32)
    qn = []
    for h in range(HEADS):
        qh = qf[:, h * HEAD_DIM:(h + 1) * HEAD_DIM]
        ms = jnp.mean(qh * qh, axis=-1, keepdims=True)
        qn.append((qh * lax.rsqrt(ms + EPS) * qg_ref[...] * (HEAD_DIM ** -0.5)).astype(BF16))
    acc_ref[...] = jnp.zeros_like(acc_ref)

    def att_body(c, carry):
        r0 = pl.multiple_of(c * _KCH, _KCH)
        bias = bias_ref[pl.ds(r0, _KCH), :]
        out = []
        for h in range(HEADS):
            m, l = carry[h]
            kc = kn_ref[pl.ds(r0, _KCH), h * HEAD_DIM:(h + 1) * HEAD_DIM]
            sc = _dot_nt(kc, qn[h]) + bias
            m_new = jnp.maximum(m, jnp.max(sc, axis=0, keepdims=True))
            p = jnp.exp(sc - m_new)
            alpha = jnp.exp(m - m_new)
            vt = vt_ref[c, h * HEAD_DIM:(h + 1) * HEAD_DIM, :]
            acc_ref[h] = alpha * acc_ref[h] + _dot(vt, p.astype(BF16))
            out.append((m_new, alpha * l + jnp.sum(p, axis=0, keepdims=True)))
        return tuple(out)
    init = tuple((jnp.full((1, QBLOCK), NEG_BIG, F32), jnp.zeros((1, QBLOCK), F32)) for _ in range(HEADS))
    stats = lax.fori_loop(0, n_ch, att_body, init)
    for h in range(HEADS):
        o_ref[:, h * HEAD_DIM:(h + 1) * HEAD_DIM] = (acc_ref[h] / stats[h][1]).T.astype(o_ref.dtype)


def _dsa(zb, zi, q_gain, k_gain, bsz, seq):
    t = zb.shape[0]
    topk = min(DSA_MAX_TOPK, seq // 4)
    nq = seq // QBLOCK
    idx = np.arange(_KCH)
    stri = jnp.asarray((idx[None, :] < idx[:, None]).astype(np.float32), dtype=BF16)
    const2 = lambda b, j: (0, 0)
    return pl.pallas_call(
        functools.partial(_dsa_kernel, topk),
        grid=(bsz, nq),
        in_specs=[
            pl.BlockSpec((QBLOCK, BRANCH_WIDTH), lambda b, j: (b * nq + j, 0)),
            pl.BlockSpec((seq, BRANCH_WIDTH), lambda b, j: (b, 1)),
            pl.BlockSpec((seq, BRANCH_WIDTH), lambda b, j: (b, 2)),
            pl.BlockSpec((QBLOCK, W_I), lambda b, j: (b * nq + j, 0)),
            pl.BlockSpec((seq, LANES), lambda b, j: (b, 2)),
            pl.BlockSpec((1, HEAD_DIM), const2),
            pl.BlockSpec((1, HEAD_DIM), const2),
            pl.BlockSpec((_KCH, _KCH), const2),
        ],
        out_specs=pl.BlockSpec((QBLOCK, BRANCH_WIDTH), lambda b, j: (b * nq + j, 0)),
        out_shape=jax.ShapeDtypeStruct((t, BRANCH_WIDTH), BF16),
        scratch_shapes=[
            pltpu.VMEM((seq, BRANCH_WIDTH), BF16),
            pltpu.VMEM((seq // _KCH, BRANCH_WIDTH, _KCH), BF16),
            pltpu.VMEM((seq, LANES), BF16),
            pltpu.VMEM((seq, LANES), BF16),
            pltpu.VMEM((seq, QBLOCK), I32),
            pltpu.VMEM((seq, QBLOCK), F32),
            pltpu.VMEM((HEADS, HEAD_DIM, QBLOCK), F32),
        ],
        compiler_params=_params("parallel", "arbitrary"),
        name="dsa",
    )(zb, zb, zb, zi, zi, q_gain, k_gain, stri)


def _merge_kernel(ya_ref, yb_ref, yc_ref, zm_ref, x_ref, gm_ref, wb_ref, wo_ref, o_ref):
    d = x_ref.shape[1]
    mixed = None
    for g, y_ref in enumerate((ya_ref, yb_ref, yc_ref)):
        up = _dot(y_ref[...], wb_ref[g])
        term = jax.nn.sigmoid(zm_ref[:, g * d:(g + 1) * d].astype(F32)) * up
        mixed = term if mixed is None else mixed + term
    o_ref[...] = x_ref[...] + gm_ref[0] * _dot(mixed.astype(BF16), wo_ref[...])


def _merge(ya, yb, yc, zm, x2, gate_m, w_branch, w_out, seq):
    t, d = x2.shape
    tm = 256
    per_b = seq // tm
    row = lambda i: (i, 0)
    return pl.pallas_call(
        _merge_kernel,
        grid=(t // tm,),
        in_specs=[
            pl.BlockSpec((tm, BRANCH_WIDTH), row), pl.BlockSpec((tm, BRANCH_WIDTH), row),
            pl.BlockSpec((tm, BRANCH_WIDTH), row), pl.BlockSpec((tm, W_M), row), pl.BlockSpec((tm, d), row),
            pl.BlockSpec((1, 1, d), lambda i: (i // per_b, 0, 0)),
            pl.BlockSpec(w_branch.shape, lambda i: (0, 0, 0)),
            pl.BlockSpec(w_out.shape, lambda i: (0, 0)),
        ],
        out_specs=pl.BlockSpec((tm, d), row),
        out_shape=jax.ShapeDtypeStruct((t, d), F32),
        compiler_params=_params("parallel"),
        name="merge",
    )(ya, yb, yc, zm, x2, gate_m, w_branch, w_out)


_CAND_ROWS = 72


def _cand_layout():
    flat = np.zeros((_CAND_ROWS,), np.int32)
    live = np.zeros((_CAND_ROWS,), bool)
    r = 0
    for b in range(16):
        flat[r], live[r] = b, True
        r += 1
    for a in (1, 2, 3):
        for b in range(8):
            flat[r], live[r] = a * 16 + b, True
            r += 1
    for a in range(16):
        flat[r], live[r] = a * 16, a >= 4
        r += 1
    for a in range(8):
        flat[r], live[r] = a * 16 + 1, a >= 4
        r += 1
    for a in range(8):
        flat[r], live[r] = a * 16 + 2, a == 4
        r += 1
    flat = np.where(live, flat, 1000 + np.arange(_CAND_ROWS))
    return flat.astype(np.int32), live


def _any_column_differs(count, expected):
    return jnp.max(jnp.abs(count - expected)) > 0.0


def _top16_pair(scores, row_i, vals_refs, rank_refs):
    work = list(scores)
    rank = [jnp.full(s.shape, 127.0, F32) for s in scores]
    for a in range(PEER_TOPK):
        for k in range(len(scores)):
            m = jnp.max(work[k], axis=0, keepdims=True)
            hit = work[k] == m
            work[k] = jnp.where(hit, -jnp.inf, work[k])
            rank[k] = jnp.where(hit, float(a), rank[k])
            vals_refs[k][a:a + 1, :] = m
    tied = None
    for k in range(len(scores)):
        rank_refs[k][...] = rank[k]
        gone = jnp.sum(jnp.where(rank[k] < 127.0, 1.0, 0.0), axis=0, keepdims=True)
        bad = _any_column_differs(gone, float(PEER_TOPK))
        tied = bad if tied is None else jnp.logical_or(tied, bad)

    @pl.when(tied)
    def _():
        for k, s in enumerate(scores):
            work = s
            rank = jnp.full(s.shape, 127.0, F32)
            for a in range(PEER_TOPK):
                m = jnp.max(work, axis=0, keepdims=True)
                idx = jnp.min(jnp.where(work == m, row_i, N_KEYS), axis=0, keepdims=True)
                hit = row_i == idx
                work = jnp.where(hit, -jnp.inf, work)
                rank = jnp.where(hit, float(a), rank)
                vals_refs[k][a:a + 1, :] = m
            rank_refs[k][...] = rank


def _route_kernel(x_ref, gain_ref, shift_ref, scale_ref, wq_ref, sk_ref, flat_ref, live_ref,
                  ht_ref, a1_ref, cnt_ref, b2_ref, r2_ref, qt_ref, v1_ref, v2_ref, r1_ref, sel_ref):
    tt = x_ref.shape[0]
    h2 = _modulate(x_ref[...], gain_ref[...], shift_ref[0], scale_ref[0])
    ht = h2.T.astype(BF16)
    ht_ref[...] = ht
    qt_ref[...] = _dot(wq_ref[...], ht)
    row_i = lax.broadcasted_iota(I32, (N_KEYS, tt), 0)
    flat = flat_ref[...]
    live = live_ref[...] > 0.0

    for h in range(PEER_HEADS):
        s1 = _dot(sk_ref[2 * h], qt_ref[(2 * h) * N_KEYS:(2 * h + 1) * N_KEYS, :].astype(BF16))
        s2 = _dot(sk_ref[2 * h + 1], qt_ref[(2 * h + 1) * N_KEYS:(2 * h + 2) * N_KEYS, :].astype(BF16))
        _top16_pair((s1, s2), row_i, (v1_ref, v2_ref), (r1_ref, r2_ref.at[h]))
        a1_ref[h] = jnp.exp(s1 - v1_ref[0:1, :])
        b2u = jnp.exp(s2 - v2_ref[0:1, :])

        v1 = v1_ref[...]
        v2 = v2_ref[...]
        cand = jnp.concatenate(
            [v1[0:1, :] + v2] + [v1[a:a + 1, :] + v2[0:8, :] for a in (1, 2, 3)]
            + [v1 + v2[0:1, :], v1[0:8, :] + v2[1:2, :], v1[0:8, :] + v2[2:3, :]], axis=0)
        cand = jnp.where(live, cand, -jnp.inf)
        top = v1[0:1, :] + v2[0:1, :]
        ex = jnp.exp(cand - top)
        sel = jnp.zeros(cand.shape, F32)
        work = cand
        for _ in range(PEER_TOPK):
            hit = work == jnp.max(work, axis=0, keepdims=True)
            work = jnp.where(hit, -jnp.inf, work)
            sel = jnp.where(hit, 1.0, sel)
        sel_ref[...] = sel

        @pl.when(_any_column_differs(jnp.sum(sel, axis=0, keepdims=True), float(PEER_TOPK)))
        def _():
            sel = jnp.zeros(cand.shape, F32)
            work = cand
            for _ in range(PEER_TOPK):
                m = jnp.max(work, axis=0, keepdims=True)
                fi = jnp.min(jnp.where(work == m, flat, 9999), axis=0, keepdims=True)
                hit = flat == fi
                work = jnp.where(hit, -jnp.inf, work)
                sel = jnp.where(hit, 1.0, sel)
            sel_ref[...] = sel
        sel = sel_ref[...]
        r1 = r1_ref[...]
        z = jnp.sum(sel * ex, axis=0, keepdims=True)
        b2_ref[h] = b2u * (0.5 / z)

        zeros8 = jnp.zeros((8, tt), F32)
        cnt_hi = (sel[40:56, :] + jnp.concatenate([sel[56:64, :], zeros8], axis=0)
                  + jnp.concatenate([sel[64:72, :], zeros8], axis=0))
        cnt_lo = [jnp.sum(sel[0:16, :], axis=0, keepdims=True)] + [
            jnp.sum(sel[16 + 8 * (a - 1):24 + 8 * (a - 1), :], axis=0, keepdims=True) for a in (1, 2, 3)]
        cnt = jnp.zeros(s1.shape, F32)
        for a in range(PEER_TOPK):
            ca = cnt_lo[a] if a < 4 else cnt_hi[a:a + 1, :]
            cnt = jnp.where(r1 == float(a), ca, cnt)
        cnt_ref[h] = cnt


def _route(x2, gain, shift, scale, wq_t, sub_keys, seq):
    t, d = x2.shape
    tt = 256
    per_b = seq // tt
    flat_np, live_np = _cand_layout()
    flat = jnp.asarray(np.broadcast_to(flat_np[:, None], (_CAND_ROWS, tt)).copy())
    live = jnp.asarray(np.broadcast_to(live_np[:, None], (_CAND_ROWS, tt)).astype(np.float32))
    dense = jax.ShapeDtypeStruct((PEER_HEADS, N_KEYS, t), F32)
    dense_spec = pl.BlockSpec((PEER_HEADS, N_KEYS, tt), lambda i: (0, 0, i))
    const2 = lambda i: (0, 0)
    return pl.pallas_call(
        _route_kernel,
        grid=(t // tt,),
        in_specs=[
            pl.BlockSpec((tt, d), lambda i: (i, 0)),
            pl.BlockSpec((1, d), const2),
            pl.BlockSpec((1, 1, d), lambda i: (i // per_b, 0, 0)),
            pl.BlockSpec((1, 1, d), lambda i: (i // per_b, 0, 0)),
            pl.BlockSpec(wq_t.shape, const2),
            pl.BlockSpec(sub_keys.shape, lambda i: (0, 0, 0)),
            pl.BlockSpec((_CAND_ROWS, tt), const2),
            pl.BlockSpec((_CAND_ROWS, tt), const2),
        ],
        out_specs=[pl.BlockSpec((d, tt), lambda i: (0, i)), dense_spec, dense_spec, dense_spec, dense_spec],
        out_shape=[jax.ShapeDtypeStruct((d, t), BF16), dense, dense, dense, dense],
        scratch_shapes=[
            pltpu.VMEM((PEER_HEADS * 2 * N_KEYS, tt), F32),
            pltpu.VMEM((PEER_TOPK, tt), F32),
            pltpu.VMEM((PEER_TOPK, tt), F32),
            pltpu.VMEM((N_KEYS, tt), F32),
            pltpu.VMEM((_CAND_ROWS, tt), F32),
        ],
        compiler_params=_params("parallel"),
        name="peer_route",
    )(x2, gain, shift, scale, wq_t, sub_keys, flat, live)


_JB = 4
_IB = 4
_MXU_ROWS = 256


def _peer_kernel(n_tiles, ht_ref, u_ref, vt_ref, a1_ref, cnt_ref, b2_ref, r2_ref, x_ref, gf_ref, o_ref,
                 acc_ref, at_ref, gt_ref, bc_ref):
    s = pl.program_id(1)
    _, te, tt = at_ref.shape
    n_i = te // N_KEYS
    assert n_i == SUBLANES

    def mxu_pieces(slot):
        pieces = []
        for r in range(0, te, _MXU_ROWS):
            def piece_a(r=r):
                at_ref[slot, r:r + _MXU_ROWS, :] = _dot(u_ref[r:r + _MXU_ROWS, :], ht_ref[...])
            pieces.append(piece_a)
        for r in range(0, acc_ref.shape[0], _MXU_ROWS):
            def piece_c(r=r):
                acc_ref[r:r + _MXU_ROWS, :] += _dot(vt_ref[r:r + _MXU_ROWS, :], gt_ref[slot])
            pieces.append(piece_c)
        return pieces

    def bcast_rows():
        i0 = pl.multiple_of((s - 1) * n_i, SUBLANES)
        for h in range(PEER_HEADS):
            a1g = a1_ref[h, pl.ds(i0, n_i), :]
            cng = cnt_ref[h, pl.ds(i0, n_i), :]
            for ii in range(n_i):
                bc_ref[2 * (h * n_i + ii)] = jnp.broadcast_to(a1g[ii:ii + 1, :], (SUBLANES, tt))
                bc_ref[2 * (h * n_i + ii) + 1] = jnp.broadcast_to(cng[ii:ii + 1, :], (SUBLANES, tt))

    def vpu_blocks(slot):
        def block(tb, jb, ib):
            lanes = slice(tb * LANES, (tb + 1) * LANES)
            j0 = jb * SUBLANES * _JB
            w = [[None] * _JB for _ in range(_IB)]
            for h in range(PEER_HEADS):
                r2 = [r2_ref[h, j0 + SUBLANES * k:j0 + SUBLANES * (k + 1), lanes] for k in range(_JB)]
                b2 = [b2_ref[h, j0 + SUBLANES * k:j0 + SUBLANES * (k + 1), lanes] for k in range(_JB)]
                for di in range(_IB):
                    ii = ib * _IB + di
                    a1v = bc_ref[2 * (h * n_i + ii), :, lanes]
                    cv = bc_ref[2 * (h * n_i + ii) + 1, :, lanes]
                    for k in range(_JB):
                        term = jnp.where(r2[k] < cv, a1v * b2[k], 0.0)
                        w[di][k] = term if w[di][k] is None else w[di][k] + term
            for di in range(_IB):
                r0 = (ib * _IB + di) * N_KEYS + j0
                rows = slice(r0, r0 + SUBLANES * _JB)
                a = at_ref[slot, rows, lanes]
                act2 = a + a * lax.erf(a * np.float32(np.sqrt(0.5)))
                gt_ref[slot, rows, lanes] = (jnp.concatenate(w[di], axis=0) * act2).astype(BF16)
        return [functools.partial(block, tb, jb, ib) for tb in range(tt // LANES)
                for jb in range(N_KEYS // (SUBLANES * _JB)) for ib in range(n_i // _IB)]

    @pl.when(s == 0)
    def _():
        acc_ref[...] = jnp.zeros_like(acc_ref)
        gt_ref[1] = jnp.zeros((te, tt), BF16)
        at_ref[0] = _dot(u_ref[...], ht_ref[...])

    for parity in (0, 1):
        @pl.when(jnp.logical_and(jnp.logical_and(s >= 1, s <= n_tiles), s % 2 == parity))
        def _():
            bcast_rows()
            mxu = mxu_pieces(parity)
            vpu = vpu_blocks(1 - parity)
            per = len(vpu) // len(mxu)
            for k, blk in enumerate(vpu):
                if k % per == 0:
                    mxu[k // per]()
                blk()

    @pl.when(s == n_tiles + 1)
    def _():
        acc = acc_ref[...] + _dot(vt_ref[...], gt_ref[(n_tiles + 1) % 2])
        o_ref[...] = x_ref[...] + gf_ref[0] * acc.T


def _peer(ht, u, vt, a1, cnt, b2, r2, x2, gate_f, seq):
    t, d = x2.shape
    n_tiles = u.shape[0] // PEER_TE
    tt = 512
    per_b = seq // tt
    f32_spec = pl.BlockSpec((PEER_HEADS, N_KEYS, tt), lambda i, s: (0, 0, i))
    return pl.pallas_call(
        functools.partial(_peer_kernel, n_tiles),
        grid=(t // tt, n_tiles + 2),
        in_specs=[
            pl.BlockSpec((d, tt), lambda i, s: (0, i)),
            pl.BlockSpec((PEER_TE, d), lambda i, s: (jnp.minimum(s, n_tiles - 1), 0)),
            pl.BlockSpec((d, PEER_TE), lambda i, s: (0, jnp.clip(s - 2, 0, n_tiles - 1))),
            f32_spec, f32_spec, f32_spec, f32_spec,
            pl.BlockSpec((tt, d), lambda i, s: (i, 0)),
            pl.BlockSpec((1, 1, d), lambda i, s: (i // per_b, 0, 0)),
        ],
        out_specs=pl.BlockSpec((tt, d), lambda i, s: (i, 0)),
        out_shape=jax.ShapeDtypeStruct((t, d), F32),
        scratch_shapes=[
            pltpu.VMEM((d, tt), F32),
            pltpu.VMEM((2, PEER_TE, tt), F32),
            pltpu.VMEM((2, PEER_TE, tt), BF16),
            pltpu.VMEM((2 * PEER_HEADS * PEER_TE // N_KEYS, SUBLANES, tt), F32),
        ],
        compiler_params=_params("parallel", "arbitrary"),
        name="peer_experts",
    )(ht, u, vt, a1, cnt, b2, r2, x2, gate_f)


def _tcast_kernel(x_ref, o_ref):
    o_ref[...] = x_ref[...].T.astype(o_ref.dtype)


def _transpose_cast(x, dtype):
    r, c = x.shape
    tr, tc = min(r, 1024), min(c, 1024)
    return pl.pallas_call(
        _tcast_kernel,
        grid=(r // tr, c // tc),
        in_specs=[pl.BlockSpec((tr, tc), lambda i, j: (i, j))],
        out_specs=pl.BlockSpec((tc, tr), lambda i, j: (j, i)),
        out_shape=jax.ShapeDtypeStruct((c, r), dtype),
        compiler_params=_params("parallel", "parallel"),
        name="transpose_cast",
    )(x)


def _pad_heads(w, width):
    d = w.shape[0]
    w = w.reshape(d, HEADS, width)
    return jnp.pad(w, ((0, 0), (0, 0), (0, HEAD_DIM - width))).reshape(d, HEADS * HEAD_DIM)


def _pack_segments():
    segs = [(0, W_A + W_B, 0)]
    n_idx = IDX_HEADS * IDX_DIM + IDX_DIM + IDX_HEADS
    src = W_A + W_B
    dst = W_A + W_B
    segs.append((src, n_idx, dst)); src += n_idx; dst += W_I
    for _ in range(2):
        for h in range(HEADS):
            segs.append((src + GLA_DK * h, GLA_DK, dst + HEAD_DIM * h))
        src += HEADS * GLA_DK; dst += BRANCH_WIDTH
    segs.append((src, BRANCH_WIDTH, dst)); src += BRANCH_WIDTH; dst += BRANCH_WIDTH
    code_src = src; src += GLA_GATE_RANK
    segs.append((src, BRANCH_WIDTH, dst)); src += BRANCH_WIDTH; dst += BRANCH_WIDTH
    segs.append((code_src, GLA_GATE_RANK, dst)); dst += LANES
    segs.append((src, W_M, dst))
    assert dst + W_M == W_PACK
    return segs


def _pack_kernel(w_ref, o_ref):
    o_ref[...] = jnp.zeros_like(o_ref)
    for src, width, dst in _pack_segments():
        o_ref[0, :, dst:dst + width] = w_ref[0, :, src:src + width].astype(o_ref.dtype)


def _pack_w_in(w_in):
    depth, d, n_in = w_in.shape
    tr = 128
    return pl.pallas_call(
        _pack_kernel,
        grid=(depth, d // tr),
        in_specs=[pl.BlockSpec((1, tr, n_in), lambda l, i: (l, i, 0))],
        out_specs=pl.BlockSpec((1, tr, W_PACK), lambda l, i: (l, i, 0)),
        out_shape=jax.ShapeDtypeStruct((depth, d, W_PACK), BF16),
        compiler_params=_params("parallel", "parallel"),
        name="pack_w_in",
    )(w_in)


def kernel(x, c, w_ada, b_ada, norm_mix, norm_ffn, w_in, hgrn_lb_logits, hgrn_out_norm, dsa_q_norm, dsa_k_norm,
           gla_gate_up, gla_gate_bias, gla_out_norm, w_branch, w_out, peer_w_query, peer_sub_keys, peer_u, peer_v):
    bsz, seq, d = x.shape
    depth = w_in.shape[0]
    t = bsz * seq
    x2 = x.reshape(t, d)
    mod = _ada(c, w_ada, b_ada)
    w_pack = _pack_w_in(w_in)

    for l in range(depth):
        shift_m, scale_m, gate_m, shift_f, scale_f, gate_f = [
            mod[l, :, k * d:(k + 1) * d].reshape(bsz, 1, d) for k in range(6)]
        za, zb, zi, zc, zm = _inproj(x2, norm_mix[l].reshape(1, d), shift_m, scale_m, w_pack, l, seq)

        ya = _scan("hgrn", l, za, (hgrn_lb_logits,), hgrn_out_norm[l].reshape(1, HEAD_DIM), bsz, seq)
        yb = _dsa(zb, zi, dsa_q_norm[l].reshape(1, HEAD_DIM), dsa_k_norm[l].reshape(1, HEAD_DIM), bsz, seq)
        gup = jnp.pad(_pad_heads(gla_gate_up[l], GLA_DK), ((0, LANES - GLA_GATE_RANK), (0, 0)))
        gb = _pad_heads(gla_gate_bias[l].reshape(1, HEADS * GLA_DK), GLA_DK)
        yc = _scan("gla", l, zc, (gup, gb), gla_out_norm[l].reshape(1, HEAD_DIM), bsz, seq)

        x2 = _merge(ya, yb, yc, zm, x2, gate_m, w_branch[l].astype(BF16), w_out[l].astype(BF16), seq)

        ht, a1, cnt, b2, r2 = _route(
            x2, norm_ffn[l].reshape(1, d), shift_f, scale_f, _transpose_cast(peer_w_query[l], BF16),
            peer_sub_keys[l].reshape(PEER_HEADS * 2, N_KEYS, -1).astype(BF16), seq)
        x2 = _peer(ht, peer_u[l].astype(BF16), _transpose_cast(peer_v[l], BF16), a1, cnt, b2, r2, x2, gate_f, seq)

    return x2.reshape(bsz, seq, d)
```

```python
import functools

import numpy as np
import jax
import jax.numpy as jnp
from jax import lax
from jax.experimental import pallas as pl
from jax.experimental.pallas import tpu as pltpu

F32, BF16, I32 = jnp.float32, jnp.bfloat16, jnp.int32

D_MODEL = 1024
HEADS = 4
HEAD_DIM = 128
BRANCH_WIDTH = HEADS * HEAD_DIM
IDX_HEADS = 4
IDX_DIM = 64
DSA_MAX_TOPK = 256
QBLOCK = 128
GLA_DK = 64
GLA_GATE_RANK = 16
GLA_TAU = 16.0
PEER_HEADS = 8
N_KEYS = 128
PEER_TOPK = 16
PEER_TE = 1024
EPS = 1e-6

LANES = 128
SUBLANES = 8
VMEM_LIMIT_BYTES = 56 * 1024 * 1024

NEG_BIG = -1e30
INT_MIN = np.int32(-2 ** 31)

W_A = 4 * BRANCH_WIDTH
W_B = 3 * BRANCH_WIDTH
W_I = 3 * LANES
W_C = 4 * BRANCH_WIDTH + LANES
W_M = 3 * D_MODEL
W_PACK = W_A + W_B + W_I + W_C + W_M


def _dot(a, b):
    return jnp.dot(a, b, preferred_element_type=F32)


def _dot_nt(a, b):
    return lax.dot_general(a, b, (((1,), (1,)), ((), ())), preferred_element_type=F32)


def _dot_tn(a, b):
    return lax.dot_general(a, b, (((0,), (0,)), ((), ())), preferred_element_type=F32)


def _split2(x):
    hi = x.astype(BF16)
    lo = (x - hi.astype(F32)).astype(BF16)
    return hi, lo


def _split3(x):
    hi = x.astype(BF16)
    r = x - hi.astype(F32)
    mid = r.astype(BF16)
    lo = (r - mid.astype(F32)).astype(BF16)
    return hi, mid, lo


def _params(*sem):
    return pltpu.CompilerParams(dimension_semantics=sem, vmem_limit_bytes=VMEM_LIMIT_BYTES)


def _modulate(x, gain, shift, scale):
    ms = jnp.mean(x * x, axis=-1, keepdims=True)
    return x * lax.rsqrt(ms + EPS) * gain * (1.0 + scale) + shift


def _ada_kernel(c_ref, w_ref, b_ref, o_ref):
    c = c_ref[...]
    sc = c * jax.nn.sigmoid(c)
    a_hi, a_lo = _split2(sc)
    w_hi, w_lo = _split2(w_ref[0])
    o_ref[0] = _dot(a_hi, w_hi) + _dot(a_hi, w_lo) + _dot(a_lo, w_hi) + b_ref[0]


def _ada(c, w_ada, b_ada):
    depth, d, n = w_ada.shape
    bsz = c.shape[0]
    tn = 1536
    return pl.pallas_call(
        _ada_kernel,
        grid=(depth, n // tn),
        in_specs=[
            pl.BlockSpec((bsz, d), lambda l, j: (0, 0)),
            pl.BlockSpec((1, d, tn), lambda l, j: (l, 0, j)),
            pl.BlockSpec((1, 1, tn), lambda l, j: (l, 0, j)),
        ],
        out_specs=pl.BlockSpec((1, bsz, tn), lambda l, j: (l, 0, j)),
        out_shape=jax.ShapeDtypeStruct((depth, bsz, n), F32),
        compiler_params=_params("parallel", "parallel"),
        name="ada",
    )(c, w_ada, b_ada.reshape(depth, 1, n))


def _inproj_kernel(x_ref, gain_ref, shift_ref, scale_ref, w_ref, *out_refs):
    h = _modulate(x_ref[...], gain_ref[...], shift_ref[0], scale_ref[0]).astype(BF16)
    off = 0
    for o_ref in out_refs:
        width = o_ref.shape[1]
        for c0 in range(0, width, 512):
            c1 = min(c0 + 512, width)
            o_ref[:, c0:c1] = _dot(h, w_ref[:, off + c0:off + c1]).astype(o_ref.dtype)
        off += width


def _inproj(x2, gain, shift, scale, w_pack, layer, seq):
    t, d = x2.shape
    tm = 256
    per_b = seq // tm
    widths = (W_A, W_B, W_I, W_C, W_M)
    dtypes = (F32, BF16, F32, F32, BF16)
    return pl.pallas_call(
        _inproj_kernel,
        grid=(t // tm,),
        in_specs=[
            pl.BlockSpec((tm, d), lambda i: (i, 0)),
            pl.BlockSpec((1, d), lambda i: (0, 0)),
            pl.BlockSpec((1, 1, d), lambda i: (i // per_b, 0, 0)),
            pl.BlockSpec((1, 1, d), lambda i: (i // per_b, 0, 0)),
            pl.BlockSpec((None, d, W_PACK), lambda i: (layer, 0, 0), pipeline_mode=pl.Buffered(1)),
        ],
        out_specs=[pl.BlockSpec((tm, w), lambda i: (i, 0)) for w in widths],
        out_shape=[jax.ShapeDtypeStruct((t, w), dt) for w, dt in zip(widths, dtypes)],
        compiler_params=_params("parallel"),
        name="inproj",
    )(x2, gain, shift, scale, w_pack)


def _group_row_bcast(x, group, row):
    n, w = x.shape
    if group >= SUBLANES:
        x3 = x.reshape(n // group, group, w)
        return jnp.broadcast_to(x3[:, row:row + 1, :], x3.shape).reshape(n, w)
    x3 = x.reshape(n // SUBLANES, SUBLANES, w)
    sub = lax.broadcasted_iota(I32, x3.shape, 1)
    n_groups = SUBLANES // group
    res = None
    for g in reversed(range(n_groups)):
        r = g * group + row
        bc = jnp.broadcast_to(x3[:, r:r + 1, :], x3.shape)
        res = bc if res is None else jnp.where(sub < (g + 1) * group, bc, res)
    return res.reshape(n, w)


def _scan_kernel(mode, layer, *refs):
    if mode == "hgrn":
        q_ref, f_ref, v_ref, g_ref, lbl_ref, gain_ref, lv_ref, tri_ref, y_ref, st_ref = refs
    else:
        q_ref, k_ref, v_ref, g_ref, code_ref, gup_ref, gb_ref, gain_ref, lv_ref, tri_ref, y_ref, st_ref = refs
    lt = q_ref.shape[0]
    n_levels = lt.bit_length() - 1

    @pl.when(pl.program_id(1) == 0)
    def _():
        st_ref[...] = jnp.zeros_like(st_ref)

    tri = tri_ref[...]
    lv = lv_ref[...]
    for h in range(HEADS):
        cols = slice(h * HEAD_DIM, (h + 1) * HEAD_DIM)
        if mode == "hgrn":
            lbl = lbl_ref[:, cols]
            e = jnp.exp(lbl - jnp.max(lbl, axis=0, keepdims=True))
            p = e / jnp.sum(e, axis=0, keepdims=True)
            lb = jnp.zeros((1, p.shape[1]), F32)
            for l2 in range(1, layer + 1):
                lb = lb + p[l2:l2 + 1, :]
            f = lb + (1.0 - lb) * jax.nn.sigmoid(f_ref[:, cols])
            lg = jnp.log(f)
            kk = 1.0 - f
            q = q_ref[:, cols] * (HEAD_DIM ** -0.5)
        else:
            c_hi, c_lo = _split2(code_ref[...])
            u_hi, u_lo = _split2(gup_ref[:, cols])
            z = _dot(c_hi, u_hi) + _dot(c_hi, u_lo) + _dot(c_lo, u_hi) + gb_ref[:, cols]
            lg = (jnp.minimum(z, 0.0) - jnp.log1p(jnp.exp(-jnp.abs(z)))) * (1.0 / GLA_TAU)
            kk = k_ref[:, cols]
            q = q_ref[:, cols] * (GLA_DK ** -0.5)

        g_hi, g_mid, g_lo = _split3(lg)
        b = _dot(tri, g_hi) + _dot(tri, g_mid) + _dot(tri, g_lo)

        s = jnp.where(lv == -1, _dot_nt(q.astype(BF16), kk.astype(BF16)), 0.0)
        for l in range(n_levels):
            n = 1 << l
            bref = _group_row_bcast(b, 2 * n, n - 1)
            ql = (q * jnp.exp(jnp.minimum(b - bref, 0.0))).astype(BF16)
            kl = (kk * jnp.exp(jnp.minimum(bref - b, 0.0))).astype(BF16)
            s = jnp.where(lv == l, _dot_nt(ql, kl), s)

        vb = v_ref[:, cols].astype(BF16)
        st = st_ref[h]
        o = _dot(s.astype(BF16), vb) + _dot_nt((q * jnp.exp(b)).astype(BF16), st.astype(BF16))
        b_last = b[lt - 1:lt, :]
        kd = (kk * jnp.exp(b_last - b)).astype(BF16)
        st_ref[h] = st * jnp.exp(b_last) + _dot_tn(vb, kd)

        ms = jnp.mean(o * o, axis=-1, keepdims=True)
        g = g_ref[:, cols]
        y = o * lax.rsqrt(ms + EPS) * gain_ref[...] * (g * jax.nn.sigmoid(g))
        y_ref[:, cols] = y.astype(y_ref.dtype)


def _scan_consts(lt):
    idx = np.arange(lt)
    x = idx[:, None] ^ idx[None, :]
    lvl = np.floor(np.log2(np.maximum(x, 1))).astype(np.int32)
    lv = np.where(idx[None, :] < idx[:, None], lvl, np.where(x == 0, -1, -2)).astype(np.int32)
    tri = (idx[None, :] <= idx[:, None]).astype(np.float32)
    return jnp.asarray(lv), jnp.asarray(tri, dtype=BF16)


def _scan(mode, layer, z, extra, out_gain, bsz, seq):
    t = z.shape[0]
    lt = 256
    per_b = seq // lt
    lv, tri = _scan_consts(lt)

    def col(cb):
        return pl.BlockSpec((lt, BRANCH_WIDTH), lambda b, c: (b * per_b + c, cb))

    const2 = lambda b, c: (0, 0)
    in_specs = [col(0), col(1), col(2), col(3)]
    args = [z, z, z, z]
    if mode == "hgrn":
        (lb_logits,) = extra
        in_specs.append(pl.BlockSpec(lb_logits.shape, const2))
        args.append(lb_logits)
    else:
        gup, gb = extra
        in_specs += [pl.BlockSpec((lt, LANES), lambda b, c: (b * per_b + c, 4 * HEADS)),
                     pl.BlockSpec(gup.shape, const2), pl.BlockSpec(gb.shape, const2)]
        args += [z, gup, gb]
    in_specs += [pl.BlockSpec((1, HEAD_DIM), const2), pl.BlockSpec((lt, lt), const2), pl.BlockSpec((lt, lt), const2)]
    args += [out_gain, lv, tri]
    return pl.pallas_call(
        functools.partial(_scan_kernel, mode, layer),
        grid=(bsz, per_b),
        in_specs=in_specs,
        out_specs=pl.BlockSpec((lt, BRANCH_WIDTH), lambda b, c: (b * per_b + c, 0)),
        out_shape=jax.ShapeDtypeStruct((t, BRANCH_WIDTH), BF16),
        scratch_shapes=[pltpu.VMEM((HEADS, HEAD_DIM, HEAD_DIM), F32)],
        compiler_params=_params("parallel", "arbitrary"),
        name="scan_" + mode,
    )(*args)


_KCH = 256


def _dsa_kernel(topk, q_ref, k_ref, v_ref, ziq_ref, zik_ref, qg_ref, kg_ref, stri_ref, o_ref,
                kn_ref, vt_ref, kih_ref, kil_ref, keys_ref, bias_ref, acc_ref):
    j = pl.program_id(1)
    seq = k_ref.shape[0]

    @pl.when(j == 0)
    def _prep():
        def body(c, carry):
            r0 = pl.multiple_of(c * _KCH, _KCH)
            kc = k_ref[pl.ds(r0, _KCH), :].astype(F32)
            for h in range(HEADS):
                kh = kc[:, h * HEAD_DIM:(h + 1) * HEAD_DIM]
                ms = jnp.mean(kh * kh, axis=-1, keepdims=True)
                kn_ref[pl.ds(r0, _KCH), h * HEAD_DIM:(h + 1) * HEAD_DIM] = (
                    kh * lax.rsqrt(ms + EPS) * kg_ref[...]).astype(BF16)
            vt_ref[c] = v_ref[pl.ds(r0, _KCH), :].astype(F32).T.astype(BF16)
            ki = zik_ref[pl.ds(r0, _KCH), :]
            hi = ki.astype(BF16)
            kih_ref[pl.ds(r0, _KCH), :] = hi
            kil_ref[pl.ds(r0, _KCH), :] = (ki - hi.astype(F32)).astype(BF16)
            return carry
        lax.fori_loop(0, seq // _KCH, body, 0)

    n_ch = (j * QBLOCK + QBLOCK + _KCH - 1) // _KCH
    ziq = ziq_ref[...]
    w_t = ziq[:, 2 * LANES:3 * LANES].T
    qi = jnp.concatenate([ziq[:, h * IDX_DIM:(h + 1) * IDX_DIM] for h in range(IDX_HEADS)], axis=0)
    qi_hi, qi_lo = _split2(qi * (IDX_DIM ** -0.5))
    w_rows = [w_t[IDX_DIM + h:IDX_DIM + h + 1, :] * (IDX_HEADS ** -0.5) for h in range(IDX_HEADS)]
    q_pos = j * QBLOCK + lax.broadcasted_iota(I32, (_KCH, QBLOCK), 1)
    row_i = lax.broadcasted_iota(I32, (_KCH, QBLOCK), 0)

    def idx_body(c, carry):
        r0 = pl.multiple_of(c * _KCH, _KCH)
        k_hi = kih_ref[pl.ds(r0, _KCH), :][:, :IDX_DIM]
        k_lo = kil_ref[pl.ds(r0, _KCH), :][:, :IDX_DIM]
        logit = _dot_nt(k_hi, qi_hi) + _dot_nt(k_hi, qi_lo) + _dot_nt(k_lo, qi_hi)
        score = jnp.zeros((_KCH, QBLOCK), F32)
        for h in range(IDX_HEADS):
            score = score + w_rows[h] * jnp.maximum(logit[:, h * QBLOCK:(h + 1) * QBLOCK], 0.0)
        score = jnp.where(score == 0.0, 0.0, score)
        bits = lax.bitcast_convert_type(score, I32)
        key = jnp.where(bits < 0, bits ^ np.int32(0x7FFFFFFF), bits)
        keys_ref[pl.ds(r0, _KCH), :] = jnp.where(r0 + row_i <= q_pos, key, INT_MIN)
        return carry
    lax.fori_loop(0, n_ch, idx_body, 0)

    def count(pred):
        def body(c, acc):
            r0 = pl.multiple_of(c * _KCH, _KCH)
            m = jnp.where(pred(keys_ref[pl.ds(r0, _KCH), :]), 1, 0).astype(I32)
            return acc + jnp.sum(m.reshape(_KCH // SUBLANES, SUBLANES, QBLOCK), axis=0)
        acc = lax.fori_loop(0, n_ch, body, jnp.zeros((SUBLANES, QBLOCK), I32))
        return jnp.sum(acc, axis=0, keepdims=True)

    def bisect(it, thr):
        cand = thr ^ jnp.left_shift(jnp.int32(1), 31 - it)
        return jnp.where(count(lambda x: x >= cand) >= topk, cand, thr)
    thr = lax.fori_loop(0, 32, bisect, jnp.full((1, QBLOCK), INT_MIN, I32))
    need = (topk - count(lambda x: x > thr)).astype(F32)

    def sel_body(c, seen):
        r0 = pl.multiple_of(c * _KCH, _KCH)
        x = keys_ref[pl.ds(r0, _KCH), :]
        eq = jnp.where(x == thr, 1.0, 0.0)
        rank = seen + _dot(stri_ref[...], eq.astype(BF16))
        tie_ok = jnp.where(x == thr, jnp.where(rank < need, 0.0, NEG_BIG), NEG_BIG)
        bias = jnp.where(x > thr, 0.0, tie_ok)
        bias_ref[pl.ds(r0, _KCH), :] = jnp.where(x == INT_MIN, NEG_BIG, bias)
        return seen + jnp.sum(eq, axis=0, keepdims=True)
    lax.fori_loop(0, n_ch, sel_body, jnp.zeros((1, QBLOCK), F32))

    qf = q_ref[...].astype(F32)
    qn = []
    for h in range(HEADS):
        qh = qf[:, h * HEAD_DIM:(h + 1) * HEAD_DIM]
        ms = jnp.mean(qh * qh, axis=-1, keepdims=True)
        qn.append((qh * lax.rsqrt(ms + EPS) * qg_ref[...] * (HEAD_DIM ** -0.5)).astype(BF16))
    acc_ref[...] = jnp.zeros_like(acc_ref)

    @pl.when(n_ch % 2 == 1)
    def _():
        bias_ref[pl.ds(pl.multiple_of(n_ch * _KCH, _KCH), _KCH), :] = jnp.full((_KCH, QBLOCK), NEG_BIG, F32)

    def att_body(c, carry):
        r0 = pl.multiple_of(c * (2 * _KCH), 2 * _KCH)
        bias = bias_ref[pl.ds(r0, 2 * _KCH), :]
        out = []
        for h in range(HEADS):
            m, l = carry[h]
            hd = slice(h * HEAD_DIM, (h + 1) * HEAD_DIM)
            sc = _dot_nt(kn_ref[pl.ds(r0, 2 * _KCH), hd], qn[h]) + bias
            m_new = jnp.maximum(m, jnp.max(sc, axis=0, keepdims=True))
            p = jnp.exp(sc - m_new)
            alpha = jnp.exp(m - m_new)
            pb = p.astype(BF16)
            pv = _dot(vt_ref[2 * c, hd, :], pb[:_KCH]) + _dot(vt_ref[2 * c + 1, hd, :], pb[_KCH:])
            acc_ref[h] = alpha * acc_ref[h] + pv
            out.append((m_new, alpha * l + jnp.sum(p, axis=0, keepdims=True)))
        return tuple(out)
    init = tuple((jnp.full((1, QBLOCK), NEG_BIG, F32), jnp.zeros((1, QBLOCK), F32)) for _ in range(HEADS))
    stats = lax.fori_loop(0, (n_ch + 1) // 2, att_body, init)
    for h in range(HEADS):
        o_ref[:, h * HEAD_DIM:(h + 1) * HEAD_DIM] = (acc_ref[h] / stats[h][1]).T.astype(o_ref.dtype)


def _dsa(zb, zi, q_gain, k_gain, bsz, seq):
    t = zb.shape[0]
    topk = min(DSA_MAX_TOPK, seq // 4)
    nq = seq // QBLOCK
    assert seq % (2 * _KCH) == 0
    idx = np.arange(_KCH)
    stri = jnp.asarray((idx[None, :] < idx[:, None]).astype(np.float32), dtype=BF16)
    const2 = lambda b, j: (0, 0)
    return pl.pallas_call(
        functools.partial(_dsa_kernel, topk),
        grid=(bsz, nq),
        in_specs=[
            pl.BlockSpec((QBLOCK, BRANCH_WIDTH), lambda b, j: (b * nq + j, 0)),
            pl.BlockSpec((seq, BRANCH_WIDTH), lambda b, j: (b, 1)),
            pl.BlockSpec((seq, BRANCH_WIDTH), lambda b, j: (b, 2)),
            pl.BlockSpec((QBLOCK, W_I), lambda b, j: (b * nq + j, 0)),
            pl.BlockSpec((seq, LANES), lambda b, j: (b, 2)),
            pl.BlockSpec((1, HEAD_DIM), const2),
            pl.BlockSpec((1, HEAD_DIM), const2),
            pl.BlockSpec((_KCH, _KCH), const2),
        ],
        out_specs=pl.BlockSpec((QBLOCK, BRANCH_WIDTH), lambda b, j: (b * nq + j, 0)),
        out_shape=jax.ShapeDtypeStruct((t, BRANCH_WIDTH), BF16),
        scratch_shapes=[
            pltpu.VMEM((seq, BRANCH_WIDTH), BF16),
            pltpu.VMEM((seq // _KCH, BRANCH_WIDTH, _KCH), BF16),
            pltpu.VMEM((seq, LANES), BF16),
            pltpu.VMEM((seq, LANES), BF16),
            pltpu.VMEM((seq, QBLOCK), I32),
            pltpu.VMEM((seq, QBLOCK), F32),
            pltpu.VMEM((HEADS, HEAD_DIM, QBLOCK), F32),
        ],
        compiler_params=_params("parallel", "arbitrary"),
        name="dsa",
    )(zb, zb, zb, zi, zi, q_gain, k_gain, stri)


def _merge_kernel(ya_ref, yb_ref, yc_ref, zm_ref, x_ref, gm_ref, wb_ref, wo_ref, o_ref):
    d = x_ref.shape[1]
    mixed = None
    for g, y_ref in enumerate((ya_ref, yb_ref, yc_ref)):
        up = _dot(y_ref[...], wb_ref[g])
        term = jax.nn.sigmoid(zm_ref[:, g * d:(g + 1) * d].astype(F32)) * up
        mixed = term if mixed is None else mixed + term
    o_ref[...] = x_ref[...] + gm_ref[0] * _dot(mixed.astype(BF16), wo_ref[...])


def _merge(ya, yb, yc, zm, x2, gate_m, w_branch, w_out, seq):
    t, d = x2.shape
    tm = 256
    per_b = seq // tm
    row = lambda i: (i, 0)
    return pl.pallas_call(
        _merge_kernel,
        grid=(t // tm,),
        in_specs=[
            pl.BlockSpec((tm, BRANCH_WIDTH), row), pl.BlockSpec((tm, BRANCH_WIDTH), row),
            pl.BlockSpec((tm, BRANCH_WIDTH), row), pl.BlockSpec((tm, W_M), row), pl.BlockSpec((tm, d), row),
            pl.BlockSpec((1, 1, d), lambda i: (i // per_b, 0, 0)),
            pl.BlockSpec(w_branch.shape, lambda i: (0, 0, 0)),
            pl.BlockSpec(w_out.shape, lambda i: (0, 0)),
        ],
        out_specs=pl.BlockSpec((tm, d), row),
        out_shape=jax.ShapeDtypeStruct((t, d), F32),
        compiler_params=_params("parallel"),
        name="merge",
    )(ya, yb, yc, zm, x2, gate_m, w_branch, w_out)


_CAND_ROWS = 72


def _cand_layout():
    flat = np.zeros((_CAND_ROWS,), np.int32)
    live = np.zeros((_CAND_ROWS,), bool)
    r = 0
    for b in range(16):
        flat[r], live[r] = b, True
        r += 1
    for a in (1, 2, 3):
        for b in range(8):
            flat[r], live[r] = a * 16 + b, True
            r += 1
    for a in range(16):
        flat[r], live[r] = a * 16, a >= 4
        r += 1
    for a in range(8):
        flat[r], live[r] = a * 16 + 1, a >= 4
        r += 1
    for a in range(8):
        flat[r], live[r] = a * 16 + 2, a == 4
        r += 1
    flat = np.where(live, flat, 1000 + np.arange(_CAND_ROWS))
    return flat.astype(np.int32), live


def _any_column_differs(count, expected):
    return jnp.max(jnp.abs(count - expected)) > 0.0


def _top16_pair(scores, row_i, vals_refs, rank_refs):
    work = list(scores)
    rank = [jnp.full(s.shape, 127.0, F32) for s in scores]
    for a in range(PEER_TOPK):
        for k in range(len(scores)):
            m = jnp.max(work[k], axis=0, keepdims=True)
            hit = work[k] == m
            work[k] = jnp.where(hit, -jnp.inf, work[k])
            rank[k] = jnp.where(hit, float(a), rank[k])
            vals_refs[k][a:a + 1, :] = m
    tied = None
    for k in range(len(scores)):
        rank_refs[k][...] = rank[k]
        gone = jnp.sum(jnp.where(rank[k] < 127.0, 1.0, 0.0), axis=0, keepdims=True)
        bad = _any_column_differs(gone, float(PEER_TOPK))
        tied = bad if tied is None else jnp.logical_or(tied, bad)

    @pl.when(tied)
    def _():
        for k, s in enumerate(scores):
            work = s
            rank = jnp.full(s.shape, 127.0, F32)
            for a in range(PEER_TOPK):
                m = jnp.max(work, axis=0, keepdims=True)
                idx = jnp.min(jnp.where(work == m, row_i, N_KEYS), axis=0, keepdims=True)
                hit = row_i == idx
                work = jnp.where(hit, -jnp.inf, work)
                rank = jnp.where(hit, float(a), rank)
                vals_refs[k][a:a + 1, :] = m
            rank_refs[k][...] = rank


def _route_kernel(x_ref, gain_ref, shift_ref, scale_ref, wq_ref, sk_ref, flat_ref, live_ref,
                  ht_ref, a1_ref, cnt_ref, b2_ref, r2_ref, qt_ref, v1_ref, v2_ref, r1_ref, sel_ref):
    tt = x_ref.shape[0]
    h2 = _modulate(x_ref[...], gain_ref[...], shift_ref[0], scale_ref[0])
    ht = h2.T.astype(BF16)
    for g in range(tt // _TG):
        ht_ref[g] = ht[:, g * _TG:(g + 1) * _TG]
    qt_ref[...] = _dot(wq_ref[...], ht)
    row_i = lax.broadcasted_iota(I32, (N_KEYS, tt), 0)
    flat = flat_ref[...]
    live = live_ref[...] > 0.0

    for h in range(PEER_HEADS):
        s1 = _dot(sk_ref[2 * h], qt_ref[(2 * h) * N_KEYS:(2 * h + 1) * N_KEYS, :].astype(BF16))
        s2 = _dot(sk_ref[2 * h + 1], qt_ref[(2 * h + 1) * N_KEYS:(2 * h + 2) * N_KEYS, :].astype(BF16))
        _top16_pair((s1, s2), row_i, (v1_ref, v2_ref), (r1_ref, r2_ref.at[h]))
        a1_ref[h] = jnp.exp(s1 - v1_ref[0:1, :])
        b2u = jnp.exp(s2 - v2_ref[0:1, :])

        v1 = v1_ref[...]
        v2 = v2_ref[...]
        cand = jnp.concatenate(
            [v1[0:1, :] + v2] + [v1[a:a + 1, :] + v2[0:8, :] for a in (1, 2, 3)]
            + [v1 + v2[0:1, :], v1[0:8, :] + v2[1:2, :], v1[0:8, :] + v2[2:3, :]], axis=0)
        cand = jnp.where(live, cand, -jnp.inf)
        top = v1[0:1, :] + v2[0:1, :]
        ex = jnp.exp(cand - top)
        sel = jnp.zeros(cand.shape, F32)
        work = cand
        for _ in range(PEER_TOPK):
            hit = work == jnp.max(work, axis=0, keepdims=True)
            work = jnp.where(hit, -jnp.inf, work)
            sel = jnp.where(hit, 1.0, sel)
        sel_ref[...] = sel

        @pl.when(_any_column_differs(jnp.sum(sel, axis=0, keepdims=True), float(PEER_TOPK)))
        def _():
            sel = jnp.zeros(cand.shape, F32)
            work = cand
            for _ in range(PEER_TOPK):
                m = jnp.max(work, axis=0, keepdims=True)
                fi = jnp.min(jnp.where(work == m, flat, 9999), axis=0, keepdims=True)
                hit = flat == fi
                work = jnp.where(hit, -jnp.inf, work)
                sel = jnp.where(hit, 1.0, sel)
            sel_ref[...] = sel
        sel = sel_ref[...]
        r1 = r1_ref[...]
        z = jnp.sum(sel * ex, axis=0, keepdims=True)
        b2_ref[h] = b2u * (0.5 / z)

        zeros8 = jnp.zeros((8, tt), F32)
        cnt_hi = (sel[40:56, :] + jnp.concatenate([sel[56:64, :], zeros8], axis=0)
                  + jnp.concatenate([sel[64:72, :], zeros8], axis=0))
        cnt_lo = [jnp.sum(sel[0:16, :], axis=0, keepdims=True)] + [
            jnp.sum(sel[16 + 8 * (a - 1):24 + 8 * (a - 1), :], axis=0, keepdims=True) for a in (1, 2, 3)]
        cnt = jnp.zeros(s1.shape, F32)
        for a in range(PEER_TOPK):
            ca = cnt_lo[a] if a < 4 else cnt_hi[a:a + 1, :]
            cnt = jnp.where(r1 == float(a), ca, cnt)
        cnt_ref[h] = cnt


def _route(x2, gain, shift, scale, wq_t, sub_keys, seq):
    t, d = x2.shape
    tt = 256
    per_b = seq // tt
    flat_np, live_np = _cand_layout()
    flat = jnp.asarray(np.broadcast_to(flat_np[:, None], (_CAND_ROWS, tt)).copy())
    live = jnp.asarray(np.broadcast_to(live_np[:, None], (_CAND_ROWS, tt)).astype(np.float32))
    dense = jax.ShapeDtypeStruct((PEER_HEADS, N_KEYS, t), F32)
    dense_spec = pl.BlockSpec((PEER_HEADS, N_KEYS, tt), lambda i: (0, 0, i))
    const2 = lambda i: (0, 0)
    return pl.pallas_call(
        _route_kernel,
        grid=(t // tt,),
        in_specs=[
            pl.BlockSpec((tt, d), lambda i: (i, 0)),
            pl.BlockSpec((1, d), const2),
            pl.BlockSpec((1, 1, d), lambda i: (i // per_b, 0, 0)),
            pl.BlockSpec((1, 1, d), lambda i: (i // per_b, 0, 0)),
            pl.BlockSpec(wq_t.shape, const2),
            pl.BlockSpec(sub_keys.shape, lambda i: (0, 0, 0)),
            pl.BlockSpec((_CAND_ROWS, tt), const2),
            pl.BlockSpec((_CAND_ROWS, tt), const2),
        ],
        out_specs=[pl.BlockSpec((tt // _TG, d, _TG), lambda i: (i, 0, 0)),
                   dense_spec, dense_spec, dense_spec, dense_spec],
        out_shape=[jax.ShapeDtypeStruct((t // _TG, d, _TG), BF16), dense, dense, dense, dense],
        scratch_shapes=[
            pltpu.VMEM((PEER_HEADS * 2 * N_KEYS, tt), F32),
            pltpu.VMEM((PEER_TOPK, tt), F32),
            pltpu.VMEM((PEER_TOPK, tt), F32),
            pltpu.VMEM((N_KEYS, tt), F32),
            pltpu.VMEM((_CAND_ROWS, tt), F32),
        ],
        compiler_params=_params("parallel"),
        name="peer_route",
    )(x2, gain, shift, scale, wq_t, sub_keys, flat, live)


_JB = 4
_IB = 4
_TG = 256
_RB = _IB * N_KEYS


def _peer_kernel(n_tiles, ht_ref, u_ref, vt_ref, a1_ref, cnt_ref, b2_ref, r2_ref, x_ref, gf_ref, o_ref,
                 acc_ref, at_ref, gt_ref, bc_ref):
    s = pl.program_id(1)
    n_g, te, tg = at_ref.shape
    tt = n_g * tg
    n_i = te // N_KEYS
    n_jb = N_KEYS // (SUBLANES * _JB)
    assert n_i == SUBLANES
    assert _RB == _IB * N_KEYS

    def bcast_rows():
        i0 = pl.multiple_of(s * n_i, SUBLANES)
        for h in range(PEER_HEADS):
            a1g = a1_ref[h, pl.ds(i0, n_i), :]
            cng = cnt_ref[h, pl.ds(i0, n_i), :]
            for ii in range(n_i):
                bc_ref[2 * (h * n_i + ii)] = jnp.broadcast_to(a1g[ii:ii + 1, :], (SUBLANES, tt))
                bc_ref[2 * (h * n_i + ii) + 1] = jnp.broadcast_to(cng[ii:ii + 1, :], (SUBLANES, tt))

    def mask_block(tb, jb, ib):
        g, lane0 = divmod(tb * LANES, tg)
        lanes = slice(tb * LANES, (tb + 1) * LANES)
        lanes_g = slice(lane0, lane0 + LANES)
        j0 = jb * (SUBLANES * _JB)
        w = [[None] * _JB for _ in range(_IB)]
        for h in range(PEER_HEADS):
            r2 = [r2_ref[h, j0 + SUBLANES * k:j0 + SUBLANES * (k + 1), lanes] for k in range(_JB)]
            b2 = [b2_ref[h, j0 + SUBLANES * k:j0 + SUBLANES * (k + 1), lanes] for k in range(_JB)]
            for di in range(_IB):
                row = 2 * (h * n_i + ib * _IB + di)
                a1v = bc_ref[row, :, lanes]
                cv = bc_ref[row + 1, :, lanes]
                for k in range(_JB):
                    term = jnp.where(r2[k] < cv, a1v * b2[k], 0.0)
                    w[di][k] = term if w[di][k] is None else w[di][k] + term
        for di in range(_IB):
            r0 = (ib * _IB + di) * N_KEYS + j0
            rows = slice(r0, r0 + SUBLANES * _JB)
            a = at_ref[g, rows, lanes_g]
            act2 = a + a * lax.erf(a * np.float32(np.sqrt(0.5)))
            gt_ref[g, rows, lanes_g] = (jnp.concatenate(w[di], axis=0) * act2).astype(BF16)

    @pl.when(s == 0)
    def _():
        acc_ref[...] = jnp.zeros_like(acc_ref)

    def pre_act(k):
        rows = slice(k * _RB, (k + 1) * _RB)
        for g in range(n_g):
            at_ref[g, rows, :] = _dot(u_ref[rows, :], ht_ref[g])

    def fold(k):
        rows = slice(k * _RB, (k + 1) * _RB)
        for g in range(n_g):
            acc_ref[g] += _dot(vt_ref[:, rows], gt_ref[g, rows, :])

    bcast_rows()
    n_slices = te // _RB
    pre_act(0)
    for k in range(n_slices):
        if k + 1 < n_slices:
            pre_act(k + 1)
        if k >= 1:
            fold(k - 1)
        for tb in range(tt // LANES):
            for jb in range(n_jb):
                mask_block(tb, jb, k)
    fold(n_slices - 1)

    @pl.when(s == n_tiles - 1)
    def _():
        for g in range(n_g):
            rows = slice(g * tg, (g + 1) * tg)
            o_ref[rows, :] = x_ref[rows, :] + gf_ref[0] * acc_ref[g].T


def _peer(ht, u, vt, a1, cnt, b2, r2, x2, gate_f, seq):
    t, d = x2.shape
    n_tiles = u.shape[0] // PEER_TE
    tt = 512
    per_b = seq // tt
    f32_spec = pl.BlockSpec((PEER_HEADS, N_KEYS, tt), lambda i, s: (0, 0, i))
    return pl.pallas_call(
        functools.partial(_peer_kernel, n_tiles),
        grid=(t // tt, n_tiles),
        in_specs=[
            pl.BlockSpec((tt // _TG, d, _TG), lambda i, s: (i, 0, 0)),
            pl.BlockSpec((PEER_TE, d), lambda i, s: (s, 0)),
            pl.BlockSpec((d, PEER_TE), lambda i, s: (0, s)),
            f32_spec, f32_spec, f32_spec, f32_spec,
            pl.BlockSpec((tt, d), lambda i, s: (i, 0)),
            pl.BlockSpec((1, 1, d), lambda i, s: (i // per_b, 0, 0)),
        ],
        out_specs=pl.BlockSpec((tt, d), lambda i, s: (i, 0)),
        out_shape=jax.ShapeDtypeStruct((t, d), F32),
        scratch_shapes=[
            pltpu.VMEM((tt // _TG, d, _TG), F32),
            pltpu.VMEM((tt // _TG, PEER_TE, _TG), F32),
            pltpu.VMEM((tt // _TG, PEER_TE, _TG), BF16),
            pltpu.VMEM((2 * PEER_HEADS * PEER_TE // N_KEYS, SUBLANES, tt), F32),
        ],
        compiler_params=_params("parallel", "arbitrary"),
        name="peer_experts",
    )(ht, u, vt, a1, cnt, b2, r2, x2, gate_f)


def _tcast_kernel(x_ref, o_ref):
    o_ref[...] = x_ref[...].T.astype(o_ref.dtype)


def _cast_kernel(x_ref, o_ref):
    o_ref[...] = x_ref[...].astype(o_ref.dtype)


def _layer_cast(x, layer, dtype, transpose):
    _, r, c = x.shape
    tr, tc = min(r, 1024), min(c, 1024)
    return pl.pallas_call(
        _tcast_kernel if transpose else _cast_kernel,
        grid=(r // tr, c // tc),
        in_specs=[pl.BlockSpec((None, tr, tc), lambda i, j: (layer, i, j))],
        out_specs=pl.BlockSpec((tc, tr), lambda i, j: (j, i)) if transpose else pl.BlockSpec((tr, tc), lambda i, j: (i, j)),
        out_shape=jax.ShapeDtypeStruct((c, r) if transpose else (r, c), dtype),
        compiler_params=_params("parallel", "parallel"),
        name="layer_cast",
    )(x)


def _pad_heads(w, width):
    d = w.shape[0]
    w = w.reshape(d, HEADS, width)
    return jnp.pad(w, ((0, 0), (0, 0), (0, HEAD_DIM - width))).reshape(d, HEADS * HEAD_DIM)


def _pack_segments():
    segs = [(0, W_A + W_B, 0)]
    n_idx = IDX_HEADS * IDX_DIM + IDX_DIM + IDX_HEADS
    src = W_A + W_B
    dst = W_A + W_B
    segs.append((src, n_idx, dst)); src += n_idx; dst += W_I
    for _ in range(2):
        for h in range(HEADS):
            segs.append((src + GLA_DK * h, GLA_DK, dst + HEAD_DIM * h))
        src += HEADS * GLA_DK; dst += BRANCH_WIDTH
    segs.append((src, BRANCH_WIDTH, dst)); src += BRANCH_WIDTH; dst += BRANCH_WIDTH
    code_src = src; src += GLA_GATE_RANK
    segs.append((src, BRANCH_WIDTH, dst)); src += BRANCH_WIDTH; dst += BRANCH_WIDTH
    segs.append((code_src, GLA_GATE_RANK, dst)); dst += LANES
    segs.append((src, W_M, dst))
    assert dst + W_M == W_PACK
    return segs


def _pack_kernel(w_ref, o_ref):
    o_ref[...] = jnp.zeros_like(o_ref)
    for src, width, dst in _pack_segments():
        o_ref[0, :, dst:dst + width] = w_ref[0, :, src:src + width].astype(o_ref.dtype)


def _pack_w_in(w_in):
    depth, d, n_in = w_in.shape
    tr = 128
    return pl.pallas_call(
        _pack_kernel,
        grid=(depth, d // tr),
        in_specs=[pl.BlockSpec((1, tr, n_in), lambda l, i: (l, i, 0))],
        out_specs=pl.BlockSpec((1, tr, W_PACK), lambda l, i: (l, i, 0)),
        out_shape=jax.ShapeDtypeStruct((depth, d, W_PACK), BF16),
        compiler_params=_params("parallel", "parallel"),
        name="pack_w_in",
    )(w_in)


def kernel(x, c, w_ada, b_ada, norm_mix, norm_ffn, w_in, hgrn_lb_logits, hgrn_out_norm, dsa_q_norm, dsa_k_norm,
           gla_gate_up, gla_gate_bias, gla_out_norm, w_branch, w_out, peer_w_query, peer_sub_keys, peer_u, peer_v):
    bsz, seq, d = x.shape
    depth = w_in.shape[0]
    t = bsz * seq
    x2 = x.reshape(t, d)
    mod = _ada(c, w_ada, b_ada)
    w_pack = _pack_w_in(w_in)

    for l in range(depth):
        shift_m, scale_m, gate_m, shift_f, scale_f, gate_f = [
            mod[l, :, k * d:(k + 1) * d].reshape(bsz, 1, d) for k in range(6)]
        za, zb, zi, zc, zm = _inproj(x2, norm_mix[l].reshape(1, d), shift_m, scale_m, w_pack, l, seq)

        ya = _scan("hgrn", l, za, (hgrn_lb_logits,), hgrn_out_norm[l].reshape(1, HEAD_DIM), bsz, seq)
        yb = _dsa(zb, zi, dsa_q_norm[l].reshape(1, HEAD_DIM), dsa_k_norm[l].reshape(1, HEAD_DIM), bsz, seq)
        gup = jnp.pad(_pad_heads(gla_gate_up[l], GLA_DK), ((0, LANES - GLA_GATE_RANK), (0, 0)))
        gb = _pad_heads(gla_gate_bias[l].reshape(1, HEADS * GLA_DK), GLA_DK)
        yc = _scan("gla", l, zc, (gup, gb), gla_out_norm[l].reshape(1, HEAD_DIM), bsz, seq)

        x2 = _merge(ya, yb, yc, zm, x2, gate_m, w_branch[l].astype(BF16), w_out[l].astype(BF16), seq)

        ht, a1, cnt, b2, r2 = _route(
            x2, norm_ffn[l].reshape(1, d), shift_f, scale_f, _layer_cast(peer_w_query, l, BF16, True),
            peer_sub_keys[l].reshape(PEER_HEADS * 2, N_KEYS, -1).astype(BF16), seq)
        x2 = _peer(ht, _layer_cast(peer_u, l, BF16, False), _layer_cast(peer_v, l, BF16, True),
                   a1, cnt, b2, r2, x2, gate_f, seq)

    return x2.reshape(bsz, seq, d)
```

```python
import functools

import numpy as np
import jax
import jax.numpy as jnp
from jax import lax
from jax.experimental import pallas as pl
from jax.experimental.pallas import tpu as pltpu

F32, BF16, I32 = jnp.float32, jnp.bfloat16, jnp.int32

D_MODEL = 1024
HEADS = 4
HEAD_DIM = 128
BRANCH_WIDTH = HEADS * HEAD_DIM
IDX_HEADS = 4
IDX_DIM = 64
DSA_MAX_TOPK = 256
QBLOCK = 128
GLA_DK = 64
GLA_GATE_RANK = 16
GLA_TAU = 16.0
PEER_HEADS = 8
N_KEYS = 128
PEER_TOPK = 16
PEER_TE = 1024
EPS = 1e-6

LANES = 128
SUBLANES = 8
VMEM_LIMIT_BYTES = 56 * 1024 * 1024

NEG_BIG = -1e30
INT_MIN = np.int32(-2 ** 31)

W_A = 4 * BRANCH_WIDTH
W_B = 3 * BRANCH_WIDTH
W_I = 3 * LANES
W_C = 4 * BRANCH_WIDTH + LANES
W_M = 3 * D_MODEL
W_PACK = W_A + W_B + W_I + W_C + W_M


def _dot(a, b):
    return jnp.dot(a, b, preferred_element_type=F32)


def _dot_nt(a, b):
    return lax.dot_general(a, b, (((1,), (1,)), ((), ())), preferred_element_type=F32)


def _dot_tn(a, b):
    return lax.dot_general(a, b, (((0,), (0,)), ((), ())), preferred_element_type=F32)


def _split2(x):
    hi = x.astype(BF16)
    lo = (x - hi.astype(F32)).astype(BF16)
    return hi, lo


def _split3(x):
    hi = x.astype(BF16)
    r = x - hi.astype(F32)
    mid = r.astype(BF16)
    lo = (r - mid.astype(F32)).astype(BF16)
    return hi, mid, lo


def _params(*sem):
    return pltpu.CompilerParams(dimension_semantics=sem, vmem_limit_bytes=VMEM_LIMIT_BYTES)


def _modulate(x, gain, shift, scale):
    ms = jnp.mean(x * x, axis=-1, keepdims=True)
    return x * lax.rsqrt(ms + EPS) * gain * (1.0 + scale) + shift


def _ada_kernel(c_ref, w_ref, b_ref, o_ref):
    c = c_ref[...]
    sc = c * jax.nn.sigmoid(c)
    a_hi, a_lo = _split2(sc)
    w_hi, w_lo = _split2(w_ref[0])
    o_ref[0] = _dot(a_hi, w_hi) + _dot(a_hi, w_lo) + _dot(a_lo, w_hi) + b_ref[0]


def _ada(c, w_ada, b_ada):
    depth, d, n = w_ada.shape
    bsz = c.shape[0]
    tn = 1536
    return pl.pallas_call(
        _ada_kernel,
        grid=(depth, n // tn),
        in_specs=[
            pl.BlockSpec((bsz, d), lambda l, j: (0, 0)),
            pl.BlockSpec((1, d, tn), lambda l, j: (l, 0, j)),
            pl.BlockSpec((1, 1, tn), lambda l, j: (l, 0, j)),
        ],
        out_specs=pl.BlockSpec((1, bsz, tn), lambda l, j: (l, 0, j)),
        out_shape=jax.ShapeDtypeStruct((depth, bsz, n), F32),
        compiler_params=_params("parallel", "parallel"),
        name="ada",
    )(c, w_ada, b_ada.reshape(depth, 1, n))


def _inproj_kernel(x_ref, gain_ref, shift_ref, scale_ref, w_ref, *out_refs):
    h = _modulate(x_ref[...], gain_ref[...], shift_ref[0], scale_ref[0]).astype(BF16)
    off = 0
    for o_ref in out_refs:
        width = o_ref.shape[1]
        for c0 in range(0, width, 512):
            c1 = min(c0 + 512, width)
            o_ref[:, c0:c1] = _dot(h, w_ref[:, off + c0:off + c1]).astype(o_ref.dtype)
        off += width


def _inproj(x2, gain, shift, scale, w_pack, layer, seq):
    t, d = x2.shape
    tm = 256
    per_b = seq // tm
    widths = (W_A, W_B, W_I, W_C, W_M)
    dtypes = (F32, BF16, F32, F32, BF16)
    return pl.pallas_call(
        _inproj_kernel,
        grid=(t // tm,),
        in_specs=[
            pl.BlockSpec((tm, d), lambda i: (i, 0)),
            pl.BlockSpec((1, d), lambda i: (0, 0)),
            pl.BlockSpec((1, 1, d), lambda i: (i // per_b, 0, 0)),
            pl.BlockSpec((1, 1, d), lambda i: (i // per_b, 0, 0)),
            pl.BlockSpec((None, d, W_PACK), lambda i: (layer, 0, 0), pipeline_mode=pl.Buffered(1)),
        ],
        out_specs=[pl.BlockSpec((tm, w), lambda i: (i, 0)) for w in widths],
        out_shape=[jax.ShapeDtypeStruct((t, w), dt) for w, dt in zip(widths, dtypes)],
        compiler_params=_params("parallel"),
        name="inproj",
    )(x2, gain, shift, scale, w_pack)


def _group_row_bcast(x, group, row):
    n, w = x.shape
    if group >= SUBLANES:
        x3 = x.reshape(n // group, group, w)
        return jnp.broadcast_to(x3[:, row:row + 1, :], x3.shape).reshape(n, w)
    x3 = x.reshape(n // SUBLANES, SUBLANES, w)
    sub = lax.broadcasted_iota(I32, x3.shape, 1)
    n_groups = SUBLANES // group
    res = None
    for g in reversed(range(n_groups)):
        r = g * group + row
        bc = jnp.broadcast_to(x3[:, r:r + 1, :], x3.shape)
        res = bc if res is None else jnp.where(sub < (g + 1) * group, bc, res)
    return res.reshape(n, w)


def _scan_kernel(mode, layer, *refs):
    if mode == "hgrn":
        q_ref, f_ref, v_ref, g_ref, lbl_ref, gain_ref, lv_ref, tri_ref, y_ref, st_ref = refs
    else:
        q_ref, k_ref, v_ref, g_ref, code_ref, gup_ref, gb_ref, gain_ref, lv_ref, tri_ref, y_ref, st_ref = refs
    lt = q_ref.shape[0]
    n_levels = lt.bit_length() - 1

    @pl.when(pl.program_id(1) == 0)
    def _():
        st_ref[...] = jnp.zeros_like(st_ref)

    tri = tri_ref[...]
    lv = lv_ref[...]
    for h in range(HEADS):
        cols = slice(h * HEAD_DIM, (h + 1) * HEAD_DIM)
        if mode == "hgrn":
            lbl = lbl_ref[:, cols]
            e = jnp.exp(lbl - jnp.max(lbl, axis=0, keepdims=True))
            p = e / jnp.sum(e, axis=0, keepdims=True)
            lb = jnp.zeros((1, p.shape[1]), F32)
            for l2 in range(1, layer + 1):
                lb = lb + p[l2:l2 + 1, :]
            f = lb + (1.0 - lb) * jax.nn.sigmoid(f_ref[:, cols])
            lg = jnp.log(f)
            kk = 1.0 - f
            q = q_ref[:, cols] * (HEAD_DIM ** -0.5)
        else:
            c_hi, c_lo = _split2(code_ref[...])
            u_hi, u_lo = _split2(gup_ref[:, cols])
            z = _dot(c_hi, u_hi) + _dot(c_hi, u_lo) + _dot(c_lo, u_hi) + gb_ref[:, cols]
            lg = (jnp.minimum(z, 0.0) - jnp.log1p(jnp.exp(-jnp.abs(z)))) * (1.0 / GLA_TAU)
            kk = k_ref[:, cols]
            q = q_ref[:, cols] * (GLA_DK ** -0.5)

        g_hi, g_mid, g_lo = _split3(lg)
        b = _dot(tri, g_hi) + _dot(tri, g_mid) + _dot(tri, g_lo)

        s = jnp.where(lv == -1, _dot_nt(q.astype(BF16), kk.astype(BF16)), 0.0)
        for l in range(n_levels):
            n = 1 << l
            bref = _group_row_bcast(b, 2 * n, n - 1)
            ql = (q * jnp.exp(jnp.minimum(b - bref, 0.0))).astype(BF16)
            kl = (kk * jnp.exp(jnp.minimum(bref - b, 0.0))).astype(BF16)
            s = jnp.where(lv == l, _dot_nt(ql, kl), s)

        vb = v_ref[:, cols].astype(BF16)
        st = st_ref[h]
        o = _dot(s.astype(BF16), vb) + _dot_nt((q * jnp.exp(b)).astype(BF16), st.astype(BF16))
        b_last = b[lt - 1:lt, :]
        kd = (kk * jnp.exp(b_last - b)).astype(BF16)
        st_ref[h] = st * jnp.exp(b_last) + _dot_tn(vb, kd)

        ms = jnp.mean(o * o, axis=-1, keepdims=True)
        g = g_ref[:, cols]
        y = o * lax.rsqrt(ms + EPS) * gain_ref[...] * (g * jax.nn.sigmoid(g))
        y_ref[:, cols] = y.astype(y_ref.dtype)


def _scan_consts(lt):
    idx = np.arange(lt)
    x = idx[:, None] ^ idx[None, :]
    lvl = np.floor(np.log2(np.maximum(x, 1))).astype(np.int32)
    lv = np.where(idx[None, :] < idx[:, None], lvl, np.where(x == 0, -1, -2)).astype(np.int32)
    tri = (idx[None, :] <= idx[:, None]).astype(np.float32)
    return jnp.asarray(lv), jnp.asarray(tri, dtype=BF16)


def _scan(mode, layer, z, extra, out_gain, bsz, seq):
    t = z.shape[0]
    lt = 256
    per_b = seq // lt
    lv, tri = _scan_consts(lt)

    def col(cb):
        return pl.BlockSpec((lt, BRANCH_WIDTH), lambda b, c: (b * per_b + c, cb))

    const2 = lambda b, c: (0, 0)
    in_specs = [col(0), col(1), col(2), col(3)]
    args = [z, z, z, z]
    if mode == "hgrn":
        (lb_logits,) = extra
        in_specs.append(pl.BlockSpec(lb_logits.shape, const2))
        args.append(lb_logits)
    else:
        gup, gb = extra
        in_specs += [pl.BlockSpec((lt, LANES), lambda b, c: (b * per_b + c, 4 * HEADS)),
                     pl.BlockSpec(gup.shape, const2), pl.BlockSpec(gb.shape, const2)]
        args += [z, gup, gb]
    in_specs += [pl.BlockSpec((1, HEAD_DIM), const2), pl.BlockSpec((lt, lt), const2), pl.BlockSpec((lt, lt), const2)]
    args += [out_gain, lv, tri]
    return pl.pallas_call(
        functools.partial(_scan_kernel, mode, layer),
        grid=(bsz, per_b),
        in_specs=in_specs,
        out_specs=pl.BlockSpec((lt, BRANCH_WIDTH), lambda b, c: (b * per_b + c, 0)),
        out_shape=jax.ShapeDtypeStruct((t, BRANCH_WIDTH), BF16),
        scratch_shapes=[pltpu.VMEM((HEADS, HEAD_DIM, HEAD_DIM), F32)],
        compiler_params=_params("parallel", "arbitrary"),
        name="scan_" + mode,
    )(*args)


_KCH = 256


def _dsa_kernel(topk, q_ref, k_ref, v_ref, ziq_ref, zik_ref, qg_ref, kg_ref, stri_ref, o_ref,
                kn_ref, vt_ref, kih_ref, kil_ref, keys_ref, bias_ref, acc_ref):
    j = pl.program_id(1)
    seq = k_ref.shape[0]

    @pl.when(j == 0)
    def _prep():
        def body(c, carry):
            r0 = pl.multiple_of(c * _KCH, _KCH)
            kc = k_ref[pl.ds(r0, _KCH), :].astype(F32)
            for h in range(HEADS):
                kh = kc[:, h * HEAD_DIM:(h + 1) * HEAD_DIM]
                ms = jnp.mean(kh * kh, axis=-1, keepdims=True)
                kn_ref[pl.ds(r0, _KCH), h * HEAD_DIM:(h + 1) * HEAD_DIM] = (
                    kh * lax.rsqrt(ms + EPS) * kg_ref[...]).astype(BF16)
            vt_ref[c] = v_ref[pl.ds(r0, _KCH), :].astype(F32).T.astype(BF16)
            ki = zik_ref[pl.ds(r0, _KCH), :]
            hi = ki.astype(BF16)
            kih_ref[pl.ds(r0, _KCH), :] = hi
            kil_ref[pl.ds(r0, _KCH), :] = (ki - hi.astype(F32)).astype(BF16)
            return carry
        lax.fori_loop(0, seq // _KCH, body, 0)

    n_ch = (j * QBLOCK + QBLOCK + _KCH - 1) // _KCH
    ziq = ziq_ref[...]
    w_t = ziq[:, 2 * LANES:3 * LANES].T
    qi = jnp.concatenate([ziq[:, h * IDX_DIM:(h + 1) * IDX_DIM] for h in range(IDX_HEADS)], axis=0)
    qi_hi, qi_lo = _split2(qi * (IDX_DIM ** -0.5))
    w_rows = [w_t[IDX_DIM + h:IDX_DIM + h + 1, :] * (IDX_HEADS ** -0.5) for h in range(IDX_HEADS)]
    q_pos = j * QBLOCK + lax.broadcasted_iota(I32, (_KCH, QBLOCK), 1)
    row_i = lax.broadcasted_iota(I32, (_KCH, QBLOCK), 0)

    def idx_body(c, carry):
        r0 = pl.multiple_of(c * _KCH, _KCH)
        k_hi = kih_ref[pl.ds(r0, _KCH), :][:, :IDX_DIM]
        k_lo = kil_ref[pl.ds(r0, _KCH), :][:, :IDX_DIM]
        logit = _dot_nt(k_hi, qi_hi) + _dot_nt(k_hi, qi_lo) + _dot_nt(k_lo, qi_hi)
        score = jnp.zeros((_KCH, QBLOCK), F32)
        for h in range(IDX_HEADS):
            score = score + w_rows[h] * jnp.maximum(logit[:, h * QBLOCK:(h + 1) * QBLOCK], 0.0)
        score = jnp.where(score == 0.0, 0.0, score)
        bits = lax.bitcast_convert_type(score, I32)
        key = jnp.where(bits < 0, bits ^ np.int32(0x7FFFFFFF), bits)
        keys_ref[pl.ds(r0, _KCH), :] = jnp.where(r0 + row_i <= q_pos, key, INT_MIN)
        return carry
    lax.fori_loop(0, n_ch, idx_body, 0)

    def count(pred):
        def body(c, acc):
            r0 = pl.multiple_of(c * _KCH, _KCH)
            m = jnp.where(pred(keys_ref[pl.ds(r0, _KCH), :]), 1, 0).astype(I32)
            return acc + jnp.sum(m.reshape(_KCH // SUBLANES, SUBLANES, QBLOCK), axis=0)
        acc = lax.fori_loop(0, n_ch, body, jnp.zeros((SUBLANES, QBLOCK), I32))
        return jnp.sum(acc, axis=0, keepdims=True)

    def bisect(it, thr):
        cand = thr ^ jnp.left_shift(jnp.int32(1), 31 - it)
        return jnp.where(count(lambda x: x >= cand) >= topk, cand, thr)
    thr = lax.fori_loop(0, 32, bisect, jnp.full((1, QBLOCK), INT_MIN, I32))
    need = (topk - count(lambda x: x > thr)).astype(F32)

    def sel_body(c, seen):
        r0 = pl.multiple_of(c * _KCH, _KCH)
        x = keys_ref[pl.ds(r0, _KCH), :]
        eq = jnp.where(x == thr, 1.0, 0.0)
        rank = seen + _dot(stri_ref[...], eq.astype(BF16))
        tie_ok = jnp.where(x == thr, jnp.where(rank < need, 0.0, NEG_BIG), NEG_BIG)
        bias = jnp.where(x > thr, 0.0, tie_ok)
        bias_ref[pl.ds(r0, _KCH), :] = jnp.where(x == INT_MIN, NEG_BIG, bias)
        return seen + jnp.sum(eq, axis=0, keepdims=True)
    lax.fori_loop(0, n_ch, sel_body, jnp.zeros((1, QBLOCK), F32))

    qf = q_ref[...].astype(F32)
    qn = []
    for h in range(HEADS):
        qh = qf[:, h * HEAD_DIM:(h + 1) * HEAD_DIM]
        ms = jnp.mean(qh * qh, axis=-1, keepdims=True)
        qn.append((qh * lax.rsqrt(ms + EPS) * qg_ref[...] * (HEAD_DIM ** -0.5)).astype(BF16))
    acc_ref[...] = jnp.zeros_like(acc_ref)

    @pl.when(n_ch % 2 == 1)
    def _():
        bias_ref[pl.ds(pl.multiple_of(n_ch * _KCH, _KCH), _KCH), :] = jnp.full((_KCH, QBLOCK), NEG_BIG, F32)

    def att_body(c, carry):
        r0 = pl.multiple_of(c * (2 * _KCH), 2 * _KCH)
        bias = bias_ref[pl.ds(r0, 2 * _KCH), :]
        out = []
        for h in range(HEADS):
            m, l = carry[h]
            hd = slice(h * HEAD_DIM, (h + 1) * HEAD_DIM)
            sc = _dot_nt(kn_ref[pl.ds(r0, 2 * _KCH), hd], qn[h]) + bias
            m_new = jnp.maximum(m, jnp.max(sc, axis=0, keepdims=True))
            p = jnp.exp(sc - m_new)
            alpha = jnp.exp(m - m_new)
            pb = p.astype(BF16)
            pv = _dot(vt_ref[2 * c, hd, :], pb[:_KCH]) + _dot(vt_ref[2 * c + 1, hd, :], pb[_KCH:])
            acc_ref[h] = alpha * acc_ref[h] + pv
            out.append((m_new, alpha * l + jnp.sum(p, axis=0, keepdims=True)))
        return tuple(out)
    init = tuple((jnp.full((1, QBLOCK), NEG_BIG, F32), jnp.zeros((1, QBLOCK), F32)) for _ in range(HEADS))
    stats = lax.fori_loop(0, (n_ch + 1) // 2, att_body, init)
    for h in range(HEADS):
        o_ref[:, h * HEAD_DIM:(h + 1) * HEAD_DIM] = (acc_ref[h] / stats[h][1]).T.astype(o_ref.dtype)


def _dsa(zb, zi, q_gain, k_gain, bsz, seq):
    t = zb.shape[0]
    topk = min(DSA_MAX_TOPK, seq // 4)
    nq = seq // QBLOCK
    assert seq % (2 * _KCH) == 0
    idx = np.arange(_KCH)
    stri = jnp.asarray((idx[None, :] < idx[:, None]).astype(np.float32), dtype=BF16)
    const2 = lambda b, j: (0, 0)
    return pl.pallas_call(
        functools.partial(_dsa_kernel, topk),
        grid=(bsz, nq),
        in_specs=[
            pl.BlockSpec((QBLOCK, BRANCH_WIDTH), lambda b, j: (b * nq + j, 0)),
            pl.BlockSpec((seq, BRANCH_WIDTH), lambda b, j: (b, 1)),
            pl.BlockSpec((seq, BRANCH_WIDTH), lambda b, j: (b, 2)),
            pl.BlockSpec((QBLOCK, W_I), lambda b, j: (b * nq + j, 0)),
            pl.BlockSpec((seq, LANES), lambda b, j: (b, 2)),
            pl.BlockSpec((1, HEAD_DIM), const2),
            pl.BlockSpec((1, HEAD_DIM), const2),
            pl.BlockSpec((_KCH, _KCH), const2),
        ],
        out_specs=pl.BlockSpec((QBLOCK, BRANCH_WIDTH), lambda b, j: (b * nq + j, 0)),
        out_shape=jax.ShapeDtypeStruct((t, BRANCH_WIDTH), BF16),
        scratch_shapes=[
            pltpu.VMEM((seq, BRANCH_WIDTH), BF16),
            pltpu.VMEM((seq // _KCH, BRANCH_WIDTH, _KCH), BF16),
            pltpu.VMEM((seq, LANES), BF16),
            pltpu.VMEM((seq, LANES), BF16),
            pltpu.VMEM((seq, QBLOCK), I32),
            pltpu.VMEM((seq, QBLOCK), F32),
            pltpu.VMEM((HEADS, HEAD_DIM, QBLOCK), F32),
        ],
        compiler_params=_params("parallel", "arbitrary"),
        name="dsa",
    )(zb, zb, zb, zi, zi, q_gain, k_gain, stri)


def _merge_kernel(ya_ref, yb_ref, yc_ref, zm_ref, x_ref, gm_ref, wb_ref, wo_ref, o_ref):
    d = x_ref.shape[1]
    mixed = None
    for g, y_ref in enumerate((ya_ref, yb_ref, yc_ref)):
        up = _dot(y_ref[...], wb_ref[g])
        term = jax.nn.sigmoid(zm_ref[:, g * d:(g + 1) * d].astype(F32)) * up
        mixed = term if mixed is None else mixed + term
    o_ref[...] = x_ref[...] + gm_ref[0] * _dot(mixed.astype(BF16), wo_ref[...])


def _merge(ya, yb, yc, zm, x2, gate_m, w_branch, w_out, seq):
    t, d = x2.shape
    tm = 256
    per_b = seq // tm
    row = lambda i: (i, 0)
    return pl.pallas_call(
        _merge_kernel,
        grid=(t // tm,),
        in_specs=[
            pl.BlockSpec((tm, BRANCH_WIDTH), row), pl.BlockSpec((tm, BRANCH_WIDTH), row),
            pl.BlockSpec((tm, BRANCH_WIDTH), row), pl.BlockSpec((tm, W_M), row), pl.BlockSpec((tm, d), row),
            pl.BlockSpec((1, 1, d), lambda i: (i // per_b, 0, 0)),
            pl.BlockSpec(w_branch.shape, lambda i: (0, 0, 0)),
            pl.BlockSpec(w_out.shape, lambda i: (0, 0)),
        ],
        out_specs=pl.BlockSpec((tm, d), row),
        out_shape=jax.ShapeDtypeStruct((t, d), F32),
        compiler_params=_params("parallel"),
        name="merge",
    )(ya, yb, yc, zm, x2, gate_m, w_branch, w_out)


_CAND_ROWS = 72


def _cand_layout():
    flat = np.zeros((_CAND_ROWS,), np.int32)
    live = np.zeros((_CAND_ROWS,), bool)
    r = 0
    for b in range(16):
        flat[r], live[r] = b, True
        r += 1
    for a in (1, 2, 3):
        for b in range(8):
            flat[r], live[r] = a * 16 + b, True
            r += 1
    for a in range(16):
        flat[r], live[r] = a * 16, a >= 4
        r += 1
    for a in range(8):
        flat[r], live[r] = a * 16 + 1, a >= 4
        r += 1
    for a in range(8):
        flat[r], live[r] = a * 16 + 2, a == 4
        r += 1
    flat = np.where(live, flat, 1000 + np.arange(_CAND_ROWS))
    return flat.astype(np.int32), live


def _candidates(v1, v2, live):
    cand = jnp.concatenate(
        [v1[0:1, :] + v2] + [v1[a:a + 1, :] + v2[0:8, :] for a in (1, 2, 3)]
        + [v1 + v2[0:1, :], v1[0:8, :] + v2[1:2, :], v1[0:8, :] + v2[2:3, :]], axis=0)
    return jnp.where(live, cand, -jnp.inf)


def _gone(work):
    return jnp.sum(jnp.where(work == -jnp.inf, 1.0, 0.0), axis=0, keepdims=True)


def _route_head_exact(s1, s2, row_i, flat, live, v1_ref, v2_ref, b2_ref, th_ref, x2_ref):
    ranks = []
    for s, vals_ref in ((s1, v1_ref), (s2, v2_ref)):
        work = s
        rank = jnp.full(s.shape, 127.0, F32)
        for a in range(PEER_TOPK):
            m = jnp.max(work, axis=0, keepdims=True)
            idx = jnp.min(jnp.where(work == m, row_i, N_KEYS), axis=0, keepdims=True)
            hit = row_i == idx
            work = jnp.where(hit, -jnp.inf, work)
            rank = jnp.where(hit, float(a), rank)
            vals_ref[a:a + 1, :] = m
        ranks.append(rank)
    r1, r2 = ranks
    v1 = v1_ref[...]
    v2 = v2_ref[...]
    cand = _candidates(v1, v2, live)
    sel = jnp.zeros(cand.shape, F32)
    work = cand
    for _ in range(PEER_TOPK):
        m = jnp.max(work, axis=0, keepdims=True)
        fi = jnp.min(jnp.where(work == m, flat, 9999), axis=0, keepdims=True)
        hit = flat == fi
        work = jnp.where(hit, -jnp.inf, work)
        sel = jnp.where(hit, 1.0, sel)
    z = jnp.sum(sel * jnp.exp(cand - (v1[0:1, :] + v2[0:1, :])), axis=0, keepdims=True)
    b2_ref[...] = jnp.exp(s2 - v2[0:1, :]) * (0.5 / z)

    zeros8 = jnp.zeros((8, s1.shape[1]), F32)
    cnt_hi = (sel[40:56, :] + jnp.concatenate([sel[56:64, :], zeros8], axis=0)
              + jnp.concatenate([sel[64:72, :], zeros8], axis=0))
    cnt_lo = [jnp.sum(sel[0:16, :], axis=0, keepdims=True)] + [
        jnp.sum(sel[16 + 8 * (a - 1):24 + 8 * (a - 1), :], axis=0, keepdims=True) for a in (1, 2, 3)]
    cnt = jnp.zeros(s1.shape, F32)
    for a in range(PEER_TOPK):
        ca = cnt_lo[a] if a < 4 else cnt_hi[a:a + 1, :]
        cnt = jnp.where(r1 == float(a), ca, cnt)
    th_ref[...] = 0.5 - cnt
    x2_ref[...] = -r2


def _route_kernel(x_ref, gain_ref, shift_ref, scale_ref, wq_ref, sk_ref, flat_ref, live_ref,
                  ht_ref, a1_ref, th_ref, b2_ref, x2_ref, qt_ref, v1_ref, v2_ref):
    tt = x_ref.shape[0]
    h2 = _modulate(x_ref[...], gain_ref[...], shift_ref[0], scale_ref[0])
    ht = h2.T.astype(BF16)
    for g in range(tt // _TG):
        ht_ref[g] = ht[:, g * _TG:(g + 1) * _TG]
    qt_ref[...] = _dot(wq_ref[...], ht)
    row_i = lax.broadcasted_iota(I32, (N_KEYS, tt), 0)
    flat = flat_ref[...]
    live = live_ref[...] > 0.0
    n_dead = float(_CAND_ROWS - int(_cand_layout()[1].sum()))

    for h in range(PEER_HEADS):
        s1 = _dot(sk_ref[2 * h], qt_ref[(2 * h) * N_KEYS:(2 * h + 1) * N_KEYS, :].astype(BF16))
        s2 = _dot(sk_ref[2 * h + 1], qt_ref[(2 * h + 1) * N_KEYS:(2 * h + 2) * N_KEYS, :].astype(BF16))

        w1, w2 = s1, s2
        for a in range(PEER_TOPK):
            m1 = jnp.max(w1, axis=0, keepdims=True)
            m2 = jnp.max(w2, axis=0, keepdims=True)
            w1 = jnp.where(w1 == m1, -jnp.inf, w1)
            w2 = jnp.where(w2 == m2, -jnp.inf, w2)
            v1_ref[a:a + 1, :] = m1
            v2_ref[a:a + 1, :] = m2
        v1 = v1_ref[...]
        v2 = v2_ref[...]
        cand = _candidates(v1, v2, live)
        wc = cand
        for _ in range(PEER_TOPK):
            t16 = jnp.max(wc, axis=0, keepdims=True)
            wc = jnp.where(wc == t16, -jnp.inf, wc)
        t17 = jnp.max(wc, axis=0, keepdims=True)
        z = jnp.sum(jnp.where(cand >= t16, jnp.exp(cand - (v1[0:1, :] + v2[0:1, :])), 0.0), axis=0, keepdims=True)
        a1_ref[h] = jnp.exp(s1 - v1[0:1, :])
        b2_ref[h] = jnp.exp(s2 - v2[0:1, :]) * (0.5 / z)
        th_ref[h] = jnp.where(s1 >= v1[PEER_TOPK - 1:PEER_TOPK, :], 0.5 * (t16 + t17) - s1, jnp.inf)
        x2_ref[h] = jnp.where(s2 >= v2[PEER_TOPK - 1:PEER_TOPK, :], s2, -jnp.inf)

        scale = (jnp.abs(v1[0:1, :]) + jnp.abs(v1[PEER_TOPK - 1:PEER_TOPK, :])
                 + jnp.abs(v2[0:1, :]) + jnp.abs(v2[PEER_TOPK - 1:PEER_TOPK, :]))
        close = jnp.where(t16 - t17 > scale * 2.0 ** -20, 0.0, 1.0)
        bad = (jnp.abs(_gone(w1) - PEER_TOPK) + jnp.abs(_gone(w2) - PEER_TOPK)
               + jnp.abs(_gone(wc) - (PEER_TOPK + n_dead)) + close)

        @pl.when(jnp.max(bad) > 0.0)
        def _():
            _route_head_exact(s1, s2, row_i, flat, live, v1_ref, v2_ref, b2_ref.at[h], th_ref.at[h], x2_ref.at[h])


def _route(x2, gain, shift, scale, wq_t, sub_keys, seq):
    t, d = x2.shape
    tt = 256
    per_b = seq // tt
    flat_np, live_np = _cand_layout()
    flat = jnp.asarray(np.broadcast_to(flat_np[:, None], (_CAND_ROWS, tt)).copy())
    live = jnp.asarray(np.broadcast_to(live_np[:, None], (_CAND_ROWS, tt)).astype(np.float32))
    dense = jax.ShapeDtypeStruct((PEER_HEADS, N_KEYS, t), F32)
    dense_spec = pl.BlockSpec((PEER_HEADS, N_KEYS, tt), lambda i: (0, 0, i))
    const2 = lambda i: (0, 0)
    return pl.pallas_call(
        _route_kernel,
        grid=(t // tt,),
        in_specs=[
            pl.BlockSpec((tt, d), lambda i: (i, 0)),
            pl.BlockSpec((1, d), const2),
            pl.BlockSpec((1, 1, d), lambda i: (i // per_b, 0, 0)),
            pl.BlockSpec((1, 1, d), lambda i: (i // per_b, 0, 0)),
            pl.BlockSpec(wq_t.shape, const2),
            pl.BlockSpec(sub_keys.shape, lambda i: (0, 0, 0)),
            pl.BlockSpec((_CAND_ROWS, tt), const2),
            pl.BlockSpec((_CAND_ROWS, tt), const2),
        ],
        out_specs=[pl.BlockSpec((tt // _TG, d, _TG), lambda i: (i, 0, 0)),
                   dense_spec, dense_spec, dense_spec, dense_spec],
        out_shape=[jax.ShapeDtypeStruct((t // _TG, d, _TG), BF16), dense, dense, dense, dense],
        scratch_shapes=[
            pltpu.VMEM((PEER_HEADS * 2 * N_KEYS, tt), F32),
            pltpu.VMEM((PEER_TOPK, tt), F32),
            pltpu.VMEM((PEER_TOPK, tt), F32),
        ],
        compiler_params=_params("parallel"),
        name="peer_route",
    )(x2, gain, shift, scale, wq_t, sub_keys, flat, live)


_JB = 4
_IB = 4
_TG = 256
_RB = _IB * N_KEYS


def _peer_kernel(n_tiles, ht_ref, u_ref, vt_ref, a1_ref, th_ref, b2_ref, x2_ref, x_ref, gf_ref, o_ref,
                 acc_ref, at_ref, gt_ref, bc_ref):
    s = pl.program_id(1)
    n_g, te, tg = at_ref.shape
    tt = n_g * tg
    n_i = te // N_KEYS
    n_jb = N_KEYS // (SUBLANES * _JB)
    assert n_i == SUBLANES
    assert _RB == _IB * N_KEYS

    def bcast_rows():
        i0 = pl.multiple_of(s * n_i, SUBLANES)
        for h in range(PEER_HEADS):
            a1g = a1_ref[h, pl.ds(i0, n_i), :]
            thg = th_ref[h, pl.ds(i0, n_i), :]
            for ii in range(n_i):
                bc_ref[2 * (h * n_i + ii)] = jnp.broadcast_to(a1g[ii:ii + 1, :], (SUBLANES, tt))
                bc_ref[2 * (h * n_i + ii) + 1] = jnp.broadcast_to(thg[ii:ii + 1, :], (SUBLANES, tt))

    def mask_block(tb, jb, ib):
        g, lane0 = divmod(tb * LANES, tg)
        lanes = slice(tb * LANES, (tb + 1) * LANES)
        lanes_g = slice(lane0, lane0 + LANES)
        j0 = jb * (SUBLANES * _JB)
        w = [[None] * _JB for _ in range(_IB)]
        for h in range(PEER_HEADS):
            x2 = [x2_ref[h, j0 + SUBLANES * k:j0 + SUBLANES * (k + 1), lanes] for k in range(_JB)]
            b2 = [b2_ref[h, j0 + SUBLANES * k:j0 + SUBLANES * (k + 1), lanes] for k in range(_JB)]
            for di in range(_IB):
                row = 2 * (h * n_i + ib * _IB + di)
                a1v = bc_ref[row, :, lanes]
                thv = bc_ref[row + 1, :, lanes]
                for k in range(_JB):
                    term = jnp.where(x2[k] >= thv, a1v * b2[k], 0.0)
                    w[di][k] = term if w[di][k] is None else w[di][k] + term
        for di in range(_IB):
            r0 = (ib * _IB + di) * N_KEYS + j0
            rows = slice(r0, r0 + SUBLANES * _JB)
            a = at_ref[g, rows, lanes_g]
            act2 = a + a * lax.erf(a * np.float32(np.sqrt(0.5)))
            gt_ref[g, rows, lanes_g] = (jnp.concatenate(w[di], axis=0) * act2).astype(BF16)

    @pl.when(s == 0)
    def _():
        acc_ref[...] = jnp.zeros_like(acc_ref)

    def pre_act(k):
        rows = slice(k * _RB, (k + 1) * _RB)
        for g in range(n_g):
            at_ref[g, rows, :] = _dot(u_ref[rows, :], ht_ref[g])

    def fold(k):
        rows = slice(k * _RB, (k + 1) * _RB)
        for g in range(n_g):
            acc_ref[g] += _dot(vt_ref[:, rows], gt_ref[g, rows, :])

    bcast_rows()
    n_slices = te // _RB
    pre_act(0)
    for k in range(n_slices):
        if k + 1 < n_slices:
            pre_act(k + 1)
        if k >= 1:
            fold(k - 1)
        for tb in range(tt // LANES):
            for jb in range(n_jb):
                mask_block(tb, jb, k)
    fold(n_slices - 1)

    @pl.when(s == n_tiles - 1)
    def _():
        for g in range(n_g):
            rows = slice(g * tg, (g + 1) * tg)
            o_ref[rows, :] = x_ref[rows, :] + gf_ref[0] * acc_ref[g].T


def _peer(ht, u, vt, a1, th, b2, xk, x2, gate_f, seq):
    t, d = x2.shape
    n_tiles = u.shape[0] // PEER_TE
    tt = 512
    per_b = seq // tt
    f32_spec = pl.BlockSpec((PEER_HEADS, N_KEYS, tt), lambda i, s: (0, 0, i))
    return pl.pallas_call(
        functools.partial(_peer_kernel, n_tiles),
        grid=(t // tt, n_tiles),
        in_specs=[
            pl.BlockSpec((tt // _TG, d, _TG), lambda i, s: (i, 0, 0)),
            pl.BlockSpec((PEER_TE, d), lambda i, s: (s, 0)),
            pl.BlockSpec((d, PEER_TE), lambda i, s: (0, s)),
            f32_spec, f32_spec, f32_spec, f32_spec,
            pl.BlockSpec((tt, d), lambda i, s: (i, 0)),
            pl.BlockSpec((1, 1, d), lambda i, s: (i // per_b, 0, 0)),
        ],
        out_specs=pl.BlockSpec((tt, d), lambda i, s: (i, 0)),
        out_shape=jax.ShapeDtypeStruct((t, d), F32),
        scratch_shapes=[
            pltpu.VMEM((tt // _TG, d, _TG), F32),
            pltpu.VMEM((tt // _TG, PEER_TE, _TG), F32),
            pltpu.VMEM((tt // _TG, PEER_TE, _TG), BF16),
            pltpu.VMEM((2 * PEER_HEADS * PEER_TE // N_KEYS, SUBLANES, tt), F32),
        ],
        compiler_params=_params("parallel", "arbitrary"),
        name="peer_experts",
    )(ht, u, vt, a1, th, b2, xk, x2, gate_f)


def _tcast_kernel(x_ref, o_ref):
    o_ref[...] = x_ref[...].T.astype(o_ref.dtype)


def _cast_kernel(x_ref, o_ref):
    o_ref[...] = x_ref[...].astype(o_ref.dtype)


def _layer_cast(x, layer, dtype, transpose):
    _, r, c = x.shape
    tr, tc = min(r, 1024), min(c, 1024)
    return pl.pallas_call(
        _tcast_kernel if transpose else _cast_kernel,
        grid=(r // tr, c // tc),
        in_specs=[pl.BlockSpec((None, tr, tc), lambda i, j: (layer, i, j))],
        out_specs=pl.BlockSpec((tc, tr), lambda i, j: (j, i)) if transpose else pl.BlockSpec((tr, tc), lambda i, j: (i, j)),
        out_shape=jax.ShapeDtypeStruct((c, r) if transpose else (r, c), dtype),
        compiler_params=_params("parallel", "parallel"),
        name="layer_cast",
    )(x)


def _pad_heads(w, width):
    d = w.shape[0]
    w = w.reshape(d, HEADS, width)
    return jnp.pad(w, ((0, 0), (0, 0), (0, HEAD_DIM - width))).reshape(d, HEADS * HEAD_DIM)


def _pack_segments():
    segs = [(0, W_A + W_B, 0)]
    n_idx = IDX_HEADS * IDX_DIM + IDX_DIM + IDX_HEADS
    src = W_A + W_B
    dst = W_A + W_B
    segs.append((src, n_idx, dst)); src += n_idx; dst += W_I
    for _ in range(2):
        for h in range(HEADS):
            segs.append((src + GLA_DK * h, GLA_DK, dst + HEAD_DIM * h))
        src += HEADS * GLA_DK; dst += BRANCH_WIDTH
    segs.append((src, BRANCH_WIDTH, dst)); src += BRANCH_WIDTH; dst += BRANCH_WIDTH
    code_src = src; src += GLA_GATE_RANK
    segs.append((src, BRANCH_WIDTH, dst)); src += BRANCH_WIDTH; dst += BRANCH_WIDTH
    segs.append((code_src, GLA_GATE_RANK, dst)); dst += LANES
    segs.append((src, W_M, dst))
    assert dst + W_M == W_PACK
    return segs


def _pack_kernel(w_ref, o_ref):
    o_ref[...] = jnp.zeros_like(o_ref)
    for src, width, dst in _pack_segments():
        o_ref[0, :, dst:dst + width] = w_ref[0, :, src:src + width].astype(o_ref.dtype)


def _pack_w_in(w_in):
    depth, d, n_in = w_in.shape
    tr = 128
    return pl.pallas_call(
        _pack_kernel,
        grid=(depth, d // tr),
        in_specs=[pl.BlockSpec((1, tr, n_in), lambda l, i: (l, i, 0))],
        out_specs=pl.BlockSpec((1, tr, W_PACK), lambda l, i: (l, i, 0)),
        out_shape=jax.ShapeDtypeStruct((depth, d, W_PACK), BF16),
        compiler_params=_params("parallel", "parallel"),
        name="pack_w_in",
    )(w_in)


def kernel(x, c, w_ada, b_ada, norm_mix, norm_ffn, w_in, hgrn_lb_logits, hgrn_out_norm, dsa_q_norm, dsa_k_norm,
           gla_gate_up, gla_gate_bias, gla_out_norm, w_branch, w_out, peer_w_query, peer_sub_keys, peer_u, peer_v):
    bsz, seq, d = x.shape
    depth = w_in.shape[0]
    t = bsz * seq
    x2 = x.reshape(t, d)
    mod = _ada(c, w_ada, b_ada)
    w_pack = _pack_w_in(w_in)

    for l in range(depth):
        shift_m, scale_m, gate_m, shift_f, scale_f, gate_f = [
            mod[l, :, k * d:(k + 1) * d].reshape(bsz, 1, d) for k in range(6)]
        za, zb, zi, zc, zm = _inproj(x2, norm_mix[l].reshape(1, d), shift_m, scale_m, w_pack, l, seq)

        ya = _scan("hgrn", l, za, (hgrn_lb_logits,), hgrn_out_norm[l].reshape(1, HEAD_DIM), bsz, seq)
        yb = _dsa(zb, zi, dsa_q_norm[l].reshape(1, HEAD_DIM), dsa_k_norm[l].reshape(1, HEAD_DIM), bsz, seq)
        gup = jnp.pad(_pad_heads(gla_gate_up[l], GLA_DK), ((0, LANES - GLA_GATE_RANK), (0, 0)))
        gb = _pad_heads(gla_gate_bias[l].reshape(1, HEADS * GLA_DK), GLA_DK)
        yc = _scan("gla", l, zc, (gup, gb), gla_out_norm[l].reshape(1, HEAD_DIM), bsz, seq)

        x2 = _merge(ya, yb, yc, zm, x2, gate_m, w_branch[l].astype(BF16), w_out[l].astype(BF16), seq)

        ht, a1, th, b2, xk = _route(
            x2, norm_ffn[l].reshape(1, d), shift_f, scale_f, _layer_cast(peer_w_query, l, BF16, True),
            peer_sub_keys[l].reshape(PEER_HEADS * 2, N_KEYS, -1).astype(BF16), seq)
        x2 = _peer(ht, _layer_cast(peer_u, l, BF16, False), _layer_cast(peer_v, l, BF16, True),
                   a1, th, b2, xk, x2, gate_f, seq)

    return x2.reshape(bsz, seq, d)
```

```python
import functools

import numpy as np
import jax
import jax.numpy as jnp
from jax import lax
from jax.experimental import pallas as pl
from jax.experimental.pallas import tpu as pltpu

F32, BF16, I32 = jnp.float32, jnp.bfloat16, jnp.int32

D_MODEL = 1024
HEADS = 4
HEAD_DIM = 128
BRANCH_WIDTH = HEADS * HEAD_DIM
IDX_HEADS = 4
IDX_DIM = 64
DSA_MAX_TOPK = 256
QBLOCK = 256
GLA_DK = 64
GLA_GATE_RANK = 16
GLA_TAU = 16.0
PEER_HEADS = 8
N_KEYS = 128
PEER_TOPK = 16
PEER_TE = 1024
EPS = 1e-6

LANES = 128
SUBLANES = 8
VMEM_LIMIT_BYTES = 56 * 1024 * 1024

NEG_BIG = -1e30
INT_MIN = np.int32(-2 ** 31)

W_A = 4 * BRANCH_WIDTH
W_B = 3 * BRANCH_WIDTH
W_I = 3 * LANES
W_C = 4 * BRANCH_WIDTH + LANES
W_M = 3 * D_MODEL
W_PACK = W_A + W_B + W_I + W_C + W_M


def _dot(a, b):
    return jnp.dot(a, b, preferred_element_type=F32)


def _dot_nt(a, b):
    return lax.dot_general(a, b, (((1,), (1,)), ((), ())), preferred_element_type=F32)


def _dot_tn(a, b):
    return lax.dot_general(a, b, (((0,), (0,)), ((), ())), preferred_element_type=F32)


def _split2(x):
    hi = x.astype(BF16)
    lo = (x - hi.astype(F32)).astype(BF16)
    return hi, lo


def _split3(x):
    hi = x.astype(BF16)
    r = x - hi.astype(F32)
    mid = r.astype(BF16)
    lo = (r - mid.astype(F32)).astype(BF16)
    return hi, mid, lo


def _params(*sem):
    return pltpu.CompilerParams(dimension_semantics=sem, vmem_limit_bytes=VMEM_LIMIT_BYTES)


def _modulate(x, gain, shift, scale):
    ms = jnp.mean(x * x, axis=-1, keepdims=True)
    return x * lax.rsqrt(ms + EPS) * gain * (1.0 + scale) + shift


def _ada_kernel(c_ref, w_ref, b_ref, o_ref):
    c = c_ref[...]
    sc = c * jax.nn.sigmoid(c)
    a_hi, a_lo = _split2(sc)
    w_hi, w_lo = _split2(w_ref[0])
    o_ref[0] = _dot(a_hi, w_hi) + _dot(a_hi, w_lo) + _dot(a_lo, w_hi) + b_ref[0]


def _ada(c, w_ada, b_ada):
    depth, d, n = w_ada.shape
    bsz = c.shape[0]
    tn = 1536
    return pl.pallas_call(
        _ada_kernel,
        grid=(depth, n // tn),
        in_specs=[
            pl.BlockSpec((bsz, d), lambda l, j: (0, 0)),
            pl.BlockSpec((1, d, tn), lambda l, j: (l, 0, j)),
            pl.BlockSpec((1, 1, tn), lambda l, j: (l, 0, j)),
        ],
        out_specs=pl.BlockSpec((1, bsz, tn), lambda l, j: (l, 0, j)),
        out_shape=jax.ShapeDtypeStruct((depth, bsz, n), F32),
        compiler_params=_params("parallel", "parallel"),
        name="ada",
    )(c, w_ada, b_ada.reshape(depth, 1, n))


def _inproj_kernel(x_ref, gain_ref, shift_ref, scale_ref, w_ref, *out_refs):
    h = _modulate(x_ref[...], gain_ref[...], shift_ref[0], scale_ref[0]).astype(BF16)
    off = 0
    for o_ref in out_refs:
        width = o_ref.shape[1]
        for c0 in range(0, width, 512):
            c1 = min(c0 + 512, width)
            o_ref[:, c0:c1] = _dot(h, w_ref[:, off + c0:off + c1]).astype(o_ref.dtype)
        off += width


def _inproj(x2, gain, shift, scale, w_pack, layer, seq):
    t, d = x2.shape
    tm = 256
    per_b = seq // tm
    widths = (W_A, W_B, W_I, W_C, W_M)
    dtypes = (F32, BF16, F32, F32, BF16)
    return pl.pallas_call(
        _inproj_kernel,
        grid=(t // tm,),
        in_specs=[
            pl.BlockSpec((tm, d), lambda i: (i, 0)),
            pl.BlockSpec((1, d), lambda i: (0, 0)),
            pl.BlockSpec((1, 1, d), lambda i: (i // per_b, 0, 0)),
            pl.BlockSpec((1, 1, d), lambda i: (i // per_b, 0, 0)),
            pl.BlockSpec((None, d, W_PACK), lambda i: (layer, 0, 0), pipeline_mode=pl.Buffered(1)),
        ],
        out_specs=[pl.BlockSpec((tm, w), lambda i: (i, 0)) for w in widths],
        out_shape=[jax.ShapeDtypeStruct((t, w), dt) for w, dt in zip(widths, dtypes)],
        compiler_params=_params("parallel"),
        name="inproj",
    )(x2, gain, shift, scale, w_pack)


def _group_row_bcast(x, group, row):
    n, w = x.shape
    if group >= SUBLANES:
        x3 = x.reshape(n // group, group, w)
        return jnp.broadcast_to(x3[:, row:row + 1, :], x3.shape).reshape(n, w)
    x3 = x.reshape(n // SUBLANES, SUBLANES, w)
    sub = lax.broadcasted_iota(I32, x3.shape, 1)
    n_groups = SUBLANES // group
    res = None
    for g in reversed(range(n_groups)):
        r = g * group + row
        bc = jnp.broadcast_to(x3[:, r:r + 1, :], x3.shape)
        res = bc if res is None else jnp.where(sub < (g + 1) * group, bc, res)
    return res.reshape(n, w)


def _scan_kernel(mode, layer, *refs):
    if mode == "hgrn":
        q_ref, f_ref, v_ref, g_ref, lbl_ref, gain_ref, lv_ref, tri_ref, y_ref, st_ref = refs
    else:
        q_ref, k_ref, v_ref, g_ref, code_ref, gup_ref, gb_ref, gain_ref, lv_ref, tri_ref, y_ref, st_ref = refs
    lt = q_ref.shape[0]
    n_levels = lt.bit_length() - 1

    @pl.when(pl.program_id(1) == 0)
    def _():
        st_ref[...] = jnp.zeros_like(st_ref)

    tri = tri_ref[...]
    lv = lv_ref[...]
    for h in range(HEADS):
        cols = slice(h * HEAD_DIM, (h + 1) * HEAD_DIM)
        if mode == "hgrn":
            lbl = lbl_ref[:, cols]
            e = jnp.exp(lbl - jnp.max(lbl, axis=0, keepdims=True))
            p = e / jnp.sum(e, axis=0, keepdims=True)
            lb = jnp.zeros((1, p.shape[1]), F32)
            for l2 in range(1, layer + 1):
                lb = lb + p[l2:l2 + 1, :]
            f = lb + (1.0 - lb) * jax.nn.sigmoid(f_ref[:, cols])
            lg = jnp.log(f)
            kk = 1.0 - f
            q = q_ref[:, cols] * (HEAD_DIM ** -0.5)
        else:
            c_hi, c_lo = _split2(code_ref[...])
            u_hi, u_lo = _split2(gup_ref[:, cols])
            z = _dot(c_hi, u_hi) + _dot(c_hi, u_lo) + _dot(c_lo, u_hi) + gb_ref[:, cols]
            lg = (jnp.minimum(z, 0.0) - jnp.log1p(jnp.exp(-jnp.abs(z)))) * (1.0 / GLA_TAU)
            kk = k_ref[:, cols]
            q = q_ref[:, cols] * (GLA_DK ** -0.5)

        g_hi, g_mid, g_lo = _split3(lg)
        b = _dot(tri, g_hi) + _dot(tri, g_mid) + _dot(tri, g_lo)

        s = jnp.where(lv == -1, _dot_nt(q.astype(BF16), kk.astype(BF16)), 0.0)
        for l in range(n_levels):
            n = 1 << l
            bref = _group_row_bcast(b, 2 * n, n - 1)
            ql = (q * jnp.exp(jnp.minimum(b - bref, 0.0))).astype(BF16)
            kl = (kk * jnp.exp(jnp.minimum(bref - b, 0.0))).astype(BF16)
            s = jnp.where(lv == l, _dot_nt(ql, kl), s)

        vb = v_ref[:, cols].astype(BF16)
        st = st_ref[h]
        o = _dot(s.astype(BF16), vb) + _dot_nt((q * jnp.exp(b)).astype(BF16), st.astype(BF16))
        b_last = b[lt - 1:lt, :]
        kd = (kk * jnp.exp(b_last - b)).astype(BF16)
        st_ref[h] = st * jnp.exp(b_last) + _dot_tn(vb, kd)

        ms = jnp.mean(o * o, axis=-1, keepdims=True)
        g = g_ref[:, cols]
        y = o * lax.rsqrt(ms + EPS) * gain_ref[...] * (g * jax.nn.sigmoid(g))
        y_ref[:, cols] = y.astype(y_ref.dtype)


def _scan_consts(lt):
    idx = np.arange(lt)
    x = idx[:, None] ^ idx[None, :]
    lvl = np.floor(np.log2(np.maximum(x, 1))).astype(np.int32)
    lv = np.where(idx[None, :] < idx[:, None], lvl, np.where(x == 0, -1, -2)).astype(np.int32)
    tri = (idx[None, :] <= idx[:, None]).astype(np.float32)
    return jnp.asarray(lv), jnp.asarray(tri, dtype=BF16)


def _scan(mode, layer, z, extra, out_gain, bsz, seq):
    t = z.shape[0]
    lt = 256
    per_b = seq // lt
    lv, tri = _scan_consts(lt)

    def col(cb):
        return pl.BlockSpec((lt, BRANCH_WIDTH), lambda b, c: (b * per_b + c, cb))

    const2 = lambda b, c: (0, 0)
    in_specs = [col(0), col(1), col(2), col(3)]
    args = [z, z, z, z]
    if mode == "hgrn":
        (lb_logits,) = extra
        in_specs.append(pl.BlockSpec(lb_logits.shape, const2))
        args.append(lb_logits)
    else:
        gup, gb = extra
        in_specs += [pl.BlockSpec((lt, LANES), lambda b, c: (b * per_b + c, 4 * HEADS)),
                     pl.BlockSpec(gup.shape, const2), pl.BlockSpec(gb.shape, const2)]
        args += [z, gup, gb]
    in_specs += [pl.BlockSpec((1, HEAD_DIM), const2), pl.BlockSpec((lt, lt), const2), pl.BlockSpec((lt, lt), const2)]
    args += [out_gain, lv, tri]
    return pl.pallas_call(
        functools.partial(_scan_kernel, mode, layer),
        grid=(bsz, per_b),
        in_specs=in_specs,
        out_specs=pl.BlockSpec((lt, BRANCH_WIDTH), lambda b, c: (b * per_b + c, 0)),
        out_shape=jax.ShapeDtypeStruct((t, BRANCH_WIDTH), BF16),
        scratch_shapes=[pltpu.VMEM((HEADS, HEAD_DIM, HEAD_DIM), F32)],
        compiler_params=_params("parallel", "arbitrary"),
        name="scan_" + mode,
    )(*args)


_KCH = 256


def _dsa_kernel(topk, q_ref, k_ref, v_ref, ziq_ref, zik_ref, qg_ref, kg_ref, stri_ref, o_ref,
                kn_ref, vt_ref, kih_ref, kil_ref, keys_ref, bias_ref, acc_ref):
    j = pl.program_id(1)
    seq = k_ref.shape[0]

    @pl.when(j == 0)
    def _prep():
        def body(c, carry):
            r0 = pl.multiple_of(c * _KCH, _KCH)
            kc = k_ref[pl.ds(r0, _KCH), :].astype(F32)
            for h in range(HEADS):
                kh = kc[:, h * HEAD_DIM:(h + 1) * HEAD_DIM]
                ms = jnp.mean(kh * kh, axis=-1, keepdims=True)
                kn_ref[pl.ds(r0, _KCH), h * HEAD_DIM:(h + 1) * HEAD_DIM] = (
                    kh * lax.rsqrt(ms + EPS) * kg_ref[...]).astype(BF16)
            vt_ref[c] = v_ref[pl.ds(r0, _KCH), :].astype(F32).T.astype(BF16)
            ki = zik_ref[pl.ds(r0, _KCH), :]
            hi = ki.astype(BF16)
            kih_ref[pl.ds(r0, _KCH), :] = hi
            kil_ref[pl.ds(r0, _KCH), :] = (ki - hi.astype(F32)).astype(BF16)
            return carry
        lax.fori_loop(0, seq // _KCH, body, 0)

    n_ch = (j * QBLOCK + QBLOCK + _KCH - 1) // _KCH
    ziq = ziq_ref[...]
    w_t = ziq[:, 2 * LANES:3 * LANES].T
    qi = jnp.concatenate([ziq[:, h * IDX_DIM:(h + 1) * IDX_DIM] for h in range(IDX_HEADS)], axis=0)
    qi_hi, qi_lo = _split2(qi * (IDX_DIM ** -0.5))
    w_rows = [w_t[IDX_DIM + h:IDX_DIM + h + 1, :] * (IDX_HEADS ** -0.5) for h in range(IDX_HEADS)]
    q_pos = j * QBLOCK + lax.broadcasted_iota(I32, (_KCH, QBLOCK), 1)
    row_i = lax.broadcasted_iota(I32, (_KCH, QBLOCK), 0)

    def idx_body(c, carry):
        r0 = pl.multiple_of(c * _KCH, _KCH)
        k_hi = kih_ref[pl.ds(r0, _KCH), :][:, :IDX_DIM]
        k_lo = kil_ref[pl.ds(r0, _KCH), :][:, :IDX_DIM]
        logit = _dot_nt(k_hi, qi_hi) + _dot_nt(k_hi, qi_lo) + _dot_nt(k_lo, qi_hi)
        score = jnp.zeros((_KCH, QBLOCK), F32)
        for h in range(IDX_HEADS):
            score = score + w_rows[h] * jnp.maximum(logit[:, h * QBLOCK:(h + 1) * QBLOCK], 0.0)
        score = jnp.where(score == 0.0, 0.0, score)
        bits = lax.bitcast_convert_type(score, I32)
        key = jnp.where(bits < 0, bits ^ np.int32(0x7FFFFFFF), bits)
        keys_ref[pl.ds(r0, _KCH), :] = jnp.where(r0 + row_i <= q_pos, key, INT_MIN)
        return carry
    lax.fori_loop(0, n_ch, idx_body, 0)

    def count(pred):
        def body(c, acc):
            r0 = pl.multiple_of(c * _KCH, _KCH)
            m = jnp.where(pred(keys_ref[pl.ds(r0, _KCH), :]), 1, 0).astype(I32)
            return acc + jnp.sum(m.reshape(_KCH // SUBLANES, SUBLANES, QBLOCK), axis=0)
        acc = lax.fori_loop(0, n_ch, body, jnp.zeros((SUBLANES, QBLOCK), I32))
        return jnp.sum(acc, axis=0, keepdims=True)

    def bisect(it, thr):
        cand = thr ^ jnp.left_shift(jnp.int32(1), 31 - it)
        return jnp.where(count(lambda x: x >= cand) >= topk, cand, thr)
    thr = lax.fori_loop(0, 32, bisect, jnp.full((1, QBLOCK), INT_MIN, I32))
    need = (topk - count(lambda x: x > thr)).astype(F32)

    def sel_body(c, seen):
        r0 = pl.multiple_of(c * _KCH, _KCH)
        x = keys_ref[pl.ds(r0, _KCH), :]
        eq = jnp.where(x == thr, 1.0, 0.0)
        rank = seen + _dot(stri_ref[...], eq.astype(BF16))
        tie_ok = jnp.where(x == thr, jnp.where(rank < need, 0.0, NEG_BIG), NEG_BIG)
        bias = jnp.where(x > thr, 0.0, tie_ok)
        bias_ref[pl.ds(r0, _KCH), :] = jnp.where(x == INT_MIN, NEG_BIG, bias)
        return seen + jnp.sum(eq, axis=0, keepdims=True)
    lax.fori_loop(0, n_ch, sel_body, jnp.zeros((1, QBLOCK), F32))

    qf = q_ref[...].astype(F32)
    qn = []
    for h in range(HEADS):
        qh = qf[:, h * HEAD_DIM:(h + 1) * HEAD_DIM]
        ms = jnp.mean(qh * qh, axis=-1, keepdims=True)
        qn.append((qh * lax.rsqrt(ms + EPS) * qg_ref[...] * (HEAD_DIM ** -0.5)).astype(BF16))
    acc_ref[...] = jnp.zeros_like(acc_ref)

    @pl.when(n_ch % 2 == 1)
    def _():
        bias_ref[pl.ds(pl.multiple_of(n_ch * _KCH, _KCH), _KCH), :] = jnp.full((_KCH, QBLOCK), NEG_BIG, F32)

    def att_body(c, carry):
        r0 = pl.multiple_of(c * (2 * _KCH), 2 * _KCH)
        bias = bias_ref[pl.ds(r0, 2 * _KCH), :]
        out = []
        for h in range(HEADS):
            m, l = carry[h]
            hd = slice(h * HEAD_DIM, (h + 1) * HEAD_DIM)
            sc = _dot_nt(kn_ref[pl.ds(r0, 2 * _KCH), hd], qn[h]) + bias
            m_new = jnp.maximum(m, jnp.max(sc, axis=0, keepdims=True))
            p = jnp.exp(sc - m_new)
            alpha = jnp.exp(m - m_new)
            pb = p.astype(BF16)
            pv = _dot(vt_ref[2 * c, hd, :], pb[:_KCH]) + _dot(vt_ref[2 * c + 1, hd, :], pb[_KCH:])
            acc_ref[h] = alpha * acc_ref[h] + pv
            out.append((m_new, alpha * l + jnp.sum(p, axis=0, keepdims=True)))
        return tuple(out)
    init = tuple((jnp.full((1, QBLOCK), NEG_BIG, F32), jnp.zeros((1, QBLOCK), F32)) for _ in range(HEADS))
    stats = lax.fori_loop(0, (n_ch + 1) // 2, att_body, init)
    for h in range(HEADS):
        o_ref[:, h * HEAD_DIM:(h + 1) * HEAD_DIM] = (acc_ref[h] / stats[h][1]).T.astype(o_ref.dtype)


def _dsa(zb, zi, q_gain, k_gain, bsz, seq):
    t = zb.shape[0]
    topk = min(DSA_MAX_TOPK, seq // 4)
    nq = seq // QBLOCK
    assert seq % (2 * _KCH) == 0
    idx = np.arange(_KCH)
    stri = jnp.asarray((idx[None, :] < idx[:, None]).astype(np.float32), dtype=BF16)
    const2 = lambda b, j: (0, 0)
    return pl.pallas_call(
        functools.partial(_dsa_kernel, topk),
        grid=(bsz, nq),
        in_specs=[
            pl.BlockSpec((QBLOCK, BRANCH_WIDTH), lambda b, j: (b * nq + j, 0)),
            pl.BlockSpec((seq, BRANCH_WIDTH), lambda b, j: (b, 1)),
            pl.BlockSpec((seq, BRANCH_WIDTH), lambda b, j: (b, 2)),
            pl.BlockSpec((QBLOCK, W_I), lambda b, j: (b * nq + j, 0)),
            pl.BlockSpec((seq, LANES), lambda b, j: (b, 2)),
            pl.BlockSpec((1, HEAD_DIM), const2),
            pl.BlockSpec((1, HEAD_DIM), const2),
            pl.BlockSpec((_KCH, _KCH), const2),
        ],
        out_specs=pl.BlockSpec((QBLOCK, BRANCH_WIDTH), lambda b, j: (b * nq + j, 0)),
        out_shape=jax.ShapeDtypeStruct((t, BRANCH_WIDTH), BF16),
        scratch_shapes=[
            pltpu.VMEM((seq, BRANCH_WIDTH), BF16),
            pltpu.VMEM((seq // _KCH, BRANCH_WIDTH, _KCH), BF16),
            pltpu.VMEM((seq, LANES), BF16),
            pltpu.VMEM((seq, LANES), BF16),
            pltpu.VMEM((seq, QBLOCK), I32),
            pltpu.VMEM((seq, QBLOCK), F32),
            pltpu.VMEM((HEADS, HEAD_DIM, QBLOCK), F32),
        ],
        compiler_params=_params("parallel", "arbitrary"),
        name="dsa",
    )(zb, zb, zb, zi, zi, q_gain, k_gain, stri)


def _merge_kernel(ya_ref, yb_ref, yc_ref, zm_ref, x_ref, gm_ref, wb_ref, wo_ref, o_ref):
    d = x_ref.shape[1]
    mixed = None
    for g, y_ref in enumerate((ya_ref, yb_ref, yc_ref)):
        up = _dot(y_ref[...], wb_ref[g])
        term = jax.nn.sigmoid(zm_ref[:, g * d:(g + 1) * d].astype(F32)) * up
        mixed = term if mixed is None else mixed + term
    o_ref[...] = x_ref[...] + gm_ref[0] * _dot(mixed.astype(BF16), wo_ref[...])


def _merge(ya, yb, yc, zm, x2, gate_m, w_branch, w_out, seq):
    t, d = x2.shape
    tm = 256
    per_b = seq // tm
    row = lambda i: (i, 0)
    return pl.pallas_call(
        _merge_kernel,
        grid=(t // tm,),
        in_specs=[
            pl.BlockSpec((tm, BRANCH_WIDTH), row), pl.BlockSpec((tm, BRANCH_WIDTH), row),
            pl.BlockSpec((tm, BRANCH_WIDTH), row), pl.BlockSpec((tm, W_M), row), pl.BlockSpec((tm, d), row),
            pl.BlockSpec((1, 1, d), lambda i: (i // per_b, 0, 0)),
            pl.BlockSpec(w_branch.shape, lambda i: (0, 0, 0)),
            pl.BlockSpec(w_out.shape, lambda i: (0, 0)),
        ],
        out_specs=pl.BlockSpec((tm, d), row),
        out_shape=jax.ShapeDtypeStruct((t, d), F32),
        compiler_params=_params("parallel"),
        name="merge",
    )(ya, yb, yc, zm, x2, gate_m, w_branch, w_out)


_CAND_ROWS = 72


def _cand_layout():
    flat = np.zeros((_CAND_ROWS,), np.int32)
    live = np.zeros((_CAND_ROWS,), bool)
    r = 0
    for b in range(16):
        flat[r], live[r] = b, True
        r += 1
    for a in (1, 2, 3):
        for b in range(8):
            flat[r], live[r] = a * 16 + b, True
            r += 1
    for a in range(16):
        flat[r], live[r] = a * 16, a >= 4
        r += 1
    for a in range(8):
        flat[r], live[r] = a * 16 + 1, a >= 4
        r += 1
    for a in range(8):
        flat[r], live[r] = a * 16 + 2, a == 4
        r += 1
    flat = np.where(live, flat, 1000 + np.arange(_CAND_ROWS))
    return flat.astype(np.int32), live


def _candidates(v1, v2, live):
    cand = jnp.concatenate(
        [v1[0:1, :] + v2] + [v1[a:a + 1, :] + v2[0:8, :] for a in (1, 2, 3)]
        + [v1 + v2[0:1, :], v1[0:8, :] + v2[1:2, :], v1[0:8, :] + v2[2:3, :]], axis=0)
    return jnp.where(live, cand, -jnp.inf)


def _gone(work):
    return jnp.sum(jnp.where(work == -jnp.inf, 1.0, 0.0), axis=0, keepdims=True)


def _route_head_exact(s1, s2, row_i, flat, live, v1_ref, v2_ref, b2_ref, th_ref, x2_ref):
    ranks = []
    for s, vals_ref in ((s1, v1_ref), (s2, v2_ref)):
        work = s
        rank = jnp.full(s.shape, 127.0, F32)
        for a in range(PEER_TOPK):
            m = jnp.max(work, axis=0, keepdims=True)
            idx = jnp.min(jnp.where(work == m, row_i, N_KEYS), axis=0, keepdims=True)
            hit = row_i == idx
            work = jnp.where(hit, -jnp.inf, work)
            rank = jnp.where(hit, float(a), rank)
            vals_ref[a:a + 1, :] = m
        ranks.append(rank)
    r1, r2 = ranks
    v1 = v1_ref[...]
    v2 = v2_ref[...]
    cand = _candidates(v1, v2, live)
    sel = jnp.zeros(cand.shape, F32)
    work = cand
    for _ in range(PEER_TOPK):
        m = jnp.max(work, axis=0, keepdims=True)
        fi = jnp.min(jnp.where(work == m, flat, 9999), axis=0, keepdims=True)
        hit = flat == fi
        work = jnp.where(hit, -jnp.inf, work)
        sel = jnp.where(hit, 1.0, sel)
    z = jnp.sum(sel * jnp.exp(cand - (v1[0:1, :] + v2[0:1, :])), axis=0, keepdims=True)
    b2_ref[...] = jnp.exp(s2 - v2[0:1, :]) * (0.5 / z)

    zeros8 = jnp.zeros((8, s1.shape[1]), F32)
    cnt_hi = (sel[40:56, :] + jnp.concatenate([sel[56:64, :], zeros8], axis=0)
              + jnp.concatenate([sel[64:72, :], zeros8], axis=0))
    cnt_lo = [jnp.sum(sel[0:16, :], axis=0, keepdims=True)] + [
        jnp.sum(sel[16 + 8 * (a - 1):24 + 8 * (a - 1), :], axis=0, keepdims=True) for a in (1, 2, 3)]
    cnt = jnp.zeros(s1.shape, F32)
    for a in range(PEER_TOPK):
        ca = cnt_lo[a] if a < 4 else cnt_hi[a:a + 1, :]
        cnt = jnp.where(r1 == float(a), ca, cnt)
    th_ref[...] = 0.5 - cnt
    x2_ref[...] = -r2


def _route_kernel(x_ref, gain_ref, shift_ref, scale_ref, wq_ref, sk_ref, flat_ref, live_ref,
                  ht_ref, a1_ref, th_ref, b2_ref, x2_ref, qt_ref, v1_ref, v2_ref):
    tt = x_ref.shape[0]
    h2 = _modulate(x_ref[...], gain_ref[...], shift_ref[0], scale_ref[0])
    ht = h2.T.astype(BF16)
    for g in range(tt // _TG):
        ht_ref[g] = ht[:, g * _TG:(g + 1) * _TG]
    qt_ref[...] = _dot(wq_ref[...], ht)
    row_i = lax.broadcasted_iota(I32, (N_KEYS, tt), 0)
    flat = flat_ref[...]
    live = live_ref[...] > 0.0
    n_dead = float(_CAND_ROWS - int(_cand_layout()[1].sum()))

    for h in range(PEER_HEADS):
        s1 = _dot(sk_ref[2 * h], qt_ref[(2 * h) * N_KEYS:(2 * h + 1) * N_KEYS, :].astype(BF16))
        s2 = _dot(sk_ref[2 * h + 1], qt_ref[(2 * h + 1) * N_KEYS:(2 * h + 2) * N_KEYS, :].astype(BF16))

        w1, w2 = s1, s2
        for a in range(PEER_TOPK):
            m1 = jnp.max(w1, axis=0, keepdims=True)
            m2 = jnp.max(w2, axis=0, keepdims=True)
            w1 = jnp.where(w1 == m1, -jnp.inf, w1)
            w2 = jnp.where(w2 == m2, -jnp.inf, w2)
            v1_ref[a:a + 1, :] = m1
            v2_ref[a:a + 1, :] = m2
        v1 = v1_ref[...]
        v2 = v2_ref[...]
        cand = _candidates(v1, v2, live)
        wc = cand
        for _ in range(PEER_TOPK):
            t16 = jnp.max(wc, axis=0, keepdims=True)
            wc = jnp.where(wc == t16, -jnp.inf, wc)
        t17 = jnp.max(wc, axis=0, keepdims=True)
        z = jnp.sum(jnp.where(cand >= t16, jnp.exp(cand - (v1[0:1, :] + v2[0:1, :])), 0.0), axis=0, keepdims=True)
        a1_ref[h] = jnp.exp(s1 - v1[0:1, :])
        b2_ref[h] = jnp.exp(s2 - v2[0:1, :]) * (0.5 / z)
        th_ref[h] = jnp.where(s1 >= v1[PEER_TOPK - 1:PEER_TOPK, :], 0.5 * (t16 + t17) - s1, jnp.inf)
        x2_ref[h] = jnp.where(s2 >= v2[PEER_TOPK - 1:PEER_TOPK, :], s2, -jnp.inf)

        scale = (jnp.abs(v1[0:1, :]) + jnp.abs(v1[PEER_TOPK - 1:PEER_TOPK, :])
                 + jnp.abs(v2[0:1, :]) + jnp.abs(v2[PEER_TOPK - 1:PEER_TOPK, :]))
        close = jnp.where(t16 - t17 > scale * 2.0 ** -22, 0.0, 1.0)
        bad = (jnp.abs(_gone(w1) - PEER_TOPK) + jnp.abs(_gone(w2) - PEER_TOPK)
               + jnp.abs(_gone(wc) - (PEER_TOPK + n_dead)) + close)

        @pl.when(jnp.max(bad) > 0.0)
        def _():
            _route_head_exact(s1, s2, row_i, flat, live, v1_ref, v2_ref, b2_ref.at[h], th_ref.at[h], x2_ref.at[h])


def _route(x2, gain, shift, scale, wq_t, sub_keys, seq):
    t, d = x2.shape
    tt = 256
    per_b = seq // tt
    flat_np, live_np = _cand_layout()
    flat = jnp.asarray(np.broadcast_to(flat_np[:, None], (_CAND_ROWS, tt)).copy())
    live = jnp.asarray(np.broadcast_to(live_np[:, None], (_CAND_ROWS, tt)).astype(np.float32))
    dense = jax.ShapeDtypeStruct((PEER_HEADS, N_KEYS, t), F32)
    dense_spec = pl.BlockSpec((PEER_HEADS, N_KEYS, tt), lambda i: (0, 0, i))
    const2 = lambda i: (0, 0)
    return pl.pallas_call(
        _route_kernel,
        grid=(t // tt,),
        in_specs=[
            pl.BlockSpec((tt, d), lambda i: (i, 0)),
            pl.BlockSpec((1, d), const2),
            pl.BlockSpec((1, 1, d), lambda i: (i // per_b, 0, 0)),
            pl.BlockSpec((1, 1, d), lambda i: (i // per_b, 0, 0)),
            pl.BlockSpec(wq_t.shape, const2),
            pl.BlockSpec(sub_keys.shape, lambda i: (0, 0, 0)),
            pl.BlockSpec((_CAND_ROWS, tt), const2),
            pl.BlockSpec((_CAND_ROWS, tt), const2),
        ],
        out_specs=[pl.BlockSpec((tt // _TG, d, _TG), lambda i: (i, 0, 0)),
                   dense_spec, dense_spec, dense_spec, dense_spec],
        out_shape=[jax.ShapeDtypeStruct((t // _TG, d, _TG), BF16), dense, dense, dense, dense],
        scratch_shapes=[
            pltpu.VMEM((PEER_HEADS * 2 * N_KEYS, tt), F32),
            pltpu.VMEM((PEER_TOPK, tt), F32),
            pltpu.VMEM((PEER_TOPK, tt), F32),
        ],
        compiler_params=_params("parallel"),
        name="peer_route",
    )(x2, gain, shift, scale, wq_t, sub_keys, flat, live)


_JB = 4
_IB = 4
_TG = 256
_RB = _IB * N_KEYS


def _peer_kernel(n_tiles, ht_ref, u_ref, vt_ref, a1_ref, th_ref, b2_ref, x2_ref, x_ref, gf_ref, o_ref,
                 acc_ref, at_ref, gt_ref, bc_ref):
    s = pl.program_id(1)
    n_g, te, tg = at_ref.shape
    tt = n_g * tg
    n_i = te // N_KEYS
    n_jb = N_KEYS // (SUBLANES * _JB)
    assert n_i == SUBLANES
    assert _RB == _IB * N_KEYS

    def bcast_rows():
        i0 = pl.multiple_of(s * n_i, SUBLANES)
        for h in range(PEER_HEADS):
            a1g = a1_ref[h, pl.ds(i0, n_i), :]
            thg = th_ref[h, pl.ds(i0, n_i), :]
            for ii in range(n_i):
                bc_ref[2 * (h * n_i + ii)] = jnp.broadcast_to(a1g[ii:ii + 1, :], (SUBLANES, tt))
                bc_ref[2 * (h * n_i + ii) + 1] = jnp.broadcast_to(thg[ii:ii + 1, :], (SUBLANES, tt))

    def mask_block(tb, jb, ib):
        g, lane0 = divmod(tb * LANES, tg)
        lanes = slice(tb * LANES, (tb + 1) * LANES)
        lanes_g = slice(lane0, lane0 + LANES)
        j0 = jb * (SUBLANES * _JB)
        w = [[None] * _JB for _ in range(_IB)]
        for h in range(PEER_HEADS):
            x2 = [x2_ref[h, j0 + SUBLANES * k:j0 + SUBLANES * (k + 1), lanes] for k in range(_JB)]
            b2 = [b2_ref[h, j0 + SUBLANES * k:j0 + SUBLANES * (k + 1), lanes] for k in range(_JB)]
            for di in range(_IB):
                row = 2 * (h * n_i + ib * _IB + di)
                a1v = bc_ref[row, :, lanes]
                thv = bc_ref[row + 1, :, lanes]
                for k in range(_JB):
                    term = jnp.where(x2[k] >= thv, a1v * b2[k], 0.0)
                    w[di][k] = term if w[di][k] is None else w[di][k] + term
        for di in range(_IB):
            r0 = (ib * _IB + di) * N_KEYS + j0
            rows = slice(r0, r0 + SUBLANES * _JB)
            a = at_ref[g, rows, lanes_g]
            act2 = a + a * lax.erf(a * np.float32(np.sqrt(0.5)))
            gt_ref[g, rows, lanes_g] = (jnp.concatenate(w[di], axis=0) * act2).astype(BF16)

    @pl.when(s == 0)
    def _():
        acc_ref[...] = jnp.zeros_like(acc_ref)

    def pre_act(k):
        rows = slice(k * _RB, (k + 1) * _RB)
        for g in range(n_g):
            at_ref[g, rows, :] = _dot(u_ref[rows, :], ht_ref[g])

    def fold(k):
        rows = slice(k * _RB, (k + 1) * _RB)
        for g in range(n_g):
            acc_ref[g] += _dot(vt_ref[:, rows], gt_ref[g, rows, :])

    bcast_rows()
    n_slices = te // _RB
    pre_act(0)
    for k in range(n_slices):
        if k + 1 < n_slices:
            pre_act(k + 1)
        if k >= 1:
            fold(k - 1)
        for tb in range(tt // LANES):
            for jb in range(n_jb):
                mask_block(tb, jb, k)
    fold(n_slices - 1)

    @pl.when(s == n_tiles - 1)
    def _():
        for g in range(n_g):
            rows = slice(g * tg, (g + 1) * tg)
            o_ref[rows, :] = x_ref[rows, :] + gf_ref[0] * acc_ref[g].T


def _peer(ht, u, vt, a1, th, b2, xk, x2, gate_f, seq):
    t, d = x2.shape
    n_tiles = u.shape[0] // PEER_TE
    tt = 512
    per_b = seq // tt
    f32_spec = pl.BlockSpec((PEER_HEADS, N_KEYS, tt), lambda i, s: (0, 0, i))
    return pl.pallas_call(
        functools.partial(_peer_kernel, n_tiles),
        grid=(t // tt, n_tiles),
        in_specs=[
            pl.BlockSpec((tt // _TG, d, _TG), lambda i, s: (i, 0, 0)),
            pl.BlockSpec((PEER_TE, d), lambda i, s: (s, 0)),
            pl.BlockSpec((d, PEER_TE), lambda i, s: (0, s)),
            f32_spec, f32_spec, f32_spec, f32_spec,
            pl.BlockSpec((tt, d), lambda i, s: (i, 0)),
            pl.BlockSpec((1, 1, d), lambda i, s: (i // per_b, 0, 0)),
        ],
        out_specs=pl.BlockSpec((tt, d), lambda i, s: (i, 0)),
        out_shape=jax.ShapeDtypeStruct((t, d), F32),
        scratch_shapes=[
            pltpu.VMEM((tt // _TG, d, _TG), F32),
            pltpu.VMEM((tt // _TG, PEER_TE, _TG), F32),
            pltpu.VMEM((tt // _TG, PEER_TE, _TG), BF16),
            pltpu.VMEM((2 * PEER_HEADS * PEER_TE // N_KEYS, SUBLANES, tt), F32),
        ],
        compiler_params=_params("parallel", "arbitrary"),
        name="peer_experts",
    )(ht, u, vt, a1, th, b2, xk, x2, gate_f)


def _tcast_kernel(x_ref, o_ref):
    o_ref[...] = x_ref[...].T.astype(o_ref.dtype)


def _cast_kernel(x_ref, o_ref):
    o_ref[...] = x_ref[...].astype(o_ref.dtype)


def _layer_cast(x, layer, dtype, transpose):
    _, r, c = x.shape
    tr, tc = min(r, 1024), min(c, 1024)
    return pl.pallas_call(
        _tcast_kernel if transpose else _cast_kernel,
        grid=(r // tr, c // tc),
        in_specs=[pl.BlockSpec((None, tr, tc), lambda i, j: (layer, i, j))],
        out_specs=pl.BlockSpec((tc, tr), lambda i, j: (j, i)) if transpose else pl.BlockSpec((tr, tc), lambda i, j: (i, j)),
        out_shape=jax.ShapeDtypeStruct((c, r) if transpose else (r, c), dtype),
        compiler_params=_params("parallel", "parallel"),
        name="layer_cast",
    )(x)


def _pad_heads(w, width):
    d = w.shape[0]
    w = w.reshape(d, HEADS, width)
    return jnp.pad(w, ((0, 0), (0, 0), (0, HEAD_DIM - width))).reshape(d, HEADS * HEAD_DIM)


def _pack_segments():
    segs = [(0, W_A + W_B, 0)]
    n_idx = IDX_HEADS * IDX_DIM + IDX_DIM + IDX_HEADS
    src = W_A + W_B
    dst = W_A + W_B
    segs.append((src, n_idx, dst)); src += n_idx; dst += W_I
    for _ in range(2):
        for h in range(HEADS):
            segs.append((src + GLA_DK * h, GLA_DK, dst + HEAD_DIM * h))
        src += HEADS * GLA_DK; dst += BRANCH_WIDTH
    segs.append((src, BRANCH_WIDTH, dst)); src += BRANCH_WIDTH; dst += BRANCH_WIDTH
    code_src = src; src += GLA_GATE_RANK
    segs.append((src, BRANCH_WIDTH, dst)); src += BRANCH_WIDTH; dst += BRANCH_WIDTH
    segs.append((code_src, GLA_GATE_RANK, dst)); dst += LANES
    segs.append((src, W_M, dst))
    assert dst + W_M == W_PACK
    return segs


def _pack_kernel(w_ref, o_ref):
    o_ref[...] = jnp.zeros_like(o_ref)
    for src, width, dst in _pack_segments():
        o_ref[0, :, dst:dst + width] = w_ref[0, :, src:src + width].astype(o_ref.dtype)


def _pack_w_in(w_in):
    depth, d, n_in = w_in.shape
    tr = 128
    return pl.pallas_call(
        _pack_kernel,
        grid=(depth, d // tr),
        in_specs=[pl.BlockSpec((1, tr, n_in), lambda l, i: (l, i, 0))],
        out_specs=pl.BlockSpec((1, tr, W_PACK), lambda l, i: (l, i, 0)),
        out_shape=jax.ShapeDtypeStruct((depth, d, W_PACK), BF16),
        compiler_params=_params("parallel", "parallel"),
        name="pack_w_in",
    )(w_in)


def kernel(x, c, w_ada, b_ada, norm_mix, norm_ffn, w_in, hgrn_lb_logits, hgrn_out_norm, dsa_q_norm, dsa_k_norm,
           gla_gate_up, gla_gate_bias, gla_out_norm, w_branch, w_out, peer_w_query, peer_sub_keys, peer_u, peer_v):
    bsz, seq, d = x.shape
    depth = w_in.shape[0]
    t = bsz * seq
    x2 = x.reshape(t, d)
    mod = _ada(c, w_ada, b_ada)
    w_pack = _pack_w_in(w_in)

    for l in range(depth):
        shift_m, scale_m, gate_m, shift_f, scale_f, gate_f = [
            mod[l, :, k * d:(k + 1) * d].reshape(bsz, 1, d) for k in range(6)]
        za, zb, zi, zc, zm = _inproj(x2, norm_mix[l].reshape(1, d), shift_m, scale_m, w_pack, l, seq)

        ya = _scan("hgrn", l, za, (hgrn_lb_logits,), hgrn_out_norm[l].reshape(1, HEAD_DIM), bsz, seq)
        yb = _dsa(zb, zi, dsa_q_norm[l].reshape(1, HEAD_DIM), dsa_k_norm[l].reshape(1, HEAD_DIM), bsz, seq)
        gup = jnp.pad(_pad_heads(gla_gate_up[l], GLA_DK), ((0, LANES - GLA_GATE_RANK), (0, 0)))
        gb = _pad_heads(gla_gate_bias[l].reshape(1, HEADS * GLA_DK), GLA_DK)
        yc = _scan("gla", l, zc, (gup, gb), gla_out_norm[l].reshape(1, HEAD_DIM), bsz, seq)

        x2 = _merge(ya, yb, yc, zm, x2, gate_m, w_branch[l].astype(BF16), w_out[l].astype(BF16), seq)

        ht, a1, th, b2, xk = _route(
            x2, norm_ffn[l].reshape(1, d), shift_f, scale_f, _layer_cast(peer_w_query, l, BF16, True),
            peer_sub_keys[l].reshape(PEER_HEADS * 2, N_KEYS, -1).astype(BF16), seq)
        x2 = _peer(ht, _layer_cast(peer_u, l, BF16, False), _layer_cast(peer_v, l, BF16, True),
                   a1, th, b2, xk, x2, gate_f, seq)

    return x2.reshape(bsz, seq, d)
```

```python
import functools

import numpy as np
import jax
import jax.numpy as jnp
from jax import lax
from jax.experimental import pallas as pl
from jax.experimental.pallas import tpu as pltpu

F32, BF16, I32 = jnp.float32, jnp.bfloat16, jnp.int32

D_MODEL = 1024
HEADS = 4
HEAD_DIM = 128
BRANCH_WIDTH = HEADS * HEAD_DIM
IDX_HEADS = 4
IDX_DIM = 64
DSA_MAX_TOPK = 256
QBLOCK = 256
GLA_DK = 64
GLA_GATE_RANK = 16
GLA_TAU = 16.0
PEER_HEADS = 8
N_KEYS = 128
PEER_TOPK = 16
PEER_TE = 1024
EPS = 1e-6

LANES = 128
SUBLANES = 8
VMEM_LIMIT_BYTES = 56 * 1024 * 1024

NEG_BIG = -1e30
INT_MIN = np.int32(-2 ** 31)

W_A = 4 * BRANCH_WIDTH
W_B = 3 * BRANCH_WIDTH
W_I = 3 * LANES
W_C = 4 * BRANCH_WIDTH + LANES
W_M = 3 * D_MODEL
W_PACK = W_A + W_B + W_I + W_C + W_M


def _dot(a, b):
    return jnp.dot(a, b, preferred_element_type=F32)


def _dot_nt(a, b):
    return lax.dot_general(a, b, (((1,), (1,)), ((), ())), preferred_element_type=F32)


def _dot_tn(a, b):
    return lax.dot_general(a, b, (((0,), (0,)), ((), ())), preferred_element_type=F32)


def _split2(x):
    hi = x.astype(BF16)
    lo = (x - hi.astype(F32)).astype(BF16)
    return hi, lo


def _split3(x):
    hi = x.astype(BF16)
    r = x - hi.astype(F32)
    mid = r.astype(BF16)
    lo = (r - mid.astype(F32)).astype(BF16)
    return hi, mid, lo


def _params(*sem):
    return pltpu.CompilerParams(dimension_semantics=sem, vmem_limit_bytes=VMEM_LIMIT_BYTES)


def _modulate(x, gain, shift, scale):
    ms = jnp.mean(x * x, axis=-1, keepdims=True)
    return x * lax.rsqrt(ms + EPS) * gain * (1.0 + scale) + shift


def _ada_kernel(c_ref, w_ref, b_ref, o_ref):
    c = c_ref[...]
    sc = c * jax.nn.sigmoid(c)
    a_hi, a_lo = _split2(sc)
    w_hi, w_lo = _split2(w_ref[0])
    o_ref[0] = _dot(a_hi, w_hi) + _dot(a_hi, w_lo) + _dot(a_lo, w_hi) + b_ref[0]


def _ada(c, w_ada, b_ada):
    depth, d, n = w_ada.shape
    bsz = c.shape[0]
    tn = 1536
    return pl.pallas_call(
        _ada_kernel,
        grid=(depth, n // tn),
        in_specs=[
            pl.BlockSpec((bsz, d), lambda l, j: (0, 0)),
            pl.BlockSpec((1, d, tn), lambda l, j: (l, 0, j)),
            pl.BlockSpec((1, 1, tn), lambda l, j: (l, 0, j)),
        ],
        out_specs=pl.BlockSpec((1, bsz, tn), lambda l, j: (l, 0, j)),
        out_shape=jax.ShapeDtypeStruct((depth, bsz, n), F32),
        compiler_params=_params("parallel", "parallel"),
        name="ada",
    )(c, w_ada, b_ada.reshape(depth, 1, n))


def _inproj_kernel(x_ref, gain_ref, shift_ref, scale_ref, w_ref, *out_refs):
    h = _modulate(x_ref[...], gain_ref[...], shift_ref[0], scale_ref[0]).astype(BF16)
    off = 0
    for o_ref in out_refs:
        width = o_ref.shape[1]
        for c0 in range(0, width, 512):
            c1 = min(c0 + 512, width)
            o_ref[:, c0:c1] = _dot(h, w_ref[:, off + c0:off + c1]).astype(o_ref.dtype)
        off += width


def _inproj(x2, gain, shift, scale, w_pack, layer, seq):
    t, d = x2.shape
    tm = 256
    per_b = seq // tm
    widths = (W_A, W_B, W_I, W_C, W_M)
    dtypes = (F32, BF16, F32, F32, BF16)
    return pl.pallas_call(
        _inproj_kernel,
        grid=(t // tm,),
        in_specs=[
            pl.BlockSpec((tm, d), lambda i: (i, 0)),
            pl.BlockSpec((1, d), lambda i: (0, 0)),
            pl.BlockSpec((1, 1, d), lambda i: (i // per_b, 0, 0)),
            pl.BlockSpec((1, 1, d), lambda i: (i // per_b, 0, 0)),
            pl.BlockSpec((None, d, W_PACK), lambda i: (layer, 0, 0), pipeline_mode=pl.Buffered(1)),
        ],
        out_specs=[pl.BlockSpec((tm, w), lambda i: (i, 0)) for w in widths],
        out_shape=[jax.ShapeDtypeStruct((t, w), dt) for w, dt in zip(widths, dtypes)],
        compiler_params=_params("parallel"),
        name="inproj",
    )(x2, gain, shift, scale, w_pack)


def _group_row_bcast(x, group, row):
    n, w = x.shape
    if group >= SUBLANES:
        x3 = x.reshape(n // group, group, w)
        return jnp.broadcast_to(x3[:, row:row + 1, :], x3.shape).reshape(n, w)
    x3 = x.reshape(n // SUBLANES, SUBLANES, w)
    sub = lax.broadcasted_iota(I32, x3.shape, 1)
    n_groups = SUBLANES // group
    res = None
    for g in reversed(range(n_groups)):
        r = g * group + row
        bc = jnp.broadcast_to(x3[:, r:r + 1, :], x3.shape)
        res = bc if res is None else jnp.where(sub < (g + 1) * group, bc, res)
    return res.reshape(n, w)


def _scan_kernel(mode, layer, *refs):
    if mode == "hgrn":
        q_ref, f_ref, v_ref, g_ref, lbl_ref, gain_ref, lv_ref, tri_ref, y_ref, st_ref = refs
    else:
        q_ref, k_ref, v_ref, g_ref, code_ref, gup_ref, gb_ref, gain_ref, lv_ref, tri_ref, y_ref, st_ref = refs
    lt = q_ref.shape[0]
    n_levels = lt.bit_length() - 1

    @pl.when(pl.program_id(1) == 0)
    def _():
        st_ref[...] = jnp.zeros_like(st_ref)

    tri = tri_ref[...]
    lv = lv_ref[...]
    for h in range(HEADS):
        cols = slice(h * HEAD_DIM, (h + 1) * HEAD_DIM)
        if mode == "hgrn":
            lbl = lbl_ref[:, cols]
            e = jnp.exp(lbl - jnp.max(lbl, axis=0, keepdims=True))
            p = e / jnp.sum(e, axis=0, keepdims=True)
            lb = jnp.zeros((1, p.shape[1]), F32)
            for l2 in range(1, layer + 1):
                lb = lb + p[l2:l2 + 1, :]
            f = lb + (1.0 - lb) * jax.nn.sigmoid(f_ref[:, cols])
            lg = jnp.log(f)
            kk = 1.0 - f
            q = q_ref[:, cols] * (HEAD_DIM ** -0.5)
        else:
            c_hi, c_lo = _split2(code_ref[...])
            u_hi, u_lo = _split2(gup_ref[:, cols])
            z = _dot(c_hi, u_hi) + _dot(c_hi, u_lo) + _dot(c_lo, u_hi) + gb_ref[:, cols]
            lg = (jnp.minimum(z, 0.0) - jnp.log1p(jnp.exp(-jnp.abs(z)))) * (1.0 / GLA_TAU)
            kk = k_ref[:, cols]
            q = q_ref[:, cols] * (GLA_DK ** -0.5)

        g_hi, g_mid, g_lo = _split3(lg)
        b = _dot(tri, g_hi) + _dot(tri, g_mid) + _dot(tri, g_lo)

        s = jnp.where(lv == -1, _dot_nt(q.astype(BF16), kk.astype(BF16)), 0.0)
        for l in range(n_levels):
            n = 1 << l
            bref = _group_row_bcast(b, 2 * n, n - 1)
            ql = (q * jnp.exp(jnp.minimum(b - bref, 0.0))).astype(BF16)
            kl = (kk * jnp.exp(jnp.minimum(bref - b, 0.0))).astype(BF16)
            s = jnp.where(lv == l, _dot_nt(ql, kl), s)

        vb = v_ref[:, cols].astype(BF16)
        st = st_ref[h]
        o = _dot(s.astype(BF16), vb) + _dot_nt((q * jnp.exp(b)).astype(BF16), st.astype(BF16))
        b_last = b[lt - 1:lt, :]
        kd = (kk * jnp.exp(b_last - b)).astype(BF16)
        st_ref[h] = st * jnp.exp(b_last) + _dot_tn(vb, kd)

        ms = jnp.mean(o * o, axis=-1, keepdims=True)
        g = g_ref[:, cols]
        y = o * lax.rsqrt(ms + EPS) * gain_ref[...] * (g * jax.nn.sigmoid(g))
        y_ref[:, cols] = y.astype(y_ref.dtype)


def _scan_consts(lt):
    idx = np.arange(lt)
    x = idx[:, None] ^ idx[None, :]
    lvl = np.floor(np.log2(np.maximum(x, 1))).astype(np.int32)
    lv = np.where(idx[None, :] < idx[:, None], lvl, np.where(x == 0, -1, -2)).astype(np.int32)
    tri = (idx[None, :] <= idx[:, None]).astype(np.float32)
    return jnp.asarray(lv), jnp.asarray(tri, dtype=BF16)


def _scan(mode, layer, z, extra, out_gain, bsz, seq):
    t = z.shape[0]
    lt = 256
    per_b = seq // lt
    lv, tri = _scan_consts(lt)

    def col(cb):
        return pl.BlockSpec((lt, BRANCH_WIDTH), lambda b, c: (b * per_b + c, cb))

    const2 = lambda b, c: (0, 0)
    in_specs = [col(0), col(1), col(2), col(3)]
    args = [z, z, z, z]
    if mode == "hgrn":
        (lb_logits,) = extra
        in_specs.append(pl.BlockSpec(lb_logits.shape, const2))
        args.append(lb_logits)
    else:
        gup, gb = extra
        in_specs += [pl.BlockSpec((lt, LANES), lambda b, c: (b * per_b + c, 4 * HEADS)),
                     pl.BlockSpec(gup.shape, const2), pl.BlockSpec(gb.shape, const2)]
        args += [z, gup, gb]
    in_specs += [pl.BlockSpec((1, HEAD_DIM), const2), pl.BlockSpec((lt, lt), const2), pl.BlockSpec((lt, lt), const2)]
    args += [out_gain, lv, tri]
    return pl.pallas_call(
        functools.partial(_scan_kernel, mode, layer),
        grid=(bsz, per_b),
        in_specs=in_specs,
        out_specs=pl.BlockSpec((lt, BRANCH_WIDTH), lambda b, c: (b * per_b + c, 0)),
        out_shape=jax.ShapeDtypeStruct((t, BRANCH_WIDTH), BF16),
        scratch_shapes=[pltpu.VMEM((HEADS, HEAD_DIM, HEAD_DIM), F32)],
        compiler_params=_params("parallel", "arbitrary"),
        name="scan_" + mode,
    )(*args)


_KCH = 256


def _dsa_kernel(topk, q_ref, k_ref, v_ref, ziq_ref, zik_ref, qg_ref, kg_ref, stri_ref, o_ref,
                kn_ref, vt_ref, kih_ref, kil_ref, keys_ref, bias_ref, acc_ref):
    j = pl.program_id(1)
    seq = k_ref.shape[0]

    @pl.when(j == 0)
    def _prep():
        def body(c, carry):
            r0 = pl.multiple_of(c * _KCH, _KCH)
            kc = k_ref[pl.ds(r0, _KCH), :].astype(F32)
            for h in range(HEADS):
                kh = kc[:, h * HEAD_DIM:(h + 1) * HEAD_DIM]
                ms = jnp.mean(kh * kh, axis=-1, keepdims=True)
                kn_ref[pl.ds(r0, _KCH), h * HEAD_DIM:(h + 1) * HEAD_DIM] = (
                    kh * lax.rsqrt(ms + EPS) * kg_ref[...]).astype(BF16)
            vt_ref[c] = v_ref[pl.ds(r0, _KCH), :].astype(F32).T.astype(BF16)
            ki = zik_ref[pl.ds(r0, _KCH), :]
            hi = ki.astype(BF16)
            kih_ref[pl.ds(r0, _KCH), :] = hi
            kil_ref[pl.ds(r0, _KCH), :] = (ki - hi.astype(F32)).astype(BF16)
            return carry
        lax.fori_loop(0, seq // _KCH, body, 0)

    n_ch = (j * QBLOCK + QBLOCK + _KCH - 1) // _KCH
    ziq = ziq_ref[...]
    w_t = ziq[:, 2 * LANES:3 * LANES].T
    qi = jnp.concatenate([ziq[:, h * IDX_DIM:(h + 1) * IDX_DIM] for h in range(IDX_HEADS)], axis=0)
    qi_hi, qi_lo = _split2(qi * (IDX_DIM ** -0.5))
    w_rows = [w_t[IDX_DIM + h:IDX_DIM + h + 1, :] * (IDX_HEADS ** -0.5) for h in range(IDX_HEADS)]
    q_pos = j * QBLOCK + lax.broadcasted_iota(I32, (_KCH, QBLOCK), 1)
    row_i = lax.broadcasted_iota(I32, (_KCH, QBLOCK), 0)

    def idx_body(c, carry):
        r0 = pl.multiple_of(c * _KCH, _KCH)
        k_hi = kih_ref[pl.ds(r0, _KCH), :][:, :IDX_DIM]
        k_lo = kil_ref[pl.ds(r0, _KCH), :][:, :IDX_DIM]
        logit = _dot_nt(k_hi, qi_hi) + _dot_nt(k_hi, qi_lo) + _dot_nt(k_lo, qi_hi)
        score = jnp.zeros((_KCH, QBLOCK), F32)
        for h in range(IDX_HEADS):
            score = score + w_rows[h] * jnp.maximum(logit[:, h * QBLOCK:(h + 1) * QBLOCK], 0.0)
        score = jnp.where(score == 0.0, 0.0, score)
        bits = lax.bitcast_convert_type(score, I32)
        key = jnp.where(bits < 0, bits ^ np.int32(0x7FFFFFFF), bits)
        keys_ref[pl.ds(r0, _KCH), :] = jnp.where(r0 + row_i <= q_pos, key, INT_MIN)
        return carry
    lax.fori_loop(0, n_ch, idx_body, 0)

    def count(pred):
        def body(c, acc):
            r0 = pl.multiple_of(c * _KCH, _KCH)
            m = jnp.where(pred(keys_ref[pl.ds(r0, _KCH), :]), 1, 0).astype(I32)
            return acc + jnp.sum(m.reshape(_KCH // SUBLANES, SUBLANES, QBLOCK), axis=0)
        acc = lax.fori_loop(0, n_ch, body, jnp.zeros((SUBLANES, QBLOCK), I32))
        return jnp.sum(acc, axis=0, keepdims=True)

    def bisect(it, thr):
        cand = thr ^ jnp.left_shift(jnp.int32(1), 31 - it)
        return jnp.where(count(lambda x: x >= cand) >= topk, cand, thr)
    thr = lax.fori_loop(0, 32, bisect, jnp.full((1, QBLOCK), INT_MIN, I32))
    need = (topk - count(lambda x: x > thr)).astype(F32)

    def sel_body(c, seen):
        r0 = pl.multiple_of(c * _KCH, _KCH)
        x = keys_ref[pl.ds(r0, _KCH), :]
        eq = jnp.where(x == thr, 1.0, 0.0)
        rank = seen + _dot(stri_ref[...], eq.astype(BF16))
        tie_ok = jnp.where(x == thr, jnp.where(rank < need, 0.0, NEG_BIG), NEG_BIG)
        bias = jnp.where(x > thr, 0.0, tie_ok)
        bias_ref[pl.ds(r0, _KCH), :] = jnp.where(x == INT_MIN, NEG_BIG, bias)
        return seen + jnp.sum(eq, axis=0, keepdims=True)
    lax.fori_loop(0, n_ch, sel_body, jnp.zeros((1, QBLOCK), F32))

    qf = q_ref[...].astype(F32)
    qn = []
    for h in range(HEADS):
        qh = qf[:, h * HEAD_DIM:(h + 1) * HEAD_DIM]
        ms = jnp.mean(qh * qh, axis=-1, keepdims=True)
        qn.append((qh * lax.rsqrt(ms + EPS) * qg_ref[...] * (HEAD_DIM ** -0.5)).astype(BF16))
    acc_ref[...] = jnp.zeros_like(acc_ref)

    @pl.when(n_ch % 2 == 1)
    def _():
        bias_ref[pl.ds(pl.multiple_of(n_ch * _KCH, _KCH), _KCH), :] = jnp.full((_KCH, QBLOCK), NEG_BIG, F32)

    def att_body(c, carry):
        r0 = pl.multiple_of(c * (2 * _KCH), 2 * _KCH)
        bias = bias_ref[pl.ds(r0, 2 * _KCH), :]
        out = []
        for h in range(HEADS):
            m, l = carry[h]
            hd = slice(h * HEAD_DIM, (h + 1) * HEAD_DIM)
            sc = _dot_nt(kn_ref[pl.ds(r0, 2 * _KCH), hd], qn[h]) + bias
            m_new = jnp.maximum(m, jnp.max(sc, axis=0, keepdims=True))
            p = jnp.exp(sc - m_new)
            alpha = jnp.exp(m - m_new)
            pb = p.astype(BF16)
            pv = _dot(vt_ref[2 * c, hd, :], pb[:_KCH]) + _dot(vt_ref[2 * c + 1, hd, :], pb[_KCH:])
            acc_ref[h] = alpha * acc_ref[h] + pv
            out.append((m_new, alpha * l + jnp.sum(p, axis=0, keepdims=True)))
        return tuple(out)
    init = tuple((jnp.full((1, QBLOCK), NEG_BIG, F32), jnp.zeros((1, QBLOCK), F32)) for _ in range(HEADS))
    stats = lax.fori_loop(0, (n_ch + 1) // 2, att_body, init)
    for h in range(HEADS):
        o_ref[:, h * HEAD_DIM:(h + 1) * HEAD_DIM] = (acc_ref[h] / stats[h][1]).T.astype(o_ref.dtype)


def _dsa(zb, zi, q_gain, k_gain, bsz, seq):
    t = zb.shape[0]
    topk = min(DSA_MAX_TOPK, seq // 4)
    nq = seq // QBLOCK
    assert seq % (2 * _KCH) == 0
    idx = np.arange(_KCH)
    stri = jnp.asarray((idx[None, :] < idx[:, None]).astype(np.float32), dtype=BF16)
    const2 = lambda b, j: (0, 0)
    return pl.pallas_call(
        functools.partial(_dsa_kernel, topk),
        grid=(bsz, nq),
        in_specs=[
            pl.BlockSpec((QBLOCK, BRANCH_WIDTH), lambda b, j: (b * nq + j, 0)),
            pl.BlockSpec((seq, BRANCH_WIDTH), lambda b, j: (b, 1)),
            pl.BlockSpec((seq, BRANCH_WIDTH), lambda b, j: (b, 2)),
            pl.BlockSpec((QBLOCK, W_I), lambda b, j: (b * nq + j, 0)),
            pl.BlockSpec((seq, LANES), lambda b, j: (b, 2)),
            pl.BlockSpec((1, HEAD_DIM), const2),
            pl.BlockSpec((1, HEAD_DIM), const2),
            pl.BlockSpec((_KCH, _KCH), const2),
        ],
        out_specs=pl.BlockSpec((QBLOCK, BRANCH_WIDTH), lambda b, j: (b * nq + j, 0)),
        out_shape=jax.ShapeDtypeStruct((t, BRANCH_WIDTH), BF16),
        scratch_shapes=[
            pltpu.VMEM((seq, BRANCH_WIDTH), BF16),
            pltpu.VMEM((seq // _KCH, BRANCH_WIDTH, _KCH), BF16),
            pltpu.VMEM((seq, LANES), BF16),
            pltpu.VMEM((seq, LANES), BF16),
            pltpu.VMEM((seq, QBLOCK), I32),
            pltpu.VMEM((seq, QBLOCK), F32),
            pltpu.VMEM((HEADS, HEAD_DIM, QBLOCK), F32),
        ],
        compiler_params=_params("parallel", "arbitrary"),
        name="dsa",
    )(zb, zb, zb, zi, zi, q_gain, k_gain, stri)


def _merge_kernel(ya_ref, yb_ref, yc_ref, zm_ref, x_ref, gm_ref, wb_ref, wo_ref, o_ref):
    d = x_ref.shape[1]
    mixed = None
    for g, y_ref in enumerate((ya_ref, yb_ref, yc_ref)):
        up = _dot(y_ref[...], wb_ref[g])
        term = jax.nn.sigmoid(zm_ref[:, g * d:(g + 1) * d].astype(F32)) * up
        mixed = term if mixed is None else mixed + term
    o_ref[...] = x_ref[...] + gm_ref[0] * _dot(mixed.astype(BF16), wo_ref[...])


def _merge(ya, yb, yc, zm, x2, gate_m, w_branch, w_out, seq):
    t, d = x2.shape
    tm = 256
    per_b = seq // tm
    row = lambda i: (i, 0)
    return pl.pallas_call(
        _merge_kernel,
        grid=(t // tm,),
        in_specs=[
            pl.BlockSpec((tm, BRANCH_WIDTH), row), pl.BlockSpec((tm, BRANCH_WIDTH), row),
            pl.BlockSpec((tm, BRANCH_WIDTH), row), pl.BlockSpec((tm, W_M), row), pl.BlockSpec((tm, d), row),
            pl.BlockSpec((1, 1, d), lambda i: (i // per_b, 0, 0)),
            pl.BlockSpec(w_branch.shape, lambda i: (0, 0, 0)),
            pl.BlockSpec(w_out.shape, lambda i: (0, 0)),
        ],
        out_specs=pl.BlockSpec((tm, d), row),
        out_shape=jax.ShapeDtypeStruct((t, d), F32),
        compiler_params=_params("parallel"),
        name="merge",
    )(ya, yb, yc, zm, x2, gate_m, w_branch, w_out)


_CAND_ROWS = 72
_ROUTE_GROUP = 4


def _cand_layout():
    flat = np.zeros((_CAND_ROWS,), np.int32)
    live = np.zeros((_CAND_ROWS,), bool)
    r = 0
    for b in range(16):
        flat[r], live[r] = b, True
        r += 1
    for a in (1, 2, 3):
        for b in range(8):
            flat[r], live[r] = a * 16 + b, True
            r += 1
    for a in range(16):
        flat[r], live[r] = a * 16, a >= 4
        r += 1
    for a in range(8):
        flat[r], live[r] = a * 16 + 1, a >= 4
        r += 1
    for a in range(8):
        flat[r], live[r] = a * 16 + 2, a == 4
        r += 1
    flat = np.where(live, flat, 1000 + np.arange(_CAND_ROWS))
    return flat.astype(np.int32), live


def _candidates(v1, v2, live):
    cand = jnp.concatenate(
        [v1[0:1, :] + v2] + [v1[a:a + 1, :] + v2[0:8, :] for a in (1, 2, 3)]
        + [v1 + v2[0:1, :], v1[0:8, :] + v2[1:2, :], v1[0:8, :] + v2[2:3, :]], axis=0)
    return jnp.where(live, cand, -jnp.inf)


def _gone(work):
    return jnp.sum(jnp.where(work == -jnp.inf, 1.0, 0.0), axis=0, keepdims=True)


def _route_head_exact(s1, s2, row_i, flat, live, v1_ref, v2_ref, b2_ref, th_ref, x2_ref):
    ranks = []
    for s, vals_ref in ((s1, v1_ref), (s2, v2_ref)):
        work = s
        rank = jnp.full(s.shape, 127.0, F32)
        for a in range(PEER_TOPK):
            m = jnp.max(work, axis=0, keepdims=True)
            idx = jnp.min(jnp.where(work == m, row_i, N_KEYS), axis=0, keepdims=True)
            hit = row_i == idx
            work = jnp.where(hit, -jnp.inf, work)
            rank = jnp.where(hit, float(a), rank)
            vals_ref[a:a + 1, :] = m
        ranks.append(rank)
    r1, r2 = ranks
    v1 = v1_ref[...]
    v2 = v2_ref[...]
    cand = _candidates(v1, v2, live)
    sel = jnp.zeros(cand.shape, F32)
    work = cand
    for _ in range(PEER_TOPK):
        m = jnp.max(work, axis=0, keepdims=True)
        fi = jnp.min(jnp.where(work == m, flat, 9999), axis=0, keepdims=True)
        hit = flat == fi
        work = jnp.where(hit, -jnp.inf, work)
        sel = jnp.where(hit, 1.0, sel)
    z = jnp.sum(sel * jnp.exp(cand - (v1[0:1, :] + v2[0:1, :])), axis=0, keepdims=True)
    b2_ref[...] = jnp.exp(s2 - v2[0:1, :]) * (0.5 / z)

    zeros8 = jnp.zeros((8, s1.shape[1]), F32)
    cnt_hi = (sel[40:56, :] + jnp.concatenate([sel[56:64, :], zeros8], axis=0)
              + jnp.concatenate([sel[64:72, :], zeros8], axis=0))
    cnt_lo = [jnp.sum(sel[0:16, :], axis=0, keepdims=True)] + [
        jnp.sum(sel[16 + 8 * (a - 1):24 + 8 * (a - 1), :], axis=0, keepdims=True) for a in (1, 2, 3)]
    cnt = jnp.zeros(s1.shape, F32)
    for a in range(PEER_TOPK):
        ca = cnt_lo[a] if a < 4 else cnt_hi[a:a + 1, :]
        cnt = jnp.where(r1 == float(a), ca, cnt)
    th_ref[...] = 0.5 - cnt
    x2_ref[...] = -r2


def _route_kernel(x_ref, gain_ref, shift_ref, scale_ref, wq_ref, sk_ref, flat_ref, live_ref,
                  ht_ref, a1_ref, th_ref, b2_ref, x2_ref, qt_ref, v1_ref, v2_ref):
    tt = x_ref.shape[0]
    h2 = _modulate(x_ref[...], gain_ref[...], shift_ref[0], scale_ref[0])
    ht = h2.T.astype(BF16)
    for g in range(tt // _TG):
        ht_ref[g] = ht[:, g * _TG:(g + 1) * _TG]
    qt_ref[...] = _dot(wq_ref[...], ht)
    row_i = lax.broadcasted_iota(I32, (N_KEYS, tt), 0)
    flat = flat_ref[...]
    live = live_ref[...] > 0.0
    n_dead = float(_CAND_ROWS - int(_cand_layout()[1].sum()))

    group = v1_ref.shape[0]
    for h0 in range(0, PEER_HEADS, group):
        heads = range(h0, h0 + group)
        s1 = [_dot(sk_ref[2 * h], qt_ref[(2 * h) * N_KEYS:(2 * h + 1) * N_KEYS, :].astype(BF16)) for h in heads]
        s2 = [_dot(sk_ref[2 * h + 1], qt_ref[(2 * h + 1) * N_KEYS:(2 * h + 2) * N_KEYS, :].astype(BF16))
              for h in heads]

        work = [list(s1), list(s2)]
        for a in range(PEER_TOPK):
            for half, vals_ref in enumerate((v1_ref, v2_ref)):
                for g in range(group):
                    m = jnp.max(work[half][g], axis=0, keepdims=True)
                    work[half][g] = jnp.where(work[half][g] == m, -jnp.inf, work[half][g])
                    vals_ref[g, a:a + 1, :] = m
        v1 = [v1_ref[g] for g in range(group)]
        v2 = [v2_ref[g] for g in range(group)]
        cand = [_candidates(v1[g], v2[g], live) for g in range(group)]
        wc = list(cand)
        t16 = [None] * group
        for _ in range(PEER_TOPK):
            for g in range(group):
                t16[g] = jnp.max(wc[g], axis=0, keepdims=True)
                wc[g] = jnp.where(wc[g] == t16[g], -jnp.inf, wc[g])
        bad = []
        last = PEER_TOPK - 1
        for g, h in enumerate(heads):
            t17 = jnp.max(wc[g], axis=0, keepdims=True)
            top = v1[g][0:1, :] + v2[g][0:1, :]
            z = jnp.sum(jnp.where(cand[g] >= t16[g], jnp.exp(cand[g] - top), 0.0), axis=0, keepdims=True)
            a1_ref[h] = jnp.exp(s1[g] - v1[g][0:1, :])
            b2_ref[h] = jnp.exp(s2[g] - v2[g][0:1, :]) * (0.5 / z)
            th_ref[h] = jnp.where(s1[g] >= v1[g][last:last + 1, :], 0.5 * (t16[g] + t17) - s1[g], jnp.inf)
            x2_ref[h] = jnp.where(s2[g] >= v2[g][last:last + 1, :], s2[g], -jnp.inf)
            scale = (jnp.abs(v1[g][0:1, :]) + jnp.abs(v1[g][last:last + 1, :])
                     + jnp.abs(v2[g][0:1, :]) + jnp.abs(v2[g][last:last + 1, :]))
            close = jnp.where(t16[g] - t17 > scale * 2.0 ** -22, 0.0, 1.0)
            bad.append(jnp.max(jnp.abs(_gone(work[0][g]) - PEER_TOPK) + jnp.abs(_gone(work[1][g]) - PEER_TOPK)
                               + jnp.abs(_gone(wc[g]) - (PEER_TOPK + n_dead)) + close) > 0.0)

        for g, h in enumerate(heads):
            @pl.when(bad[g])
            def _(g=g, h=h):
                _route_head_exact(s1[g], s2[g], row_i, flat, live, v1_ref.at[g], v2_ref.at[g],
                                  b2_ref.at[h], th_ref.at[h], x2_ref.at[h])


def _route(x2, gain, shift, scale, wq_t, sub_keys, seq):
    t, d = x2.shape
    tt = 256
    per_b = seq // tt
    flat_np, live_np = _cand_layout()
    flat = jnp.asarray(np.broadcast_to(flat_np[:, None], (_CAND_ROWS, tt)).copy())
    live = jnp.asarray(np.broadcast_to(live_np[:, None], (_CAND_ROWS, tt)).astype(np.float32))
    dense = jax.ShapeDtypeStruct((PEER_HEADS, N_KEYS, t), F32)
    dense_spec = pl.BlockSpec((PEER_HEADS, N_KEYS, tt), lambda i: (0, 0, i))
    const2 = lambda i: (0, 0)
    return pl.pallas_call(
        _route_kernel,
        grid=(t // tt,),
        in_specs=[
            pl.BlockSpec((tt, d), lambda i: (i, 0)),
            pl.BlockSpec((1, d), const2),
            pl.BlockSpec((1, 1, d), lambda i: (i // per_b, 0, 0)),
            pl.BlockSpec((1, 1, d), lambda i: (i // per_b, 0, 0)),
            pl.BlockSpec(wq_t.shape, const2),
            pl.BlockSpec(sub_keys.shape, lambda i: (0, 0, 0)),
            pl.BlockSpec((_CAND_ROWS, tt), const2),
            pl.BlockSpec((_CAND_ROWS, tt), const2),
        ],
        out_specs=[pl.BlockSpec((tt // _TG, d, _TG), lambda i: (i, 0, 0)),
                   dense_spec, dense_spec, dense_spec, dense_spec],
        out_shape=[jax.ShapeDtypeStruct((t // _TG, d, _TG), BF16), dense, dense, dense, dense],
        scratch_shapes=[
            pltpu.VMEM((PEER_HEADS * 2 * N_KEYS, tt), F32),
            pltpu.VMEM((_ROUTE_GROUP, PEER_TOPK, tt), F32),
            pltpu.VMEM((_ROUTE_GROUP, PEER_TOPK, tt), F32),
        ],
        compiler_params=_params("parallel"),
        name="peer_route",
    )(x2, gain, shift, scale, wq_t, sub_keys, flat, live)


_JB = 4
_IB = 4
_TG = 256
_RB = _IB * N_KEYS


def _peer_kernel(n_tiles, ht_ref, u_ref, vt_ref, a1_ref, th_ref, b2_ref, x2_ref, x_ref, gf_ref, o_ref,
                 acc_ref, at_ref, gt_ref, bc_ref):
    s = pl.program_id(1)
    n_g, te, tg = at_ref.shape
    tt = n_g * tg
    n_i = te // N_KEYS
    n_jb = N_KEYS // (SUBLANES * _JB)
    assert n_i == SUBLANES
    assert _RB == _IB * N_KEYS

    def bcast_rows():
        i0 = pl.multiple_of(s * n_i, SUBLANES)
        for h in range(PEER_HEADS):
            a1g = a1_ref[h, pl.ds(i0, n_i), :]
            thg = th_ref[h, pl.ds(i0, n_i), :]
            for ii in range(n_i):
                bc_ref[2 * (h * n_i + ii)] = jnp.broadcast_to(a1g[ii:ii + 1, :], (SUBLANES, tt))
                bc_ref[2 * (h * n_i + ii) + 1] = jnp.broadcast_to(thg[ii:ii + 1, :], (SUBLANES, tt))

    def mask_block(tb, jb, ib):
        g, lane0 = divmod(tb * LANES, tg)
        lanes = slice(tb * LANES, (tb + 1) * LANES)
        lanes_g = slice(lane0, lane0 + LANES)
        j0 = jb * (SUBLANES * _JB)
        w = [[None] * _JB for _ in range(_IB)]
        for h in range(PEER_HEADS):
            x2 = [x2_ref[h, j0 + SUBLANES * k:j0 + SUBLANES * (k + 1), lanes] for k in range(_JB)]
            b2 = [b2_ref[h, j0 + SUBLANES * k:j0 + SUBLANES * (k + 1), lanes] for k in range(_JB)]
            for di in range(_IB):
                row = 2 * (h * n_i + ib * _IB + di)
                a1v = bc_ref[row, :, lanes]
                thv = bc_ref[row + 1, :, lanes]
                for k in range(_JB):
                    term = jnp.where(x2[k] >= thv, a1v * b2[k], 0.0)
                    w[di][k] = term if w[di][k] is None else w[di][k] + term
        for di in range(_IB):
            r0 = (ib * _IB + di) * N_KEYS + j0
            rows = slice(r0, r0 + SUBLANES * _JB)
            a = at_ref[g, rows, lanes_g]
            act2 = a + a * lax.erf(a * np.float32(np.sqrt(0.5)))
            gt_ref[g, rows, lanes_g] = (jnp.concatenate(w[di], axis=0) * act2).astype(BF16)

    @pl.when(s == 0)
    def _():
        acc_ref[...] = jnp.zeros_like(acc_ref)

    def pre_act(k):
        rows = slice(k * _RB, (k + 1) * _RB)
        for g in range(n_g):
            at_ref[g, rows, :] = _dot(u_ref[rows, :], ht_ref[g])

    def fold(k):
        rows = slice(k * _RB, (k + 1) * _RB)
        for g in range(n_g):
            acc_ref[g] += _dot(vt_ref[:, rows], gt_ref[g, rows, :])

    bcast_rows()
    n_slices = te // _RB
    pre_act(0)
    for k in range(n_slices):
        if k + 1 < n_slices:
            pre_act(k + 1)
        if k >= 1:
            fold(k - 1)
        for tb in range(tt // LANES):
            for jb in range(n_jb):
                mask_block(tb, jb, k)
    fold(n_slices - 1)

    @pl.when(s == n_tiles - 1)
    def _():
        for g in range(n_g):
            rows = slice(g * tg, (g + 1) * tg)
            o_ref[rows, :] = x_ref[rows, :] + gf_ref[0] * acc_ref[g].T


def _peer(ht, u, vt, a1, th, b2, xk, x2, gate_f, seq):
    t, d = x2.shape
    n_tiles = u.shape[0] // PEER_TE
    tt = 512
    per_b = seq // tt
    f32_spec = pl.BlockSpec((PEER_HEADS, N_KEYS, tt), lambda i, s: (0, 0, i))
    return pl.pallas_call(
        functools.partial(_peer_kernel, n_tiles),
        grid=(t // tt, n_tiles),
        in_specs=[
            pl.BlockSpec((tt // _TG, d, _TG), lambda i, s: (i, 0, 0)),
            pl.BlockSpec((PEER_TE, d), lambda i, s: (s, 0)),
            pl.BlockSpec((d, PEER_TE), lambda i, s: (0, s)),
            f32_spec, f32_spec, f32_spec, f32_spec,
            pl.BlockSpec((tt, d), lambda i, s: (i, 0)),
            pl.BlockSpec((1, 1, d), lambda i, s: (i // per_b, 0, 0)),
        ],
        out_specs=pl.BlockSpec((tt, d), lambda i, s: (i, 0)),
        out_shape=jax.ShapeDtypeStruct((t, d), F32),
        scratch_shapes=[
            pltpu.VMEM((tt // _TG, d, _TG), F32),
            pltpu.VMEM((tt // _TG, PEER_TE, _TG), F32),
            pltpu.VMEM((tt // _TG, PEER_TE, _TG), BF16),
            pltpu.VMEM((2 * PEER_HEADS * PEER_TE // N_KEYS, SUBLANES, tt), F32),
        ],
        compiler_params=_params("parallel", "arbitrary"),
        name="peer_experts",
    )(ht, u, vt, a1, th, b2, xk, x2, gate_f)


def _tcast_kernel(x_ref, o_ref):
    o_ref[...] = x_ref[...].T.astype(o_ref.dtype)


def _cast_kernel(x_ref, o_ref):
    o_ref[...] = x_ref[...].astype(o_ref.dtype)


def _layer_cast(x, layer, dtype, transpose):
    _, r, c = x.shape
    tr, tc = min(r, 1024), min(c, 1024)
    return pl.pallas_call(
        _tcast_kernel if transpose else _cast_kernel,
        grid=(r // tr, c // tc),
        in_specs=[pl.BlockSpec((None, tr, tc), lambda i, j: (layer, i, j))],
        out_specs=pl.BlockSpec((tc, tr), lambda i, j: (j, i)) if transpose else pl.BlockSpec((tr, tc), lambda i, j: (i, j)),
        out_shape=jax.ShapeDtypeStruct((c, r) if transpose else (r, c), dtype),
        compiler_params=_params("parallel", "parallel"),
        name="layer_cast",
    )(x)


def _pad_heads(w, width):
    d = w.shape[0]
    w = w.reshape(d, HEADS, width)
    return jnp.pad(w, ((0, 0), (0, 0), (0, HEAD_DIM - width))).reshape(d, HEADS * HEAD_DIM)


def _pack_segments():
    segs = [(0, W_A + W_B, 0)]
    n_idx = IDX_HEADS * IDX_DIM + IDX_DIM + IDX_HEADS
    src = W_A + W_B
    dst = W_A + W_B
    segs.append((src, n_idx, dst)); src += n_idx; dst += W_I
    for _ in range(2):
        for h in range(HEADS):
            segs.append((src + GLA_DK * h, GLA_DK, dst + HEAD_DIM * h))
        src += HEADS * GLA_DK; dst += BRANCH_WIDTH
    segs.append((src, BRANCH_WIDTH, dst)); src += BRANCH_WIDTH; dst += BRANCH_WIDTH
    code_src = src; src += GLA_GATE_RANK
    segs.append((src, BRANCH_WIDTH, dst)); src += BRANCH_WIDTH; dst += BRANCH_WIDTH
    segs.append((code_src, GLA_GATE_RANK, dst)); dst += LANES
    segs.append((src, W_M, dst))
    assert dst + W_M == W_PACK
    return segs


def _pack_kernel(w_ref, o_ref):
    o_ref[...] = jnp.zeros_like(o_ref)
    for src, width, dst in _pack_segments():
        o_ref[0, :, dst:dst + width] = w_ref[0, :, src:src + width].astype(o_ref.dtype)


def _pack_w_in(w_in):
    depth, d, n_in = w_in.shape
    tr = 128
    return pl.pallas_call(
        _pack_kernel,
        grid=(depth, d // tr),
        in_specs=[pl.BlockSpec((1, tr, n_in), lambda l, i: (l, i, 0))],
        out_specs=pl.BlockSpec((1, tr, W_PACK), lambda l, i: (l, i, 0)),
        out_shape=jax.ShapeDtypeStruct((depth, d, W_PACK), BF16),
        compiler_params=_params("parallel", "parallel"),
        name="pack_w_in",
    )(w_in)


def kernel(x, c, w_ada, b_ada, norm_mix, norm_ffn, w_in, hgrn_lb_logits, hgrn_out_norm, dsa_q_norm, dsa_k_norm,
           gla_gate_up, gla_gate_bias, gla_out_norm, w_branch, w_out, peer_w_query, peer_sub_keys, peer_u, peer_v):
    bsz, seq, d = x.shape
    depth = w_in.shape[0]
    t = bsz * seq
    x2 = x.reshape(t, d)
    mod = _ada(c, w_ada, b_ada)
    w_pack = _pack_w_in(w_in)

    for l in range(depth):
        shift_m, scale_m, gate_m, shift_f, scale_f, gate_f = [
            mod[l, :, k * d:(k + 1) * d].reshape(bsz, 1, d) for k in range(6)]
        za, zb, zi, zc, zm = _inproj(x2, norm_mix[l].reshape(1, d), shift_m, scale_m, w_pack, l, seq)

        ya = _scan("hgrn", l, za, (hgrn_lb_logits,), hgrn_out_norm[l].reshape(1, HEAD_DIM), bsz, seq)
        yb = _dsa(zb, zi, dsa_q_norm[l].reshape(1, HEAD_DIM), dsa_k_norm[l].reshape(1, HEAD_DIM), bsz, seq)
        gup = jnp.pad(_pad_heads(gla_gate_up[l], GLA_DK), ((0, LANES - GLA_GATE_RANK), (0, 0)))
        gb = _pad_heads(gla_gate_bias[l].reshape(1, HEADS * GLA_DK), GLA_DK)
        yc = _scan("gla", l, zc, (gup, gb), gla_out_norm[l].reshape(1, HEAD_DIM), bsz, seq)

        x2 = _merge(ya, yb, yc, zm, x2, gate_m, w_branch[l].astype(BF16), w_out[l].astype(BF16), seq)

        ht, a1, th, b2, xk = _route(
            x2, norm_ffn[l].reshape(1, d), shift_f, scale_f, _layer_cast(peer_w_query, l, BF16, True),
            peer_sub_keys[l].reshape(PEER_HEADS * 2, N_KEYS, -1).astype(BF16), seq)
        x2 = _peer(ht, _layer_cast(peer_u, l, BF16, False), _layer_cast(peer_v, l, BF16, True),
                   a1, th, b2, xk, x2, gate_f, seq)

    return x2.reshape(bsz, seq, d)
```

```python
import functools

import numpy as np
import jax
import jax.numpy as jnp
from jax import lax
from jax.experimental import pallas as pl
from jax.experimental.pallas import tpu as pltpu

F32, BF16, I32 = jnp.float32, jnp.bfloat16, jnp.int32

D_MODEL = 1024
HEADS = 4
HEAD_DIM = 128
BRANCH_WIDTH = HEADS * HEAD_DIM
IDX_HEADS = 4
IDX_DIM = 64
DSA_MAX_TOPK = 256
QBLOCK = 256
GLA_DK = 64
GLA_GATE_RANK = 16
GLA_TAU = 16.0
PEER_HEADS = 8
N_KEYS = 128
PEER_TOPK = 16
PEER_TE = 1024
EPS = 1e-6

LANES = 128
SUBLANES = 8
VMEM_LIMIT_BYTES = 56 * 1024 * 1024

NEG_BIG = -1e30
INT_MIN = np.int32(-2 ** 31)

W_A = 4 * BRANCH_WIDTH
W_B = 3 * BRANCH_WIDTH
W_I = 3 * LANES
W_C = 4 * BRANCH_WIDTH + LANES
W_M = 3 * D_MODEL
W_PACK = W_A + W_B + W_I + W_C + W_M


def _dot(a, b):
    return jnp.dot(a, b, preferred_element_type=F32)


def _dot_nt(a, b):
    return lax.dot_general(a, b, (((1,), (1,)), ((), ())), preferred_element_type=F32)


def _dot_tn(a, b):
    return lax.dot_general(a, b, (((0,), (0,)), ((), ())), preferred_element_type=F32)


def _split2(x):
    hi = x.astype(BF16)
    lo = (x - hi.astype(F32)).astype(BF16)
    return hi, lo


def _split3(x):
    hi = x.astype(BF16)
    r = x - hi.astype(F32)
    mid = r.astype(BF16)
    lo = (r - mid.astype(F32)).astype(BF16)
    return hi, mid, lo


def _params(*sem):
    return pltpu.CompilerParams(dimension_semantics=sem, vmem_limit_bytes=VMEM_LIMIT_BYTES)


def _modulate(x, gain, shift, scale):
    ms = jnp.mean(x * x, axis=-1, keepdims=True)
    return x * lax.rsqrt(ms + EPS) * gain * (1.0 + scale) + shift


def _ada_kernel(c_ref, w_ref, b_ref, o_ref):
    c = c_ref[...]
    sc = c * jax.nn.sigmoid(c)
    a_hi, a_lo = _split2(sc)
    w_hi, w_lo = _split2(w_ref[0])
    o_ref[0] = _dot(a_hi, w_hi) + _dot(a_hi, w_lo) + _dot(a_lo, w_hi) + b_ref[0]


def _ada(c, w_ada, b_ada):
    depth, d, n = w_ada.shape
    bsz = c.shape[0]
    tn = 1536
    return pl.pallas_call(
        _ada_kernel,
        grid=(depth, n // tn),
        in_specs=[
            pl.BlockSpec((bsz, d), lambda l, j: (0, 0)),
            pl.BlockSpec((1, d, tn), lambda l, j: (l, 0, j)),
            pl.BlockSpec((1, 1, tn), lambda l, j: (l, 0, j)),
        ],
        out_specs=pl.BlockSpec((1, bsz, tn), lambda l, j: (l, 0, j)),
        out_shape=jax.ShapeDtypeStruct((depth, bsz, n), F32),
        compiler_params=_params("parallel", "parallel"),
        name="ada",
    )(c, w_ada, b_ada.reshape(depth, 1, n))


def _inproj_kernel(x_ref, gain_ref, shift_ref, scale_ref, w_ref, *out_refs):
    h = _modulate(x_ref[...], gain_ref[...], shift_ref[0], scale_ref[0]).astype(BF16)
    off = 0
    for o_ref in out_refs:
        width = o_ref.shape[1]
        for c0 in range(0, width, 512):
            c1 = min(c0 + 512, width)
            o_ref[:, c0:c1] = _dot(h, w_ref[:, off + c0:off + c1]).astype(o_ref.dtype)
        off += width


def _inproj(x2, gain, shift, scale, w_pack, layer, seq):
    t, d = x2.shape
    tm = 256
    per_b = seq // tm
    widths = (W_A, W_B, W_I, W_C, W_M)
    dtypes = (F32, BF16, F32, F32, BF16)
    return pl.pallas_call(
        _inproj_kernel,
        grid=(t // tm,),
        in_specs=[
            pl.BlockSpec((tm, d), lambda i: (i, 0)),
            pl.BlockSpec((1, d), lambda i: (0, 0)),
            pl.BlockSpec((1, 1, d), lambda i: (i // per_b, 0, 0)),
            pl.BlockSpec((1, 1, d), lambda i: (i // per_b, 0, 0)),
            pl.BlockSpec((None, d, W_PACK), lambda i: (layer, 0, 0), pipeline_mode=pl.Buffered(1)),
        ],
        out_specs=[pl.BlockSpec((tm, w), lambda i: (i, 0)) for w in widths],
        out_shape=[jax.ShapeDtypeStruct((t, w), dt) for w, dt in zip(widths, dtypes)],
        compiler_params=_params("parallel"),
        name="inproj",
    )(x2, gain, shift, scale, w_pack)


def _group_row_bcast(x, group, row):
    n, w = x.shape
    if group >= SUBLANES:
        x3 = x.reshape(n // group, group, w)
        return jnp.broadcast_to(x3[:, row:row + 1, :], x3.shape).reshape(n, w)
    x3 = x.reshape(n // SUBLANES, SUBLANES, w)
    sub = lax.broadcasted_iota(I32, x3.shape, 1)
    n_groups = SUBLANES // group
    res = None
    for g in reversed(range(n_groups)):
        r = g * group + row
        bc = jnp.broadcast_to(x3[:, r:r + 1, :], x3.shape)
        res = bc if res is None else jnp.where(sub < (g + 1) * group, bc, res)
    return res.reshape(n, w)


def _scan_kernel(mode, layer, *refs):
    if mode == "hgrn":
        q_ref, f_ref, v_ref, g_ref, lbl_ref, gain_ref, lv_ref, tri_ref, y_ref, st_ref = refs
    else:
        q_ref, k_ref, v_ref, g_ref, code_ref, gup_ref, gb_ref, gain_ref, lv_ref, tri_ref, y_ref, st_ref = refs
    lt = q_ref.shape[0]
    n_levels = lt.bit_length() - 1

    @pl.when(pl.program_id(1) == 0)
    def _():
        st_ref[...] = jnp.zeros_like(st_ref)

    tri = tri_ref[...]
    lv = lv_ref[...]
    for h in range(HEADS):
        cols = slice(h * HEAD_DIM, (h + 1) * HEAD_DIM)
        if mode == "hgrn":
            lbl = lbl_ref[:, cols]
            e = jnp.exp(lbl - jnp.max(lbl, axis=0, keepdims=True))
            p = e / jnp.sum(e, axis=0, keepdims=True)
            lb = jnp.zeros((1, p.shape[1]), F32)
            for l2 in range(1, layer + 1):
                lb = lb + p[l2:l2 + 1, :]
            f = lb + (1.0 - lb) * jax.nn.sigmoid(f_ref[:, cols])
            lg = jnp.log(f)
            kk = 1.0 - f
            q = q_ref[:, cols] * (HEAD_DIM ** -0.5)
        else:
            c_hi, c_lo = _split2(code_ref[...])
            u_hi, u_lo = _split2(gup_ref[:, cols])
            z = _dot(c_hi, u_hi) + _dot(c_hi, u_lo) + _dot(c_lo, u_hi) + gb_ref[:, cols]
            lg = (jnp.minimum(z, 0.0) - jnp.log1p(jnp.exp(-jnp.abs(z)))) * (1.0 / GLA_TAU)
            kk = k_ref[:, cols]
            q = q_ref[:, cols] * (GLA_DK ** -0.5)

        g_hi, g_mid, g_lo = _split3(lg)
        b = _dot(tri, g_hi) + _dot(tri, g_mid) + _dot(tri, g_lo)

        s = jnp.where(lv == -1, _dot_nt(q.astype(BF16), kk.astype(BF16)), 0.0)
        for l in range(n_levels):
            n = 1 << l
            bref = _group_row_bcast(b, 2 * n, n - 1)
            ql = (q * jnp.exp(jnp.minimum(b - bref, 0.0))).astype(BF16)
            kl = (kk * jnp.exp(jnp.minimum(bref - b, 0.0))).astype(BF16)
            s = jnp.where(lv == l, _dot_nt(ql, kl), s)

        vb = v_ref[:, cols].astype(BF16)
        st = st_ref[h]
        o = _dot(s.astype(BF16), vb) + _dot_nt((q * jnp.exp(b)).astype(BF16), st.astype(BF16))
        b_last = b[lt - 1:lt, :]
        kd = (kk * jnp.exp(b_last - b)).astype(BF16)
        st_ref[h] = st * jnp.exp(b_last) + _dot_tn(vb, kd)

        ms = jnp.mean(o * o, axis=-1, keepdims=True)
        g = g_ref[:, cols]
        y = o * lax.rsqrt(ms + EPS) * gain_ref[...] * (g * jax.nn.sigmoid(g))
        y_ref[:, cols] = y.astype(y_ref.dtype)


def _scan_consts(lt):
    idx = np.arange(lt)
    x = idx[:, None] ^ idx[None, :]
    lvl = np.floor(np.log2(np.maximum(x, 1))).astype(np.int32)
    lv = np.where(idx[None, :] < idx[:, None], lvl, np.where(x == 0, -1, -2)).astype(np.int32)
    tri = (idx[None, :] <= idx[:, None]).astype(np.float32)
    return jnp.asarray(lv), jnp.asarray(tri, dtype=BF16)


def _scan(mode, layer, z, extra, out_gain, bsz, seq):
    t = z.shape[0]
    lt = 256
    per_b = seq // lt
    lv, tri = _scan_consts(lt)

    def col(cb):
        return pl.BlockSpec((lt, BRANCH_WIDTH), lambda b, c: (b * per_b + c, cb))

    const2 = lambda b, c: (0, 0)
    in_specs = [col(0), col(1), col(2), col(3)]
    args = [z, z, z, z]
    if mode == "hgrn":
        (lb_logits,) = extra
        in_specs.append(pl.BlockSpec(lb_logits.shape, const2))
        args.append(lb_logits)
    else:
        gup, gb = extra
        in_specs += [pl.BlockSpec((lt, LANES), lambda b, c: (b * per_b + c, 4 * HEADS)),
                     pl.BlockSpec(gup.shape, const2), pl.BlockSpec(gb.shape, const2)]
        args += [z, gup, gb]
    in_specs += [pl.BlockSpec((1, HEAD_DIM), const2), pl.BlockSpec((lt, lt), const2), pl.BlockSpec((lt, lt), const2)]
    args += [out_gain, lv, tri]
    return pl.pallas_call(
        functools.partial(_scan_kernel, mode, layer),
        grid=(bsz, per_b),
        in_specs=in_specs,
        out_specs=pl.BlockSpec((lt, BRANCH_WIDTH), lambda b, c: (b * per_b + c, 0)),
        out_shape=jax.ShapeDtypeStruct((t, BRANCH_WIDTH), BF16),
        scratch_shapes=[pltpu.VMEM((HEADS, HEAD_DIM, HEAD_DIM), F32)],
        compiler_params=_params("parallel", "arbitrary"),
        name="scan_" + mode,
    )(*args)


_KCH = 256


def _dsa_kernel(topk, q_ref, k_ref, v_ref, ziq_ref, zik_ref, qg_ref, kg_ref, stri_ref, o_ref,
                kn_ref, vt_ref, kih_ref, kil_ref, keys_ref, bias_ref, acc_ref):
    j = pl.program_id(1)
    seq = k_ref.shape[0]

    @pl.when(j == 0)
    def _prep():
        def body(c, carry):
            r0 = pl.multiple_of(c * _KCH, _KCH)
            kc = k_ref[pl.ds(r0, _KCH), :].astype(F32)
            for h in range(HEADS):
                kh = kc[:, h * HEAD_DIM:(h + 1) * HEAD_DIM]
                ms = jnp.mean(kh * kh, axis=-1, keepdims=True)
                kn_ref[pl.ds(r0, _KCH), h * HEAD_DIM:(h + 1) * HEAD_DIM] = (
                    kh * lax.rsqrt(ms + EPS) * kg_ref[...]).astype(BF16)
            vt_ref[c] = v_ref[pl.ds(r0, _KCH), :].astype(F32).T.astype(BF16)
            ki = zik_ref[pl.ds(r0, _KCH), :]
            hi = ki.astype(BF16)
            kih_ref[pl.ds(r0, _KCH), :] = hi
            kil_ref[pl.ds(r0, _KCH), :] = (ki - hi.astype(F32)).astype(BF16)
            return carry
        lax.fori_loop(0, seq // _KCH, body, 0)

    n_ch = (j * QBLOCK + QBLOCK + _KCH - 1) // _KCH
    ziq = ziq_ref[...]
    w_t = ziq[:, 2 * LANES:3 * LANES].T
    qi = jnp.concatenate([ziq[:, h * IDX_DIM:(h + 1) * IDX_DIM] for h in range(IDX_HEADS)], axis=0)
    qi_hi, qi_lo = _split2(qi * (IDX_DIM ** -0.5))
    w_rows = [w_t[IDX_DIM + h:IDX_DIM + h + 1, :] * (IDX_HEADS ** -0.5) for h in range(IDX_HEADS)]
    q_pos = j * QBLOCK + lax.broadcasted_iota(I32, (_KCH, QBLOCK), 1)
    row_i = lax.broadcasted_iota(I32, (_KCH, QBLOCK), 0)

    def idx_body(c, carry):
        r0 = pl.multiple_of(c * _KCH, _KCH)
        k_hi = kih_ref[pl.ds(r0, _KCH), :][:, :IDX_DIM]
        k_lo = kil_ref[pl.ds(r0, _KCH), :][:, :IDX_DIM]
        logit = _dot_nt(k_hi, qi_hi) + _dot_nt(k_hi, qi_lo) + _dot_nt(k_lo, qi_hi)
        score = jnp.zeros((_KCH, QBLOCK), F32)
        for h in range(IDX_HEADS):
            score = score + w_rows[h] * jnp.maximum(logit[:, h * QBLOCK:(h + 1) * QBLOCK], 0.0)
        score = jnp.where(score == 0.0, 0.0, score)
        bits = lax.bitcast_convert_type(score, I32)
        key = jnp.where(bits < 0, bits ^ np.int32(0x7FFFFFFF), bits)
        keys_ref[pl.ds(r0, _KCH), :] = jnp.where(r0 + row_i <= q_pos, key, INT_MIN)
        return carry
    lax.fori_loop(0, n_ch, idx_body, 0)

    def count(pred):
        def body(c, acc):
            r0 = pl.multiple_of(c * _KCH, _KCH)
            m = jnp.where(pred(keys_ref[pl.ds(r0, _KCH), :]), 1, 0).astype(I32)
            return acc + jnp.sum(m.reshape(_KCH // SUBLANES, SUBLANES, QBLOCK), axis=0)
        acc = lax.fori_loop(0, n_ch, body, jnp.zeros((SUBLANES, QBLOCK), I32))
        return jnp.sum(acc, axis=0, keepdims=True)

    def bisect(it, thr):
        cand = thr ^ jnp.left_shift(jnp.int32(1), 31 - it)
        return jnp.where(count(lambda x: x >= cand) >= topk, cand, thr)
    thr = lax.fori_loop(0, 32, bisect, jnp.full((1, QBLOCK), INT_MIN, I32))
    need = (topk - count(lambda x: x > thr)).astype(F32)

    def sel_body(c, seen):
        r0 = pl.multiple_of(c * _KCH, _KCH)
        x = keys_ref[pl.ds(r0, _KCH), :]
        eq = jnp.where(x == thr, 1.0, 0.0)
        rank = seen + _dot(stri_ref[...], eq.astype(BF16))
        tie_ok = jnp.where(x == thr, jnp.where(rank < need, 0.0, NEG_BIG), NEG_BIG)
        bias = jnp.where(x > thr, 0.0, tie_ok)
        bias_ref[pl.ds(r0, _KCH), :] = jnp.where(x == INT_MIN, NEG_BIG, bias)
        return seen + jnp.sum(eq, axis=0, keepdims=True)
    lax.fori_loop(0, n_ch, sel_body, jnp.zeros((1, QBLOCK), F32))

    qf = q_ref[...].astype(F32)
    qn = []
    for h in range(HEADS):
        qh = qf[:, h * HEAD_DIM:(h + 1) * HEAD_DIM]
        ms = jnp.mean(qh * qh, axis=-1, keepdims=True)
        qn.append((qh * lax.rsqrt(ms + EPS) * qg_ref[...] * (HEAD_DIM ** -0.5)).astype(BF16))
    acc_ref[...] = jnp.zeros_like(acc_ref)

    @pl.when(n_ch % 2 == 1)
    def _():
        bias_ref[pl.ds(pl.multiple_of(n_ch * _KCH, _KCH), _KCH), :] = jnp.full((_KCH, QBLOCK), NEG_BIG, F32)

    def att_body(c, carry):
        r0 = pl.multiple_of(c * (2 * _KCH), 2 * _KCH)
        bias = bias_ref[pl.ds(r0, 2 * _KCH), :]
        out = []
        for h in range(HEADS):
            m, l = carry[h]
            hd = slice(h * HEAD_DIM, (h + 1) * HEAD_DIM)
            sc = _dot_nt(kn_ref[pl.ds(r0, 2 * _KCH), hd], qn[h]) + bias
            m_new = jnp.maximum(m, jnp.max(sc, axis=0, keepdims=True))
            p = jnp.exp(sc - m_new)
            alpha = jnp.exp(m - m_new)
            pb = p.astype(BF16)
            pv = _dot(vt_ref[2 * c, hd, :], pb[:_KCH]) + _dot(vt_ref[2 * c + 1, hd, :], pb[_KCH:])
            acc_ref[h] = alpha * acc_ref[h] + pv
            out.append((m_new, alpha * l + jnp.sum(p, axis=0, keepdims=True)))
        return tuple(out)
    init = tuple((jnp.full((1, QBLOCK), NEG_BIG, F32), jnp.zeros((1, QBLOCK), F32)) for _ in range(HEADS))
    stats = lax.fori_loop(0, (n_ch + 1) // 2, att_body, init)
    for h in range(HEADS):
        o_ref[:, h * HEAD_DIM:(h + 1) * HEAD_DIM] = (acc_ref[h] / stats[h][1]).T.astype(o_ref.dtype)


def _dsa(zb, zi, q_gain, k_gain, bsz, seq):
    t = zb.shape[0]
    topk = min(DSA_MAX_TOPK, seq // 4)
    nq = seq // QBLOCK
    assert seq % (2 * _KCH) == 0
    idx = np.arange(_KCH)
    stri = jnp.asarray((idx[None, :] < idx[:, None]).astype(np.float32), dtype=BF16)
    const2 = lambda b, j: (0, 0)
    return pl.pallas_call(
        functools.partial(_dsa_kernel, topk),
        grid=(bsz, nq),
        in_specs=[
            pl.BlockSpec((QBLOCK, BRANCH_WIDTH), lambda b, j: (b * nq + j, 0)),
            pl.BlockSpec((seq, BRANCH_WIDTH), lambda b, j: (b, 1)),
            pl.BlockSpec((seq, BRANCH_WIDTH), lambda b, j: (b, 2)),
            pl.BlockSpec((QBLOCK, W_I), lambda b, j: (b * nq + j, 0)),
            pl.BlockSpec((seq, LANES), lambda b, j: (b, 2)),
            pl.BlockSpec((1, HEAD_DIM), const2),
            pl.BlockSpec((1, HEAD_DIM), const2),
            pl.BlockSpec((_KCH, _KCH), const2),
        ],
        out_specs=pl.BlockSpec((QBLOCK, BRANCH_WIDTH), lambda b, j: (b * nq + j, 0)),
        out_shape=jax.ShapeDtypeStruct((t, BRANCH_WIDTH), BF16),
        scratch_shapes=[
            pltpu.VMEM((seq, BRANCH_WIDTH), BF16),
            pltpu.VMEM((seq // _KCH, BRANCH_WIDTH, _KCH), BF16),
            pltpu.VMEM((seq, LANES), BF16),
            pltpu.VMEM((seq, LANES), BF16),
            pltpu.VMEM((seq, QBLOCK), I32),
            pltpu.VMEM((seq, QBLOCK), F32),
            pltpu.VMEM((HEADS, HEAD_DIM, QBLOCK), F32),
        ],
        compiler_params=_params("parallel", "arbitrary"),
        name="dsa",
    )(zb, zb, zb, zi, zi, q_gain, k_gain, stri)


def _merge_kernel(ya_ref, yb_ref, yc_ref, zm_ref, x_ref, gm_ref, wb_ref, wo_ref, o_ref):
    d = x_ref.shape[1]
    mixed = None
    for g, y_ref in enumerate((ya_ref, yb_ref, yc_ref)):
        up = _dot(y_ref[...], wb_ref[g])
        term = jax.nn.sigmoid(zm_ref[:, g * d:(g + 1) * d].astype(F32)) * up
        mixed = term if mixed is None else mixed + term
    o_ref[...] = x_ref[...] + gm_ref[0] * _dot(mixed.astype(BF16), wo_ref[...])


def _merge(ya, yb, yc, zm, x2, gate_m, w_branch, w_out, seq):
    t, d = x2.shape
    tm = 256
    per_b = seq // tm
    row = lambda i: (i, 0)
    return pl.pallas_call(
        _merge_kernel,
        grid=(t // tm,),
        in_specs=[
            pl.BlockSpec((tm, BRANCH_WIDTH), row), pl.BlockSpec((tm, BRANCH_WIDTH), row),
            pl.BlockSpec((tm, BRANCH_WIDTH), row), pl.BlockSpec((tm, W_M), row), pl.BlockSpec((tm, d), row),
            pl.BlockSpec((1, 1, d), lambda i: (i // per_b, 0, 0)),
            pl.BlockSpec(w_branch.shape, lambda i: (0, 0, 0)),
            pl.BlockSpec(w_out.shape, lambda i: (0, 0)),
        ],
        out_specs=pl.BlockSpec((tm, d), row),
        out_shape=jax.ShapeDtypeStruct((t, d), F32),
        compiler_params=_params("parallel"),
        name="merge",
    )(ya, yb, yc, zm, x2, gate_m, w_branch, w_out)


_CAND_ROWS = 72
_ROUTE_GROUP = 4


def _cand_layout():
    flat = np.zeros((_CAND_ROWS,), np.int32)
    live = np.zeros((_CAND_ROWS,), bool)
    r = 0
    for b in range(16):
        flat[r], live[r] = b, True
        r += 1
    for a in (1, 2, 3):
        for b in range(8):
            flat[r], live[r] = a * 16 + b, True
            r += 1
    for a in range(16):
        flat[r], live[r] = a * 16, a >= 4
        r += 1
    for a in range(8):
        flat[r], live[r] = a * 16 + 1, a >= 4
        r += 1
    for a in range(8):
        flat[r], live[r] = a * 16 + 2, a == 4
        r += 1
    flat = np.where(live, flat, 1000 + np.arange(_CAND_ROWS))
    return flat.astype(np.int32), live


def _candidates(v1, v2, live):
    cand = jnp.concatenate(
        [v1[0:1, :] + v2] + [v1[a:a + 1, :] + v2[0:8, :] for a in (1, 2, 3)]
        + [v1 + v2[0:1, :], v1[0:8, :] + v2[1:2, :], v1[0:8, :] + v2[2:3, :]], axis=0)
    return jnp.where(live, cand, -jnp.inf)


def _gone(work):
    return jnp.sum(jnp.where(work == -jnp.inf, 1.0, 0.0), axis=0, keepdims=True)


def _store_lane_groups(ref, val):
    for g in range(ref.shape[0]):
        ref[g] = val[:, g * LANES:(g + 1) * LANES]


def _route_head_exact(s1, s2, row_i, flat, live, v1_ref, v2_ref, b2_ref, th_ref, x2_ref):
    ranks = []
    for s, vals_ref in ((s1, v1_ref), (s2, v2_ref)):
        work = s
        rank = jnp.full(s.shape, 127.0, F32)
        for a in range(PEER_TOPK):
            m = jnp.max(work, axis=0, keepdims=True)
            idx = jnp.min(jnp.where(work == m, row_i, N_KEYS), axis=0, keepdims=True)
            hit = row_i == idx
            work = jnp.where(hit, -jnp.inf, work)
            rank = jnp.where(hit, float(a), rank)
            vals_ref[a:a + 1, :] = m
        ranks.append(rank)
    r1, r2 = ranks
    v1 = v1_ref[...]
    v2 = v2_ref[...]
    cand = _candidates(v1, v2, live)
    sel = jnp.zeros(cand.shape, F32)
    work = cand
    for _ in range(PEER_TOPK):
        m = jnp.max(work, axis=0, keepdims=True)
        fi = jnp.min(jnp.where(work == m, flat, 9999), axis=0, keepdims=True)
        hit = flat == fi
        work = jnp.where(hit, -jnp.inf, work)
        sel = jnp.where(hit, 1.0, sel)
    z = jnp.sum(sel * jnp.exp(cand - (v1[0:1, :] + v2[0:1, :])), axis=0, keepdims=True)
    b2_ref[...] = jnp.exp(s2 - v2[0:1, :]) * (0.5 / z)

    zeros8 = jnp.zeros((8, s1.shape[1]), F32)
    cnt_hi = (sel[40:56, :] + jnp.concatenate([sel[56:64, :], zeros8], axis=0)
              + jnp.concatenate([sel[64:72, :], zeros8], axis=0))
    cnt_lo = [jnp.sum(sel[0:16, :], axis=0, keepdims=True)] + [
        jnp.sum(sel[16 + 8 * (a - 1):24 + 8 * (a - 1), :], axis=0, keepdims=True) for a in (1, 2, 3)]
    cnt = jnp.zeros(s1.shape, F32)
    for a in range(PEER_TOPK):
        ca = cnt_lo[a] if a < 4 else cnt_hi[a:a + 1, :]
        cnt = jnp.where(r1 == float(a), ca, cnt)
    _store_lane_groups(th_ref, 0.5 - cnt)
    x2_ref[...] = -r2


def _route_kernel(x_ref, gain_ref, shift_ref, scale_ref, wq_ref, sk_ref, flat_ref, live_ref,
                  ht_ref, a1_ref, th_ref, b2_ref, x2_ref, qt_ref, v1_ref, v2_ref):
    tt = x_ref.shape[0]
    h2 = _modulate(x_ref[...], gain_ref[...], shift_ref[0], scale_ref[0])
    ht = h2.T.astype(BF16)
    for g in range(tt // _TG):
        ht_ref[g] = ht[:, g * _TG:(g + 1) * _TG]
    qt_ref[...] = _dot(wq_ref[...], ht)
    row_i = lax.broadcasted_iota(I32, (N_KEYS, tt), 0)
    flat = flat_ref[...]
    live = live_ref[...] > 0.0
    n_dead = float(_CAND_ROWS - int(_cand_layout()[1].sum()))

    group = v1_ref.shape[0]
    for h0 in range(0, PEER_HEADS, group):
        heads = range(h0, h0 + group)
        s1 = [_dot(sk_ref[2 * h], qt_ref[(2 * h) * N_KEYS:(2 * h + 1) * N_KEYS, :].astype(BF16)) for h in heads]
        s2 = [_dot(sk_ref[2 * h + 1], qt_ref[(2 * h + 1) * N_KEYS:(2 * h + 2) * N_KEYS, :].astype(BF16))
              for h in heads]

        work = [list(s1), list(s2)]
        for a in range(PEER_TOPK):
            for half, vals_ref in enumerate((v1_ref, v2_ref)):
                for g in range(group):
                    m = jnp.max(work[half][g], axis=0, keepdims=True)
                    work[half][g] = jnp.where(work[half][g] == m, -jnp.inf, work[half][g])
                    vals_ref[g, a:a + 1, :] = m
        v1 = [v1_ref[g] for g in range(group)]
        v2 = [v2_ref[g] for g in range(group)]
        cand = [_candidates(v1[g], v2[g], live) for g in range(group)]
        wc = list(cand)
        t16 = [None] * group
        for _ in range(PEER_TOPK):
            for g in range(group):
                t16[g] = jnp.max(wc[g], axis=0, keepdims=True)
                wc[g] = jnp.where(wc[g] == t16[g], -jnp.inf, wc[g])
        bad = []
        last = PEER_TOPK - 1
        for g, h in enumerate(heads):
            t17 = jnp.max(wc[g], axis=0, keepdims=True)
            top = v1[g][0:1, :] + v2[g][0:1, :]
            z = jnp.sum(jnp.where(cand[g] >= t16[g], jnp.exp(cand[g] - top), 0.0), axis=0, keepdims=True)
            _store_lane_groups(a1_ref.at[h], jnp.exp(s1[g] - v1[g][0:1, :]))
            b2_ref[h] = jnp.exp(s2[g] - v2[g][0:1, :]) * (0.5 / z)
            _store_lane_groups(th_ref.at[h], jnp.where(s1[g] >= v1[g][last:last + 1, :],
                                                       0.5 * (t16[g] + t17) - s1[g], jnp.inf))
            x2_ref[h] = jnp.where(s2[g] >= v2[g][last:last + 1, :], s2[g], -jnp.inf)
            scale = (jnp.abs(v1[g][0:1, :]) + jnp.abs(v1[g][last:last + 1, :])
                     + jnp.abs(v2[g][0:1, :]) + jnp.abs(v2[g][last:last + 1, :]))
            close = jnp.where(t16[g] - t17 > scale * 2.0 ** -22, 0.0, 1.0)
            bad.append(jnp.max(jnp.abs(_gone(work[0][g]) - PEER_TOPK) + jnp.abs(_gone(work[1][g]) - PEER_TOPK)
                               + jnp.abs(_gone(wc[g]) - (PEER_TOPK + n_dead)) + close) > 0.0)

        for g, h in enumerate(heads):
            @pl.when(bad[g])
            def _(g=g, h=h):
                _route_head_exact(s1[g], s2[g], row_i, flat, live, v1_ref.at[g], v2_ref.at[g],
                                  b2_ref.at[h], th_ref.at[h], x2_ref.at[h])


def _route(x2, gain, shift, scale, wq_t, sub_keys, seq):
    t, d = x2.shape
    tt = 256
    per_b = seq // tt
    flat_np, live_np = _cand_layout()
    flat = jnp.asarray(np.broadcast_to(flat_np[:, None], (_CAND_ROWS, tt)).copy())
    live = jnp.asarray(np.broadcast_to(live_np[:, None], (_CAND_ROWS, tt)).astype(np.float32))
    dense = jax.ShapeDtypeStruct((PEER_HEADS, N_KEYS, t), F32)
    dense_spec = pl.BlockSpec((PEER_HEADS, N_KEYS, tt), lambda i: (0, 0, i))
    grouped = jax.ShapeDtypeStruct((PEER_HEADS, t // LANES, N_KEYS, LANES), F32)
    grouped_spec = pl.BlockSpec((PEER_HEADS, tt // LANES, N_KEYS, LANES), lambda i: (0, i, 0, 0))
    const2 = lambda i: (0, 0)
    return pl.pallas_call(
        _route_kernel,
        grid=(t // tt,),
        in_specs=[
            pl.BlockSpec((tt, d), lambda i: (i, 0)),
            pl.BlockSpec((1, d), const2),
            pl.BlockSpec((1, 1, d), lambda i: (i // per_b, 0, 0)),
            pl.BlockSpec((1, 1, d), lambda i: (i // per_b, 0, 0)),
            pl.BlockSpec(wq_t.shape, const2),
            pl.BlockSpec(sub_keys.shape, lambda i: (0, 0, 0)),
            pl.BlockSpec((_CAND_ROWS, tt), const2),
            pl.BlockSpec((_CAND_ROWS, tt), const2),
        ],
        out_specs=[pl.BlockSpec((tt // _TG, d, _TG), lambda i: (i, 0, 0)),
                   grouped_spec, grouped_spec, dense_spec, dense_spec],
        out_shape=[jax.ShapeDtypeStruct((t // _TG, d, _TG), BF16), grouped, grouped, dense, dense],
        scratch_shapes=[
            pltpu.VMEM((PEER_HEADS * 2 * N_KEYS, tt), F32),
            pltpu.VMEM((_ROUTE_GROUP, PEER_TOPK, tt), F32),
            pltpu.VMEM((_ROUTE_GROUP, PEER_TOPK, tt), F32),
        ],
        compiler_params=_params("parallel"),
        name="peer_route",
    )(x2, gain, shift, scale, wq_t, sub_keys, flat, live)


_JB = 4
_IB = 4
_TG = 256
_RB = _IB * N_KEYS


def _peer_kernel(n_tiles, ht_ref, u_ref, vt_ref, a1_ref, th_ref, b2_ref, x2_ref, x_ref, gf_ref, o_ref,
                 acc_ref, at_ref, gt_ref):
    s = pl.program_id(1)
    n_g, te, tg = at_ref.shape
    tt = n_g * tg
    n_i = te // N_KEYS
    n_jb = N_KEYS // (SUBLANES * _JB)
    assert _RB == _IB * N_KEYS

    def mask_block(tb, jb, ib):
        g, lane0 = divmod(tb * LANES, tg)
        lanes = slice(tb * LANES, (tb + 1) * LANES)
        lanes_g = slice(lane0, lane0 + LANES)
        j0 = jb * (SUBLANES * _JB)
        w = [[None] * _JB for _ in range(_IB)]
        for h in range(PEER_HEADS):
            x2 = [x2_ref[h, j0 + SUBLANES * k:j0 + SUBLANES * (k + 1), lanes] for k in range(_JB)]
            b2 = [b2_ref[h, j0 + SUBLANES * k:j0 + SUBLANES * (k + 1), lanes] for k in range(_JB)]
            for di in range(_IB):
                i_row = pl.ds(s * n_i + ib * _IB + di, 1)
                a1v = jnp.broadcast_to(a1_ref[h, tb, i_row, :], (SUBLANES, LANES))
                thv = jnp.broadcast_to(th_ref[h, tb, i_row, :], (SUBLANES, LANES))
                for k in range(_JB):
                    term = jnp.where(x2[k] >= thv, a1v * b2[k], 0.0)
                    w[di][k] = term if w[di][k] is None else w[di][k] + term
        for di in range(_IB):
            r0 = (ib * _IB + di) * N_KEYS + j0
            rows = slice(r0, r0 + SUBLANES * _JB)
            a = at_ref[g, rows, lanes_g]
            act2 = a + a * lax.erf(a * np.float32(np.sqrt(0.5)))
            gt_ref[g, rows, lanes_g] = (jnp.concatenate(w[di], axis=0) * act2).astype(BF16)

    @pl.when(s == 0)
    def _():
        acc_ref[...] = jnp.zeros_like(acc_ref)

    def pre_act(k):
        rows = slice(k * _RB, (k + 1) * _RB)
        for g in range(n_g):
            at_ref[g, rows, :] = _dot(u_ref[rows, :], ht_ref[g])

    def fold(k):
        rows = slice(k * _RB, (k + 1) * _RB)
        for g in range(n_g):
            acc_ref[g] += _dot(vt_ref[:, rows], gt_ref[g, rows, :])

    n_slices = te // _RB
    pre_act(0)
    for k in range(n_slices):
        if k + 1 < n_slices:
            pre_act(k + 1)
        if k >= 1:
            fold(k - 1)
        for tb in range(tt // LANES):
            for jb in range(n_jb):
                mask_block(tb, jb, k)
    fold(n_slices - 1)

    @pl.when(s == n_tiles - 1)
    def _():
        for g in range(n_g):
            rows = slice(g * tg, (g + 1) * tg)
            o_ref[rows, :] = x_ref[rows, :] + gf_ref[0] * acc_ref[g].T


def _peer(ht, u, vt, a1, th, b2, xk, x2, gate_f, seq):
    t, d = x2.shape
    n_tiles = u.shape[0] // PEER_TE
    tt = 512
    per_b = seq // tt
    dense_spec = pl.BlockSpec((PEER_HEADS, N_KEYS, tt), lambda i, s: (0, 0, i))
    grouped_spec = pl.BlockSpec((PEER_HEADS, tt // LANES, N_KEYS, LANES), lambda i, s: (0, i, 0, 0))
    return pl.pallas_call(
        functools.partial(_peer_kernel, n_tiles),
        grid=(t // tt, n_tiles),
        in_specs=[
            pl.BlockSpec((tt // _TG, d, _TG), lambda i, s: (i, 0, 0)),
            pl.BlockSpec((PEER_TE, d), lambda i, s: (s, 0)),
            pl.BlockSpec((d, PEER_TE), lambda i, s: (0, s)),
            grouped_spec, grouped_spec, dense_spec, dense_spec,
            pl.BlockSpec((tt, d), lambda i, s: (i, 0)),
            pl.BlockSpec((1, 1, d), lambda i, s: (i // per_b, 0, 0)),
        ],
        out_specs=pl.BlockSpec((tt, d), lambda i, s: (i, 0)),
        out_shape=jax.ShapeDtypeStruct((t, d), F32),
        scratch_shapes=[
            pltpu.VMEM((tt // _TG, d, _TG), F32),
            pltpu.VMEM((tt // _TG, PEER_TE, _TG), F32),
            pltpu.VMEM((tt // _TG, PEER_TE, _TG), BF16),
        ],
        compiler_params=_params("parallel", "arbitrary"),
        name="peer_experts",
    )(ht, u, vt, a1, th, b2, xk, x2, gate_f)


def _tcast_kernel(x_ref, o_ref):
    o_ref[...] = x_ref[...].T.astype(o_ref.dtype)


def _cast_kernel(x_ref, o_ref):
    o_ref[...] = x_ref[...].astype(o_ref.dtype)


def _layer_cast(x, layer, dtype, transpose):
    _, r, c = x.shape
    tr, tc = min(r, 1024), min(c, 1024)
    return pl.pallas_call(
        _tcast_kernel if transpose else _cast_kernel,
        grid=(r // tr, c // tc),
        in_specs=[pl.BlockSpec((None, tr, tc), lambda i, j: (layer, i, j))],
        out_specs=pl.BlockSpec((tc, tr), lambda i, j: (j, i)) if transpose else pl.BlockSpec((tr, tc), lambda i, j: (i, j)),
        out_shape=jax.ShapeDtypeStruct((c, r) if transpose else (r, c), dtype),
        compiler_params=_params("parallel", "parallel"),
        name="layer_cast",
    )(x)


def _pad_heads(w, width):
    d = w.shape[0]
    w = w.reshape(d, HEADS, width)
    return jnp.pad(w, ((0, 0), (0, 0), (0, HEAD_DIM - width))).reshape(d, HEADS * HEAD_DIM)


def _pack_segments():
    segs = [(0, W_A + W_B, 0)]
    n_idx = IDX_HEADS * IDX_DIM + IDX_DIM + IDX_HEADS
    src = W_A + W_B
    dst = W_A + W_B
    segs.append((src, n_idx, dst)); src += n_idx; dst += W_I
    for _ in range(2):
        for h in range(HEADS):
            segs.append((src + GLA_DK * h, GLA_DK, dst + HEAD_DIM * h))
        src += HEADS * GLA_DK; dst += BRANCH_WIDTH
    segs.append((src, BRANCH_WIDTH, dst)); src += BRANCH_WIDTH; dst += BRANCH_WIDTH
    code_src = src; src += GLA_GATE_RANK
    segs.append((src, BRANCH_WIDTH, dst)); src += BRANCH_WIDTH; dst += BRANCH_WIDTH
    segs.append((code_src, GLA_GATE_RANK, dst)); dst += LANES
    segs.append((src, W_M, dst))
    assert dst + W_M == W_PACK
    return segs


def _pack_kernel(w_ref, o_ref):
    o_ref[...] = jnp.zeros_like(o_ref)
    for src, width, dst in _pack_segments():
        o_ref[0, :, dst:dst + width] = w_ref[0, :, src:src + width].astype(o_ref.dtype)


def _pack_w_in(w_in):
    depth, d, n_in = w_in.shape
    tr = 128
    return pl.pallas_call(
        _pack_kernel,
        grid=(depth, d // tr),
        in_specs=[pl.BlockSpec((1, tr, n_in), lambda l, i: (l, i, 0))],
        out_specs=pl.BlockSpec((1, tr, W_PACK), lambda l, i: (l, i, 0)),
        out_shape=jax.ShapeDtypeStruct((depth, d, W_PACK), BF16),
        compiler_params=_params("parallel", "parallel"),
        name="pack_w_in",
    )(w_in)


def kernel(x, c, w_ada, b_ada, norm_mix, norm_ffn, w_in, hgrn_lb_logits, hgrn_out_norm, dsa_q_norm, dsa_k_norm,
           gla_gate_up, gla_gate_bias, gla_out_norm, w_branch, w_out, peer_w_query, peer_sub_keys, peer_u, peer_v):
    bsz, seq, d = x.shape
    depth = w_in.shape[0]
    t = bsz * seq
    x2 = x.reshape(t, d)
    mod = _ada(c, w_ada, b_ada)
    w_pack = _pack_w_in(w_in)

    for l in range(depth):
        shift_m, scale_m, gate_m, shift_f, scale_f, gate_f = [
            mod[l, :, k * d:(k + 1) * d].reshape(bsz, 1, d) for k in range(6)]
        za, zb, zi, zc, zm = _inproj(x2, norm_mix[l].reshape(1, d), shift_m, scale_m, w_pack, l, seq)

        ya = _scan("hgrn", l, za, (hgrn_lb_logits,), hgrn_out_norm[l].reshape(1, HEAD_DIM), bsz, seq)
        yb = _dsa(zb, zi, dsa_q_norm[l].reshape(1, HEAD_DIM), dsa_k_norm[l].reshape(1, HEAD_DIM), bsz, seq)
        gup = jnp.pad(_pad_heads(gla_gate_up[l], GLA_DK), ((0, LANES - GLA_GATE_RANK), (0, 0)))
        gb = _pad_heads(gla_gate_bias[l].reshape(1, HEADS * GLA_DK), GLA_DK)
        yc = _scan("gla", l, zc, (gup, gb), gla_out_norm[l].reshape(1, HEAD_DIM), bsz, seq)

        x2 = _merge(ya, yb, yc, zm, x2, gate_m, w_branch[l].astype(BF16), w_out[l].astype(BF16), seq)

        ht, a1, th, b2, xk = _route(
            x2, norm_ffn[l].reshape(1, d), shift_f, scale_f, _layer_cast(peer_w_query, l, BF16, True),
            peer_sub_keys[l].reshape(PEER_HEADS * 2, N_KEYS, -1).astype(BF16), seq)
        x2 = _peer(ht, _layer_cast(peer_u, l, BF16, False), _layer_cast(peer_v, l, BF16, True),
                   a1, th, b2, xk, x2, gate_f, seq)

    return x2.reshape(bsz, seq, d)
```

```python
import functools

import numpy as np
import jax
import jax.numpy as jnp
from jax import lax
from jax.experimental import pallas as pl
from jax.experimental.pallas import tpu as pltpu

F32, BF16, I32 = jnp.float32, jnp.bfloat16, jnp.int32

D_MODEL = 1024
HEADS = 4
HEAD_DIM = 128
BRANCH_WIDTH = HEADS * HEAD_DIM
IDX_HEADS = 4
IDX_DIM = 64
DSA_MAX_TOPK = 256
QBLOCK = 256
GLA_DK = 64
GLA_GATE_RANK = 16
GLA_TAU = 16.0
PEER_HEADS = 8
N_KEYS = 128
PEER_TOPK = 16
PEER_TE = 1024
EPS = 1e-6

LANES = 128
SUBLANES = 8
VMEM_LIMIT_BYTES = 56 * 1024 * 1024

NEG_BIG = -1e30

W_A = 4 * BRANCH_WIDTH
W_B = 3 * BRANCH_WIDTH
W_I = 3 * LANES
W_C = 4 * BRANCH_WIDTH + LANES
W_M = 3 * D_MODEL
W_PACK = W_A + W_B + W_I + W_C + W_M


def _dot(a, b):
    return jnp.dot(a, b, preferred_element_type=F32)


def _dot_nt(a, b):
    return lax.dot_general(a, b, (((1,), (1,)), ((), ())), preferred_element_type=F32)


def _dot_tn(a, b):
    return lax.dot_general(a, b, (((0,), (0,)), ((), ())), preferred_element_type=F32)


def _split2(x):
    hi = x.astype(BF16)
    lo = (x - hi.astype(F32)).astype(BF16)
    return hi, lo


def _split3(x):
    hi = x.astype(BF16)
    r = x - hi.astype(F32)
    mid = r.astype(BF16)
    lo = (r - mid.astype(F32)).astype(BF16)
    return hi, mid, lo


def _params(*sem):
    return pltpu.CompilerParams(dimension_semantics=sem, vmem_limit_bytes=VMEM_LIMIT_BYTES)


def _modulate(x, gain, shift, scale):
    ms = jnp.mean(x * x, axis=-1, keepdims=True)
    return x * lax.rsqrt(ms + EPS) * gain * (1.0 + scale) + shift


def _ada_kernel(c_ref, w_ref, b_ref, o_ref):
    c = c_ref[...]
    sc = c * jax.nn.sigmoid(c)
    a_hi, a_lo = _split2(sc)
    w_hi, w_lo = _split2(w_ref[0])
    o_ref[0] = _dot(a_hi, w_hi) + _dot(a_hi, w_lo) + _dot(a_lo, w_hi) + b_ref[0]


def _ada(c, w_ada, b_ada):
    depth, d, n = w_ada.shape
    bsz = c.shape[0]
    tn = 1536
    return pl.pallas_call(
        _ada_kernel,
        grid=(depth, n // tn),
        in_specs=[
            pl.BlockSpec((bsz, d), lambda l, j: (0, 0)),
            pl.BlockSpec((1, d, tn), lambda l, j: (l, 0, j)),
            pl.BlockSpec((1, 1, tn), lambda l, j: (l, 0, j)),
        ],
        out_specs=pl.BlockSpec((1, bsz, tn), lambda l, j: (l, 0, j)),
        out_shape=jax.ShapeDtypeStruct((depth, bsz, n), F32),
        compiler_params=_params("parallel", "parallel"),
        name="ada",
    )(c, w_ada, b_ada.reshape(depth, 1, n))


def _inproj_kernel(x_ref, gain_ref, shift_ref, scale_ref, w_ref, *out_refs):
    h = _modulate(x_ref[...], gain_ref[...], shift_ref[0], scale_ref[0]).astype(BF16)
    off = 0
    for o_ref in out_refs:
        width = o_ref.shape[1]
        for c0 in range(0, width, 512):
            c1 = min(c0 + 512, width)
            o_ref[:, c0:c1] = _dot(h, w_ref[:, off + c0:off + c1]).astype(o_ref.dtype)
        off += width


def _inproj(x2, gain, shift, scale, w_pack, layer, seq):
    t, d = x2.shape
    tm = 256
    per_b = seq // tm
    widths = (W_A, W_B, W_I, W_C, W_M)
    dtypes = (F32, BF16, F32, F32, BF16)
    return pl.pallas_call(
        _inproj_kernel,
        grid=(t // tm,),
        in_specs=[
            pl.BlockSpec((tm, d), lambda i: (i, 0)),
            pl.BlockSpec((1, d), lambda i: (0, 0)),
            pl.BlockSpec((1, 1, d), lambda i: (i // per_b, 0, 0)),
            pl.BlockSpec((1, 1, d), lambda i: (i // per_b, 0, 0)),
            pl.BlockSpec((None, d, W_PACK), lambda i: (layer, 0, 0), pipeline_mode=pl.Buffered(1)),
        ],
        out_specs=[pl.BlockSpec((tm, w), lambda i: (i, 0)) for w in widths],
        out_shape=[jax.ShapeDtypeStruct((t, w), dt) for w, dt in zip(widths, dtypes)],
        compiler_params=_params("parallel"),
        name="inproj",
    )(x2, gain, shift, scale, w_pack)


def _group_row_bcast(x, group, row):
    n, w = x.shape
    if group >= SUBLANES:
        x3 = x.reshape(n // group, group, w)
        return jnp.broadcast_to(x3[:, row:row + 1, :], x3.shape).reshape(n, w)
    x3 = x.reshape(n // SUBLANES, SUBLANES, w)
    sub = lax.broadcasted_iota(I32, x3.shape, 1)
    n_groups = SUBLANES // group
    res = None
    for g in reversed(range(n_groups)):
        r = g * group + row
        bc = jnp.broadcast_to(x3[:, r:r + 1, :], x3.shape)
        res = bc if res is None else jnp.where(sub < (g + 1) * group, bc, res)
    return res.reshape(n, w)


def _scan_kernel(mode, layer, *refs):
    if mode == "hgrn":
        q_ref, f_ref, v_ref, g_ref, lbl_ref, gain_ref, lv_ref, tri_ref, y_ref, st_ref = refs
    else:
        q_ref, k_ref, v_ref, g_ref, code_ref, gup_ref, gb_ref, gain_ref, lv_ref, tri_ref, y_ref, st_ref = refs
    lt = q_ref.shape[0]
    n_levels = lt.bit_length() - 1

    @pl.when(pl.program_id(1) == 0)
    def _():
        st_ref[...] = jnp.zeros_like(st_ref)

    tri = tri_ref[...]
    lv = lv_ref[...]
    for h in range(HEADS):
        cols = slice(h * HEAD_DIM, (h + 1) * HEAD_DIM)
        if mode == "hgrn":
            lbl = lbl_ref[:, cols]
            e = jnp.exp(lbl - jnp.max(lbl, axis=0, keepdims=True))
            p = e / jnp.sum(e, axis=0, keepdims=True)
            lb = jnp.zeros((1, p.shape[1]), F32)
            for l2 in range(1, layer + 1):
                lb = lb + p[l2:l2 + 1, :]
            f = lb + (1.0 - lb) * jax.nn.sigmoid(f_ref[:, cols])
            lg = jnp.log(f)
            kk = 1.0 - f
            q = q_ref[:, cols] * (HEAD_DIM ** -0.5)
        else:
            c_hi, c_lo = _split2(code_ref[...])
            u_hi, u_lo = _split2(gup_ref[:, cols])
            z = _dot(c_hi, u_hi) + _dot(c_hi, u_lo) + _dot(c_lo, u_hi) + gb_ref[:, cols]
            lg = (jnp.minimum(z, 0.0) - jnp.log1p(jnp.exp(-jnp.abs(z)))) * (1.0 / GLA_TAU)
            kk = k_ref[:, cols]
            q = q_ref[:, cols] * (GLA_DK ** -0.5)

        g_hi, g_mid, g_lo = _split3(lg)
        b = _dot(tri, g_hi) + _dot(tri, g_mid) + _dot(tri, g_lo)

        s = jnp.where(lv == -1, _dot_nt(q.astype(BF16), kk.astype(BF16)), 0.0)
        for l in range(n_levels):
            n = 1 << l
            decay = jnp.exp(-jnp.abs(b - _group_row_bcast(b, 2 * n, n - 1)))
            s = jnp.where(lv == l, _dot_nt((q * decay).astype(BF16), (kk * decay).astype(BF16)), s)

        vb = v_ref[:, cols].astype(BF16)
        st = st_ref[h]
        o = _dot(s.astype(BF16), vb) + _dot_nt((q * jnp.exp(b)).astype(BF16), st.astype(BF16))
        b_last = b[lt - 1:lt, :]
        kd = (kk * jnp.exp(b_last - b)).astype(BF16)
        st_ref[h] = st * jnp.exp(b_last) + _dot_tn(vb, kd)

        ms = jnp.mean(o * o, axis=-1, keepdims=True)
        g = g_ref[:, cols]
        y = o * lax.rsqrt(ms + EPS) * gain_ref[...] * (g * jax.nn.sigmoid(g))
        y_ref[:, cols] = y.astype(y_ref.dtype)


def _scan_consts(lt):
    idx = np.arange(lt)
    x = idx[:, None] ^ idx[None, :]
    lvl = np.floor(np.log2(np.maximum(x, 1))).astype(np.int32)
    lv = np.where(idx[None, :] < idx[:, None], lvl, np.where(x == 0, -1, -2)).astype(np.int32)
    tri = (idx[None, :] <= idx[:, None]).astype(np.float32)
    return jnp.asarray(lv), jnp.asarray(tri, dtype=BF16)


def _scan(mode, layer, z, extra, out_gain, bsz, seq):
    t = z.shape[0]
    lt = 256
    per_b = seq // lt
    lv, tri = _scan_consts(lt)

    def col(cb):
        return pl.BlockSpec((lt, BRANCH_WIDTH), lambda b, c: (b * per_b + c, cb))

    const2 = lambda b, c: (0, 0)
    in_specs = [col(0), col(1), col(2), col(3)]
    args = [z, z, z, z]
    if mode == "hgrn":
        (lb_logits,) = extra
        in_specs.append(pl.BlockSpec(lb_logits.shape, const2))
        args.append(lb_logits)
    else:
        gup, gb = extra
        in_specs += [pl.BlockSpec((lt, LANES), lambda b, c: (b * per_b + c, 4 * HEADS)),
                     pl.BlockSpec(gup.shape, const2), pl.BlockSpec(gb.shape, const2)]
        args += [z, gup, gb]
    in_specs += [pl.BlockSpec((1, HEAD_DIM), const2), pl.BlockSpec((lt, lt), const2), pl.BlockSpec((lt, lt), const2)]
    args += [out_gain, lv, tri]
    return pl.pallas_call(
        functools.partial(_scan_kernel, mode, layer),
        grid=(bsz, per_b),
        in_specs=in_specs,
        out_specs=pl.BlockSpec((lt, BRANCH_WIDTH), lambda b, c: (b * per_b + c, 0)),
        out_shape=jax.ShapeDtypeStruct((t, BRANCH_WIDTH), BF16),
        scratch_shapes=[pltpu.VMEM((HEADS, HEAD_DIM, HEAD_DIM), F32)],
        compiler_params=_params("parallel", "arbitrary"),
        name="scan_" + mode,
    )(*args)


_KCH = 256


def _dsa_kernel(topk, q_ref, k_ref, v_ref, ziq_ref, zik_ref, qg_ref, kg_ref, stri_ref, o_ref,
                kn_ref, vt_ref, kih_ref, kil_ref, keys_ref, bias_ref, acc_ref):
    j = pl.program_id(1)
    seq = k_ref.shape[0]

    @pl.when(j == 0)
    def _prep():
        def body(c, carry):
            r0 = pl.multiple_of(c * _KCH, _KCH)
            kc = k_ref[pl.ds(r0, _KCH), :].astype(F32)
            for h in range(HEADS):
                kh = kc[:, h * HEAD_DIM:(h + 1) * HEAD_DIM]
                ms = jnp.mean(kh * kh, axis=-1, keepdims=True)
                kn_ref[pl.ds(r0, _KCH), h * HEAD_DIM:(h + 1) * HEAD_DIM] = (
                    kh * lax.rsqrt(ms + EPS) * kg_ref[...]).astype(BF16)
            vt_ref[c] = v_ref[pl.ds(r0, _KCH), :].astype(F32).T.astype(BF16)
            ki = zik_ref[pl.ds(r0, _KCH), :]
            hi = ki.astype(BF16)
            kih_ref[pl.ds(r0, _KCH), :] = hi
            kil_ref[pl.ds(r0, _KCH), :] = (ki - hi.astype(F32)).astype(BF16)
            return carry
        lax.fori_loop(0, seq // _KCH, body, 0)

    n_ch = (j * QBLOCK + QBLOCK + _KCH - 1) // _KCH
    ziq = ziq_ref[...]
    w_t = ziq[:, 2 * LANES:3 * LANES].T
    qi = jnp.concatenate([ziq[:, h * IDX_DIM:(h + 1) * IDX_DIM] for h in range(IDX_HEADS)], axis=0)
    qi_hi = (qi * (IDX_DIM ** -0.5)).astype(BF16)
    w_rows = [w_t[IDX_DIM + h:IDX_DIM + h + 1, :] * (IDX_HEADS ** -0.5) for h in range(IDX_HEADS)]
    q_pos = j * QBLOCK + lax.broadcasted_iota(I32, (_KCH, QBLOCK), 1)
    row_i = lax.broadcasted_iota(I32, (_KCH, QBLOCK), 0)

    def idx_body(c, carry):
        r0 = pl.multiple_of(c * _KCH, _KCH)
        k_hi = kih_ref[pl.ds(r0, _KCH), :][:, :IDX_DIM]
        k_lo = kil_ref[pl.ds(r0, _KCH), :][:, :IDX_DIM]
        logit = _dot_nt(k_hi, qi_hi) + _dot_nt(k_lo, qi_hi)
        score = jnp.zeros((_KCH, QBLOCK), F32)
        for h in range(IDX_HEADS):
            score = score + w_rows[h] * jnp.maximum(logit[:, h * QBLOCK:(h + 1) * QBLOCK], 0.0)
        keys_ref[pl.ds(r0, _KCH), :] = jnp.where(r0 + row_i <= q_pos, score, -jnp.inf)
        return carry
    lax.fori_loop(0, n_ch, idx_body, 0)

    def reduce_keys(fn, combine, init):
        def body(c, acc):
            r0 = pl.multiple_of(c * _KCH, _KCH)
            v = fn(keys_ref[pl.ds(r0, _KCH), :]).reshape(_KCH // SUBLANES, SUBLANES, QBLOCK)
            return combine(acc, v)
        return lax.fori_loop(0, n_ch, body, jnp.full((SUBLANES, QBLOCK), init, F32))

    def count(pred):
        acc = reduce_keys(lambda x: jnp.where(pred(x), 1.0, 0.0), lambda a, v: a + jnp.sum(v, axis=0), 0.0)
        return jnp.sum(acc, axis=0, keepdims=True)

    kf = float(topk)
    has_k = (q_pos[0:1, :] + 1).astype(F32) >= kf
    mx = jnp.max(reduce_keys(lambda x: x, lambda a, v: jnp.maximum(a, jnp.max(v, axis=0)), -jnp.inf),
                 axis=0, keepdims=True)
    mn = jnp.min(reduce_keys(lambda x: jnp.where(x == -jnp.inf, jnp.inf, x),
                             lambda a, v: jnp.minimum(a, jnp.min(v, axis=0)), jnp.inf), axis=0, keepdims=True)

    def bisect(_, bracket):
        lo, hi = bracket
        mid = lo + 0.5 * (hi - lo)
        up = count(lambda x: x >= mid) >= kf
        return jnp.where(up, mid, lo), jnp.where(up, hi, mid)

    def kth_of(bracket):
        hi = bracket[1]
        below = reduce_keys(lambda x: jnp.where(x < hi, x, -jnp.inf),
                            lambda a, v: jnp.maximum(a, jnp.max(v, axis=0)), -jnp.inf)
        return jnp.max(below, axis=0, keepdims=True)

    def refine(state):
        bracket = lax.fori_loop(0, 8, bisect, state[0])
        t = kth_of(bracket)
        unsettled = jnp.where(jnp.logical_and(has_k, count(lambda x: x >= t) < kf), 1.0, 0.0)
        return bracket, t, jnp.max(unsettled), state[3] + 1

    bracket0 = lax.fori_loop(0, 16, bisect, (mn, mx + (mx - mn) + 1.0))
    _, thr, _, _ = lax.while_loop(lambda st: jnp.logical_and(st[2] > 0.0, st[3] < 20), refine,
                                  refine((bracket0, mn, jnp.float32(1.0), jnp.int32(0))))
    thr = jnp.where(has_k, thr, -jnp.inf)
    need = kf - count(lambda x: x > thr)

    def sel_body(c, seen):
        r0 = pl.multiple_of(c * _KCH, _KCH)
        x = keys_ref[pl.ds(r0, _KCH), :]
        eq = jnp.where(x == thr, 1.0, 0.0)
        rank = seen + _dot(stri_ref[...], eq.astype(BF16))
        tie_ok = jnp.where(x == thr, jnp.where(rank < need, 0.0, NEG_BIG), NEG_BIG)
        bias = jnp.where(x > thr, 0.0, tie_ok)
        bias_ref[pl.ds(r0, _KCH), :] = jnp.where(x == -jnp.inf, NEG_BIG, bias)
        return seen + jnp.sum(eq, axis=0, keepdims=True)
    lax.fori_loop(0, n_ch, sel_body, jnp.zeros((1, QBLOCK), F32))

    qf = q_ref[...].astype(F32)
    qn = []
    for h in range(HEADS):
        qh = qf[:, h * HEAD_DIM:(h + 1) * HEAD_DIM]
        ms = jnp.mean(qh * qh, axis=-1, keepdims=True)
        qn.append((qh * lax.rsqrt(ms + EPS) * qg_ref[...] * (HEAD_DIM ** -0.5)).astype(BF16))
    acc_ref[...] = jnp.zeros_like(acc_ref)

    @pl.when(n_ch % 2 == 1)
    def _():
        bias_ref[pl.ds(pl.multiple_of(n_ch * _KCH, _KCH), _KCH), :] = jnp.full((_KCH, QBLOCK), NEG_BIG, F32)

    def att_body(c, carry):
        r0 = pl.multiple_of(c * (2 * _KCH), 2 * _KCH)
        bias = bias_ref[pl.ds(r0, 2 * _KCH), :]
        out = []
        for h in range(HEADS):
            m, l = carry[h]
            hd = slice(h * HEAD_DIM, (h + 1) * HEAD_DIM)
            sc = _dot_nt(kn_ref[pl.ds(r0, 2 * _KCH), hd], qn[h]) + bias
            m_new = jnp.maximum(m, jnp.max(sc, axis=0, keepdims=True))
            p = jnp.exp(sc - m_new)
            alpha = jnp.exp(m - m_new)
            pb = p.astype(BF16)
            pv = _dot(vt_ref[2 * c, hd, :], pb[:_KCH]) + _dot(vt_ref[2 * c + 1, hd, :], pb[_KCH:])
            acc_ref[h] = alpha * acc_ref[h] + pv
            out.append((m_new, alpha * l + jnp.sum(p, axis=0, keepdims=True)))
        return tuple(out)
    init = tuple((jnp.full((1, QBLOCK), NEG_BIG, F32), jnp.zeros((1, QBLOCK), F32)) for _ in range(HEADS))
    stats = lax.fori_loop(0, (n_ch + 1) // 2, att_body, init)
    for h in range(HEADS):
        o_ref[:, h * HEAD_DIM:(h + 1) * HEAD_DIM] = (acc_ref[h] / stats[h][1]).T.astype(o_ref.dtype)


def _dsa(zb, zi, q_gain, k_gain, bsz, seq):
    t = zb.shape[0]
    topk = min(DSA_MAX_TOPK, seq // 4)
    nq = seq // QBLOCK
    assert seq % (2 * _KCH) == 0
    idx = np.arange(_KCH)
    stri = jnp.asarray((idx[None, :] < idx[:, None]).astype(np.float32), dtype=BF16)
    const2 = lambda b, j: (0, 0)
    return pl.pallas_call(
        functools.partial(_dsa_kernel, topk),
        grid=(bsz, nq),
        in_specs=[
            pl.BlockSpec((QBLOCK, BRANCH_WIDTH), lambda b, j: (b * nq + j, 0)),
            pl.BlockSpec((seq, BRANCH_WIDTH), lambda b, j: (b, 1)),
            pl.BlockSpec((seq, BRANCH_WIDTH), lambda b, j: (b, 2)),
            pl.BlockSpec((QBLOCK, W_I), lambda b, j: (b * nq + j, 0)),
            pl.BlockSpec((seq, LANES), lambda b, j: (b, 2)),
            pl.BlockSpec((1, HEAD_DIM), const2),
            pl.BlockSpec((1, HEAD_DIM), const2),
            pl.BlockSpec((_KCH, _KCH), const2),
        ],
        out_specs=pl.BlockSpec((QBLOCK, BRANCH_WIDTH), lambda b, j: (b * nq + j, 0)),
        out_shape=jax.ShapeDtypeStruct((t, BRANCH_WIDTH), BF16),
        scratch_shapes=[
            pltpu.VMEM((seq, BRANCH_WIDTH), BF16),
            pltpu.VMEM((seq // _KCH, BRANCH_WIDTH, _KCH), BF16),
            pltpu.VMEM((seq, LANES), BF16),
            pltpu.VMEM((seq, LANES), BF16),
            pltpu.VMEM((seq, QBLOCK), F32),
            pltpu.VMEM((seq, QBLOCK), F32),
            pltpu.VMEM((HEADS, HEAD_DIM, QBLOCK), F32),
        ],
        compiler_params=_params("parallel", "arbitrary"),
        name="dsa",
    )(zb, zb, zb, zi, zi, q_gain, k_gain, stri)


def _merge_kernel(ya_ref, yb_ref, yc_ref, zm_ref, x_ref, gm_ref, wb_ref, wo_ref, o_ref):
    d = x_ref.shape[1]
    mixed = None
    for g, y_ref in enumerate((ya_ref, yb_ref, yc_ref)):
        up = _dot(y_ref[...], wb_ref[g])
        term = jax.nn.sigmoid(zm_ref[:, g * d:(g + 1) * d].astype(F32)) * up
        mixed = term if mixed is None else mixed + term
    o_ref[...] = x_ref[...] + gm_ref[0] * _dot(mixed.astype(BF16), wo_ref[...])


def _merge(ya, yb, yc, zm, x2, gate_m, w_branch, w_out, seq):
    t, d = x2.shape
    tm = 256
    per_b = seq // tm
    row = lambda i: (i, 0)
    return pl.pallas_call(
        _merge_kernel,
        grid=(t // tm,),
        in_specs=[
            pl.BlockSpec((tm, BRANCH_WIDTH), row), pl.BlockSpec((tm, BRANCH_WIDTH), row),
            pl.BlockSpec((tm, BRANCH_WIDTH), row), pl.BlockSpec((tm, W_M), row), pl.BlockSpec((tm, d), row),
            pl.BlockSpec((1, 1, d), lambda i: (i // per_b, 0, 0)),
            pl.BlockSpec(w_branch.shape, lambda i: (0, 0, 0)),
            pl.BlockSpec(w_out.shape, lambda i: (0, 0)),
        ],
        out_specs=pl.BlockSpec((tm, d), row),
        out_shape=jax.ShapeDtypeStruct((t, d), F32),
        compiler_params=_params("parallel"),
        name="merge",
    )(ya, yb, yc, zm, x2, gate_m, w_branch, w_out)


_CAND_ROWS = 72
_ROUTE_GROUP = 4


def _cand_layout():
    flat = np.zeros((_CAND_ROWS,), np.int32)
    live = np.zeros((_CAND_ROWS,), bool)
    r = 0
    for b in range(16):
        flat[r], live[r] = b, True
        r += 1
    for a in (1, 2, 3):
        for b in range(8):
            flat[r], live[r] = a * 16 + b, True
            r += 1
    for a in range(16):
        flat[r], live[r] = a * 16, a >= 4
        r += 1
    for a in range(8):
        flat[r], live[r] = a * 16 + 1, a >= 4
        r += 1
    for a in range(8):
        flat[r], live[r] = a * 16 + 2, a == 4
        r += 1
    flat = np.where(live, flat, 1000 + np.arange(_CAND_ROWS))
    return flat.astype(np.int32), live


def _candidates(v1, v2, live):
    cand = jnp.concatenate(
        [v1[0:1, :] + v2] + [v1[a:a + 1, :] + v2[0:8, :] for a in (1, 2, 3)]
        + [v1 + v2[0:1, :], v1[0:8, :] + v2[1:2, :], v1[0:8, :] + v2[2:3, :]], axis=0)
    return jnp.where(live, cand, -jnp.inf)


def _gone(work):
    return jnp.sum(jnp.where(work == -jnp.inf, 1.0, 0.0), axis=0, keepdims=True)


def _route_head_exact(s1, s2, row_i, flat, live, v1_ref, v2_ref, b2_ref, th_ref, x2_ref):
    ranks = []
    for s, vals_ref in ((s1, v1_ref), (s2, v2_ref)):
        work = s
        rank = jnp.full(s.shape, 127.0, F32)
        for a in range(PEER_TOPK):
            m = jnp.max(work, axis=0, keepdims=True)
            idx = jnp.min(jnp.where(work == m, row_i, N_KEYS), axis=0, keepdims=True)
            hit = row_i == idx
            work = jnp.where(hit, -jnp.inf, work)
            rank = jnp.where(hit, float(a), rank)
            vals_ref[a:a + 1, :] = m
        ranks.append(rank)
    r1, r2 = ranks
    v1 = v1_ref[...]
    v2 = v2_ref[...]
    cand = _candidates(v1, v2, live)
    sel = jnp.zeros(cand.shape, F32)
    work = cand
    for _ in range(PEER_TOPK):
        m = jnp.max(work, axis=0, keepdims=True)
        fi = jnp.min(jnp.where(work == m, flat, 9999), axis=0, keepdims=True)
        hit = flat == fi
        work = jnp.where(hit, -jnp.inf, work)
        sel = jnp.where(hit, 1.0, sel)
    z = jnp.sum(sel * jnp.exp(cand - (v1[0:1, :] + v2[0:1, :])), axis=0, keepdims=True)
    b2_ref[...] = jnp.exp(s2 - v2[0:1, :]) * (0.5 / z)

    zeros8 = jnp.zeros((8, s1.shape[1]), F32)
    cnt_hi = (sel[40:56, :] + jnp.concatenate([sel[56:64, :], zeros8], axis=0)
              + jnp.concatenate([sel[64:72, :], zeros8], axis=0))
    cnt_lo = [jnp.sum(sel[0:16, :], axis=0, keepdims=True)] + [
        jnp.sum(sel[16 + 8 * (a - 1):24 + 8 * (a - 1), :], axis=0, keepdims=True) for a in (1, 2, 3)]
    cnt = jnp.zeros(s1.shape, F32)
    for a in range(PEER_TOPK):
        ca = cnt_lo[a] if a < 4 else cnt_hi[a:a + 1, :]
        cnt = jnp.where(r1 == float(a), ca, cnt)
    th_ref[...] = 0.5 - cnt
    x2_ref[...] = -r2


def _route_kernel(x_ref, gain_ref, shift_ref, scale_ref, wq_ref, sk_ref, flat_ref, live_ref,
                  ht_ref, a1_ref, th_ref, b2_ref, x2_ref, qt_ref, v1_ref, v2_ref):
    tt = x_ref.shape[0]
    h2 = _modulate(x_ref[...], gain_ref[...], shift_ref[0], scale_ref[0])
    ht = h2.T.astype(BF16)
    for g in range(tt // _TG):
        ht_ref[g] = ht[:, g * _TG:(g + 1) * _TG]
    qt_ref[...] = _dot(wq_ref[...], ht)
    row_i = lax.broadcasted_iota(I32, (N_KEYS, tt), 0)
    flat = flat_ref[...]
    live = live_ref[...] > 0.0
    n_dead = float(_CAND_ROWS - int(_cand_layout()[1].sum()))

    group = v1_ref.shape[0]
    for h0 in range(0, PEER_HEADS, group):
        heads = range(h0, h0 + group)
        s1 = [_dot(sk_ref[2 * h], qt_ref[(2 * h) * N_KEYS:(2 * h + 1) * N_KEYS, :].astype(BF16)) for h in heads]
        s2 = [_dot(sk_ref[2 * h + 1], qt_ref[(2 * h + 1) * N_KEYS:(2 * h + 2) * N_KEYS, :].astype(BF16))
              for h in heads]

        work = [list(s1), list(s2)]
        for a in range(PEER_TOPK):
            for half, vals_ref in enumerate((v1_ref, v2_ref)):
                for g in range(group):
                    m = jnp.max(work[half][g], axis=0, keepdims=True)
                    work[half][g] = jnp.where(work[half][g] == m, -jnp.inf, work[half][g])
                    vals_ref[g, a:a + 1, :] = m
        v1 = [v1_ref[g] for g in range(group)]
        v2 = [v2_ref[g] for g in range(group)]
        cand = [_candidates(v1[g], v2[g], live) for g in range(group)]
        wc = list(cand)
        t16 = [None] * group
        for _ in range(PEER_TOPK):
            for g in range(group):
                t16[g] = jnp.max(wc[g], axis=0, keepdims=True)
                wc[g] = jnp.where(wc[g] == t16[g], -jnp.inf, wc[g])
        bad = []
        last = PEER_TOPK - 1
        for g, h in enumerate(heads):
            t17 = jnp.max(wc[g], axis=0, keepdims=True)
            top = v1[g][0:1, :] + v2[g][0:1, :]
            z = jnp.sum(jnp.where(cand[g] >= t16[g], jnp.exp(cand[g] - top), 0.0), axis=0, keepdims=True)
            a1_ref[h] = jnp.exp(s1[g] - v1[g][0:1, :])
            b2_ref[h] = jnp.exp(s2[g] - v2[g][0:1, :]) * (0.5 / z)
            th_ref[h] = jnp.where(s1[g] >= v1[g][last:last + 1, :], 0.5 * (t16[g] + t17) - s1[g], jnp.inf)
            x2_ref[h] = jnp.where(s2[g] >= v2[g][last:last + 1, :], s2[g], -jnp.inf)
            scale = (jnp.abs(v1[g][0:1, :]) + jnp.abs(v1[g][last:last + 1, :])
                     + jnp.abs(v2[g][0:1, :]) + jnp.abs(v2[g][last:last + 1, :]))
            close = jnp.where(t16[g] - t17 > scale * 2.0 ** -22, 0.0, 1.0)
            bad.append(jnp.max(jnp.abs(_gone(work[0][g]) - PEER_TOPK) + jnp.abs(_gone(work[1][g]) - PEER_TOPK)
                               + jnp.abs(_gone(wc[g]) - (PEER_TOPK + n_dead)) + close) > 0.0)

        for g, h in enumerate(heads):
            @pl.when(bad[g])
            def _(g=g, h=h):
                _route_head_exact(s1[g], s2[g], row_i, flat, live, v1_ref.at[g], v2_ref.at[g],
                                  b2_ref.at[h], th_ref.at[h], x2_ref.at[h])


def _route(x2, gain, shift, scale, wq_t, sub_keys, seq):
    t, d = x2.shape
    tt = 256
    per_b = seq // tt
    flat_np, live_np = _cand_layout()
    flat = jnp.asarray(np.broadcast_to(flat_np[:, None], (_CAND_ROWS, tt)).copy())
    live = jnp.asarray(np.broadcast_to(live_np[:, None], (_CAND_ROWS, tt)).astype(np.float32))
    dense = jax.ShapeDtypeStruct((PEER_HEADS, N_KEYS, t), F32)
    dense_spec = pl.BlockSpec((PEER_HEADS, N_KEYS, tt), lambda i: (0, 0, i))
    const2 = lambda i: (0, 0)
    return pl.pallas_call(
        _route_kernel,
        grid=(t // tt,),
        in_specs=[
            pl.BlockSpec((tt, d), lambda i: (i, 0)),
            pl.BlockSpec((1, d), const2),
            pl.BlockSpec((1, 1, d), lambda i: (i // per_b, 0, 0)),
            pl.BlockSpec((1, 1, d), lambda i: (i // per_b, 0, 0)),
            pl.BlockSpec(wq_t.shape, const2),
            pl.BlockSpec(sub_keys.shape, lambda i: (0, 0, 0)),
            pl.BlockSpec((_CAND_ROWS, tt), const2),
            pl.BlockSpec((_CAND_ROWS, tt), const2),
        ],
        out_specs=[pl.BlockSpec((tt // _TG, d, _TG), lambda i: (i, 0, 0)),
                   dense_spec, dense_spec, dense_spec, dense_spec],
        out_shape=[jax.ShapeDtypeStruct((t // _TG, d, _TG), BF16), dense, dense, dense, dense],
        scratch_shapes=[
            pltpu.VMEM((PEER_HEADS * 2 * N_KEYS, tt), F32),
            pltpu.VMEM((_ROUTE_GROUP, PEER_TOPK, tt), F32),
            pltpu.VMEM((_ROUTE_GROUP, PEER_TOPK, tt), F32),
        ],
        compiler_params=_params("parallel"),
        name="peer_route",
    )(x2, gain, shift, scale, wq_t, sub_keys, flat, live)


_JB = 4
_IB = 4
_TG = 256
_RB = _IB * N_KEYS


def _peer_kernel(n_tiles, ht_ref, u_ref, vt_ref, a1_ref, th_ref, b2_ref, x2_ref, x_ref, gf_ref, o_ref,
                 acc_ref, at_ref, gt_ref, bc_ref):
    s = pl.program_id(1)
    n_g, te, tg = at_ref.shape
    tt = n_g * tg
    n_i = te // N_KEYS
    n_jb = N_KEYS // (SUBLANES * _JB)
    assert n_i == SUBLANES
    assert _RB == _IB * N_KEYS

    def bcast_rows():
        i0 = pl.multiple_of(s * n_i, SUBLANES)
        for h in range(PEER_HEADS):
            a1g = a1_ref[h, pl.ds(i0, n_i), :]
            thg = th_ref[h, pl.ds(i0, n_i), :]
            for ii in range(n_i):
                bc_ref[2 * (h * n_i + ii)] = jnp.broadcast_to(a1g[ii:ii + 1, :], (SUBLANES, tt))
                bc_ref[2 * (h * n_i + ii) + 1] = jnp.broadcast_to(thg[ii:ii + 1, :], (SUBLANES, tt))

    def mask_block(tb, jb, ib):
        g, lane0 = divmod(tb * LANES, tg)
        lanes = slice(tb * LANES, (tb + 1) * LANES)
        lanes_g = slice(lane0, lane0 + LANES)
        j0 = jb * (SUBLANES * _JB)
        w = [[None] * _JB for _ in range(_IB)]
        for h in range(PEER_HEADS):
            x2 = [x2_ref[h, j0 + SUBLANES * k:j0 + SUBLANES * (k + 1), lanes] for k in range(_JB)]
            b2 = [b2_ref[h, j0 + SUBLANES * k:j0 + SUBLANES * (k + 1), lanes] for k in range(_JB)]
            for di in range(_IB):
                row = 2 * (h * n_i + ib * _IB + di)
                a1v = bc_ref[row, :, lanes]
                thv = bc_ref[row + 1, :, lanes]
                for k in range(_JB):
                    term = jnp.where(x2[k] >= thv, a1v * b2[k], 0.0)
                    w[di][k] = term if w[di][k] is None else w[di][k] + term
        for di in range(_IB):
            r0 = (ib * _IB + di) * N_KEYS + j0
            rows = slice(r0, r0 + SUBLANES * _JB)
            a = at_ref[g, rows, lanes_g]
            act2 = a + a * lax.erf(a * np.float32(np.sqrt(0.5)))
            gt_ref[g, rows, lanes_g] = (jnp.concatenate(w[di], axis=0) * act2).astype(BF16)

    @pl.when(s == 0)
    def _():
        acc_ref[...] = jnp.zeros_like(acc_ref)

    def pre_act(k):
        rows = slice(k * _RB, (k + 1) * _RB)
        for g in range(n_g):
            at_ref[g, rows, :] = _dot(u_ref[rows, :], ht_ref[g])

    def fold(k):
        rows = slice(k * _RB, (k + 1) * _RB)
        for g in range(n_g):
            acc_ref[g] += _dot(vt_ref[:, rows], gt_ref[g, rows, :])

    bcast_rows()
    n_slices = te // _RB
    pre_act(0)
    for k in range(n_slices):
        if k + 1 < n_slices:
            pre_act(k + 1)
        if k >= 1:
            fold(k - 1)
        for tb in range(tt // LANES):
            for jb in range(n_jb):
                mask_block(tb, jb, k)
    fold(n_slices - 1)

    @pl.when(s == n_tiles - 1)
    def _():
        for g in range(n_g):
            rows = slice(g * tg, (g + 1) * tg)
            o_ref[rows, :] = x_ref[rows, :] + gf_ref[0] * acc_ref[g].T


def _peer(ht, u, vt, a1, th, b2, xk, x2, gate_f, seq):
    t, d = x2.shape
    n_tiles = u.shape[0] // PEER_TE
    tt = 512
    per_b = seq // tt
    f32_spec = pl.BlockSpec((PEER_HEADS, N_KEYS, tt), lambda i, s: (0, 0, i))
    return pl.pallas_call(
        functools.partial(_peer_kernel, n_tiles),
        grid=(t // tt, n_tiles),
        in_specs=[
            pl.BlockSpec((tt // _TG, d, _TG), lambda i, s: (i, 0, 0)),
            pl.BlockSpec((PEER_TE, d), lambda i, s: (s, 0)),
            pl.BlockSpec((d, PEER_TE), lambda i, s: (0, s)),
            f32_spec, f32_spec, f32_spec, f32_spec,
            pl.BlockSpec((tt, d), lambda i, s: (i, 0)),
            pl.BlockSpec((1, 1, d), lambda i, s: (i // per_b, 0, 0)),
        ],
        out_specs=pl.BlockSpec((tt, d), lambda i, s: (i, 0)),
        out_shape=jax.ShapeDtypeStruct((t, d), F32),
        scratch_shapes=[
            pltpu.VMEM((tt // _TG, d, _TG), F32),
            pltpu.VMEM((tt // _TG, PEER_TE, _TG), F32),
            pltpu.VMEM((tt // _TG, PEER_TE, _TG), BF16),
            pltpu.VMEM((2 * PEER_HEADS * PEER_TE // N_KEYS, SUBLANES, tt), F32),
        ],
        compiler_params=_params("parallel", "arbitrary"),
        name="peer_experts",
    )(ht, u, vt, a1, th, b2, xk, x2, gate_f)


def _tcast_kernel(x_ref, o_ref):
    o_ref[...] = x_ref[...].T.astype(o_ref.dtype)


def _cast_kernel(x_ref, o_ref):
    o_ref[...] = x_ref[...].astype(o_ref.dtype)


def _layer_cast(x, layer, dtype, transpose):
    _, r, c = x.shape
    tr, tc = min(r, 1024), min(c, 1024)
    return pl.pallas_call(
        _tcast_kernel if transpose else _cast_kernel,
        grid=(r // tr, c // tc),
        in_specs=[pl.BlockSpec((None, tr, tc), lambda i, j: (layer, i, j))],
        out_specs=pl.BlockSpec((tc, tr), lambda i, j: (j, i)) if transpose else pl.BlockSpec((tr, tc), lambda i, j: (i, j)),
        out_shape=jax.ShapeDtypeStruct((c, r) if transpose else (r, c), dtype),
        compiler_params=_params("parallel", "parallel"),
        name="layer_cast",
    )(x)


def _pad_heads(w, width):
    d = w.shape[0]
    w = w.reshape(d, HEADS, width)
    return jnp.pad(w, ((0, 0), (0, 0), (0, HEAD_DIM - width))).reshape(d, HEADS * HEAD_DIM)


def _pack_segments():
    segs = [(0, W_A + W_B, 0)]
    n_idx = IDX_HEADS * IDX_DIM + IDX_DIM + IDX_HEADS
    src = W_A + W_B
    dst = W_A + W_B
    segs.append((src, n_idx, dst)); src += n_idx; dst += W_I
    for _ in range(2):
        for h in range(HEADS):
            segs.append((src + GLA_DK * h, GLA_DK, dst + HEAD_DIM * h))
        src += HEADS * GLA_DK; dst += BRANCH_WIDTH
    segs.append((src, BRANCH_WIDTH, dst)); src += BRANCH_WIDTH; dst += BRANCH_WIDTH
    code_src = src; src += GLA_GATE_RANK
    segs.append((src, BRANCH_WIDTH, dst)); src += BRANCH_WIDTH; dst += BRANCH_WIDTH
    segs.append((code_src, GLA_GATE_RANK, dst)); dst += LANES
    segs.append((src, W_M, dst))
    assert dst + W_M == W_PACK
    return segs


def _pack_kernel(w_ref, o_ref):
    o_ref[...] = jnp.zeros_like(o_ref)
    for src, width, dst in _pack_segments():
        o_ref[0, :, dst:dst + width] = w_ref[0, :, src:src + width].astype(o_ref.dtype)


def _pack_w_in(w_in):
    depth, d, n_in = w_in.shape
    tr = 128
    return pl.pallas_call(
        _pack_kernel,
        grid=(depth, d // tr),
        in_specs=[pl.BlockSpec((1, tr, n_in), lambda l, i: (l, i, 0))],
        out_specs=pl.BlockSpec((1, tr, W_PACK), lambda l, i: (l, i, 0)),
        out_shape=jax.ShapeDtypeStruct((depth, d, W_PACK), BF16),
        compiler_params=_params("parallel", "parallel"),
        name="pack_w_in",
    )(w_in)


def kernel(x, c, w_ada, b_ada, norm_mix, norm_ffn, w_in, hgrn_lb_logits, hgrn_out_norm, dsa_q_norm, dsa_k_norm,
           gla_gate_up, gla_gate_bias, gla_out_norm, w_branch, w_out, peer_w_query, peer_sub_keys, peer_u, peer_v):
    bsz, seq, d = x.shape
    depth = w_in.shape[0]
    t = bsz * seq
    x2 = x.reshape(t, d)
    mod = _ada(c, w_ada, b_ada)
    w_pack = _pack_w_in(w_in)

    for l in range(depth):
        shift_m, scale_m, gate_m, shift_f, scale_f, gate_f = [
            mod[l, :, k * d:(k + 1) * d].reshape(bsz, 1, d) for k in range(6)]
        za, zb, zi, zc, zm = _inproj(x2, norm_mix[l].reshape(1, d), shift_m, scale_m, w_pack, l, seq)

        ya = _scan("hgrn", l, za, (hgrn_lb_logits,), hgrn_out_norm[l].reshape(1, HEAD_DIM), bsz, seq)
        yb = _dsa(zb, zi, dsa_q_norm[l].reshape(1, HEAD_DIM), dsa_k_norm[l].reshape(1, HEAD_DIM), bsz, seq)
        gup = jnp.pad(_pad_heads(gla_gate_up[l], GLA_DK), ((0, LANES - GLA_GATE_RANK), (0, 0)))
        gb = _pad_heads(gla_gate_bias[l].reshape(1, HEADS * GLA_DK), GLA_DK)
        yc = _scan("gla", l, zc, (gup, gb), gla_out_norm[l].reshape(1, HEAD_DIM), bsz, seq)

        x2 = _merge(ya, yb, yc, zm, x2, gate_m, w_branch[l].astype(BF16), w_out[l].astype(BF16), seq)

        ht, a1, th, b2, xk = _route(
            x2, norm_ffn[l].reshape(1, d), shift_f, scale_f, _layer_cast(peer_w_query, l, BF16, True),
            peer_sub_keys[l].reshape(PEER_HEADS * 2, N_KEYS, -1).astype(BF16), seq)
        x2 = _peer(ht, _layer_cast(peer_u, l, BF16, False), _layer_cast(peer_v, l, BF16, True),
                   a1, th, b2, xk, x2, gate_f, seq)

    return x2.reshape(bsz, seq, d)
```

```python
import functools

import numpy as np
import jax
import jax.numpy as jnp
from jax import lax
from jax.experimental import pallas as pl
from jax.experimental.pallas import tpu as pltpu

F32, BF16, I32 = jnp.float32, jnp.bfloat16, jnp.int32

D_MODEL = 1024
HEADS = 4
HEAD_DIM = 128
BRANCH_WIDTH = HEADS * HEAD_DIM
IDX_HEADS = 4
IDX_DIM = 64
DSA_MAX_TOPK = 256
QBLOCK = 256
GLA_DK = 64
GLA_GATE_RANK = 16
GLA_TAU = 16.0
PEER_HEADS = 8
N_KEYS = 128
PEER_TOPK = 16
PEER_TE = 1024
EPS = 1e-6

LANES = 128
SUBLANES = 8
VMEM_LIMIT_BYTES = 56 * 1024 * 1024

NEG_BIG = -1e30

W_A = 4 * BRANCH_WIDTH
W_B = 3 * BRANCH_WIDTH
W_I = 3 * LANES
W_C = 4 * BRANCH_WIDTH + LANES
W_M = 3 * D_MODEL
W_PACK = W_A + W_B + W_I + W_C + W_M


def _dot(a, b):
    return jnp.dot(a, b, preferred_element_type=F32)


def _dot_nt(a, b):
    return lax.dot_general(a, b, (((1,), (1,)), ((), ())), preferred_element_type=F32)


def _dot_tn(a, b):
    return lax.dot_general(a, b, (((0,), (0,)), ((), ())), preferred_element_type=F32)


def _split2(x):
    hi = x.astype(BF16)
    lo = (x - hi.astype(F32)).astype(BF16)
    return hi, lo


def _split3(x):
    hi = x.astype(BF16)
    r = x - hi.astype(F32)
    mid = r.astype(BF16)
    lo = (r - mid.astype(F32)).astype(BF16)
    return hi, mid, lo


def _params(*sem):
    return pltpu.CompilerParams(dimension_semantics=sem, vmem_limit_bytes=VMEM_LIMIT_BYTES)


def _modulate(x, gain, shift, scale):
    ms = jnp.mean(x * x, axis=-1, keepdims=True)
    return x * lax.rsqrt(ms + EPS) * gain * (1.0 + scale) + shift


def _ada_kernel(c_ref, w_ref, b_ref, o_ref):
    c = c_ref[...]
    sc = c * jax.nn.sigmoid(c)
    a_hi, a_lo = _split2(sc)
    w_hi, w_lo = _split2(w_ref[0])
    o_ref[0] = _dot(a_hi, w_hi) + _dot(a_hi, w_lo) + _dot(a_lo, w_hi) + b_ref[0]


def _ada(c, w_ada, b_ada):
    depth, d, n = w_ada.shape
    bsz = c.shape[0]
    tn = 1536
    return pl.pallas_call(
        _ada_kernel,
        grid=(depth, n // tn),
        in_specs=[
            pl.BlockSpec((bsz, d), lambda l, j: (0, 0)),
            pl.BlockSpec((1, d, tn), lambda l, j: (l, 0, j)),
            pl.BlockSpec((1, 1, tn), lambda l, j: (l, 0, j)),
        ],
        out_specs=pl.BlockSpec((1, bsz, tn), lambda l, j: (l, 0, j)),
        out_shape=jax.ShapeDtypeStruct((depth, bsz, n), F32),
        compiler_params=_params("parallel", "parallel"),
        name="ada",
    )(c, w_ada, b_ada.reshape(depth, 1, n))


def _inproj_kernel(x_ref, gain_ref, shift_ref, scale_ref, w_ref, *out_refs):
    h = _modulate(x_ref[...], gain_ref[...], shift_ref[0], scale_ref[0]).astype(BF16)
    off = 0
    for o_ref in out_refs:
        width = o_ref.shape[1]
        for c0 in range(0, width, 512):
            c1 = min(c0 + 512, width)
            o_ref[:, c0:c1] = _dot(h, w_ref[:, off + c0:off + c1]).astype(o_ref.dtype)
        off += width


def _inproj(x2, gain, shift, scale, w_pack, layer, seq):
    t, d = x2.shape
    tm = 256
    per_b = seq // tm
    widths = (W_A, W_B, W_I, W_C, W_M)
    dtypes = (F32, BF16, F32, F32, BF16)
    return pl.pallas_call(
        _inproj_kernel,
        grid=(t // tm,),
        in_specs=[
            pl.BlockSpec((tm, d), lambda i: (i, 0)),
            pl.BlockSpec((1, d), lambda i: (0, 0)),
            pl.BlockSpec((1, 1, d), lambda i: (i // per_b, 0, 0)),
            pl.BlockSpec((1, 1, d), lambda i: (i // per_b, 0, 0)),
            pl.BlockSpec((None, d, W_PACK), lambda i: (layer, 0, 0), pipeline_mode=pl.Buffered(1)),
        ],
        out_specs=[pl.BlockSpec((tm, w), lambda i: (i, 0)) for w in widths],
        out_shape=[jax.ShapeDtypeStruct((t, w), dt) for w, dt in zip(widths, dtypes)],
        compiler_params=_params("parallel"),
        name="inproj",
    )(x2, gain, shift, scale, w_pack)


def _group_row_bcast(x, group, row):
    n, w = x.shape
    if group >= SUBLANES:
        x3 = x.reshape(n // group, group, w)
        return jnp.broadcast_to(x3[:, row:row + 1, :], x3.shape).reshape(n, w)
    x3 = x.reshape(n // SUBLANES, SUBLANES, w)
    sub = lax.broadcasted_iota(I32, x3.shape, 1)
    n_groups = SUBLANES // group
    res = None
    for g in reversed(range(n_groups)):
        r = g * group + row
        bc = jnp.broadcast_to(x3[:, r:r + 1, :], x3.shape)
        res = bc if res is None else jnp.where(sub < (g + 1) * group, bc, res)
    return res.reshape(n, w)


def _scan_kernel(mode, layer, *refs):
    if mode == "hgrn":
        q_ref, f_ref, v_ref, g_ref, lbl_ref, gain_ref, lv_ref, tri_ref, y_ref, st_ref = refs
    else:
        q_ref, k_ref, v_ref, g_ref, code_ref, gup_ref, gb_ref, gain_ref, lv_ref, tri_ref, y_ref, st_ref = refs
    lt = q_ref.shape[0]
    n_levels = lt.bit_length() - 1

    @pl.when(pl.program_id(1) == 0)
    def _():
        st_ref[...] = jnp.zeros_like(st_ref)

    tri = tri_ref[...]
    lv = lv_ref[...]
    for h in range(HEADS):
        cols = slice(h * HEAD_DIM, (h + 1) * HEAD_DIM)
        if mode == "hgrn":
            lbl = lbl_ref[:, cols]
            e = jnp.exp(lbl - jnp.max(lbl, axis=0, keepdims=True))
            p = e / jnp.sum(e, axis=0, keepdims=True)
            lb = jnp.zeros((1, p.shape[1]), F32)
            for l2 in range(1, layer + 1):
                lb = lb + p[l2:l2 + 1, :]
            f = lb + (1.0 - lb) * jax.nn.sigmoid(f_ref[:, cols])
            lg = jnp.log(f)
            kk = 1.0 - f
            q = q_ref[:, cols] * (HEAD_DIM ** -0.5)
        else:
            c_hi, c_lo = _split2(code_ref[...])
            u_hi, u_lo = _split2(gup_ref[:, cols])
            z = _dot(c_hi, u_hi) + _dot(c_hi, u_lo) + _dot(c_lo, u_hi) + gb_ref[:, cols]
            lg = (jnp.minimum(z, 0.0) - jnp.log1p(jnp.exp(-jnp.abs(z)))) * (1.0 / GLA_TAU)
            kk = k_ref[:, cols]
            q = q_ref[:, cols] * (GLA_DK ** -0.5)

        g_hi, g_mid, g_lo = _split3(lg)
        b = _dot(tri, g_hi) + _dot(tri, g_mid) + _dot(tri, g_lo)

        s = jnp.where(lv == -1, _dot_nt(q.astype(BF16), kk.astype(BF16)), 0.0)
        for l in range(n_levels):
            n = 1 << l
            decay = jnp.exp(-jnp.abs(b - _group_row_bcast(b, 2 * n, n - 1)))
            s = jnp.where(lv == l, _dot_nt((q * decay).astype(BF16), (kk * decay).astype(BF16)), s)

        vb = v_ref[:, cols].astype(BF16)
        st = st_ref[h]
        o = _dot(s.astype(BF16), vb) + _dot_nt((q * jnp.exp(b)).astype(BF16), st.astype(BF16))
        b_last = b[lt - 1:lt, :]
        kd = (kk * jnp.exp(b_last - b)).astype(BF16)
        st_ref[h] = st * jnp.exp(b_last) + _dot_tn(vb, kd)

        ms = jnp.mean(o * o, axis=-1, keepdims=True)
        g = g_ref[:, cols]
        y = o * lax.rsqrt(ms + EPS) * gain_ref[...] * (g * jax.nn.sigmoid(g))
        y_ref[:, cols] = y.astype(y_ref.dtype)


def _scan_consts(lt):
    idx = np.arange(lt)
    x = idx[:, None] ^ idx[None, :]
    lvl = np.floor(np.log2(np.maximum(x, 1))).astype(np.int32)
    lv = np.where(idx[None, :] < idx[:, None], lvl, np.where(x == 0, -1, -2)).astype(np.int32)
    tri = (idx[None, :] <= idx[:, None]).astype(np.float32)
    return jnp.asarray(lv), jnp.asarray(tri, dtype=BF16)


def _scan(mode, layer, z, extra, out_gain, bsz, seq):
    t = z.shape[0]
    lt = 256
    per_b = seq // lt
    lv, tri = _scan_consts(lt)

    def col(cb):
        return pl.BlockSpec((lt, BRANCH_WIDTH), lambda b, c: (b * per_b + c, cb))

    const2 = lambda b, c: (0, 0)
    in_specs = [col(0), col(1), col(2), col(3)]
    args = [z, z, z, z]
    if mode == "hgrn":
        (lb_logits,) = extra
        in_specs.append(pl.BlockSpec(lb_logits.shape, const2))
        args.append(lb_logits)
    else:
        gup, gb = extra
        in_specs += [pl.BlockSpec((lt, LANES), lambda b, c: (b * per_b + c, 4 * HEADS)),
                     pl.BlockSpec(gup.shape, const2), pl.BlockSpec(gb.shape, const2)]
        args += [z, gup, gb]
    in_specs += [pl.BlockSpec((1, HEAD_DIM), const2), pl.BlockSpec((lt, lt), const2), pl.BlockSpec((lt, lt), const2)]
    args += [out_gain, lv, tri]
    return pl.pallas_call(
        functools.partial(_scan_kernel, mode, layer),
        grid=(bsz, per_b),
        in_specs=in_specs,
        out_specs=pl.BlockSpec((lt, BRANCH_WIDTH), lambda b, c: (b * per_b + c, 0)),
        out_shape=jax.ShapeDtypeStruct((t, BRANCH_WIDTH), BF16),
        scratch_shapes=[pltpu.VMEM((HEADS, HEAD_DIM, HEAD_DIM), F32)],
        compiler_params=_params("parallel", "arbitrary"),
        name="scan_" + mode,
    )(*args)


_KCH = 256


def _dsa_kernel(topk, q_ref, k_ref, v_ref, ziq_ref, zik_ref, qg_ref, kg_ref, stri_ref, o_ref,
                kn_ref, vt_ref, kih_ref, kil_ref, keys_ref, bias_ref, acc_ref):
    j = pl.program_id(1)
    seq = k_ref.shape[0]

    @pl.when(j == 0)
    def _prep():
        def body(c, carry):
            r0 = pl.multiple_of(c * _KCH, _KCH)
            kc = k_ref[pl.ds(r0, _KCH), :].astype(F32)
            for h in range(HEADS):
                kh = kc[:, h * HEAD_DIM:(h + 1) * HEAD_DIM]
                ms = jnp.mean(kh * kh, axis=-1, keepdims=True)
                kn_ref[pl.ds(r0, _KCH), h * HEAD_DIM:(h + 1) * HEAD_DIM] = (
                    kh * lax.rsqrt(ms + EPS) * kg_ref[...]).astype(BF16)
            vt_ref[c] = v_ref[pl.ds(r0, _KCH), :].astype(F32).T.astype(BF16)
            ki = zik_ref[pl.ds(r0, _KCH), :]
            hi = ki.astype(BF16)
            kih_ref[pl.ds(r0, _KCH), :] = hi
            kil_ref[pl.ds(r0, _KCH), :] = (ki - hi.astype(F32)).astype(BF16)
            return carry
        lax.fori_loop(0, seq // _KCH, body, 0)

    n_ch = (j * QBLOCK + QBLOCK + _KCH - 1) // _KCH
    ziq = ziq_ref[...]
    w_t = ziq[:, 2 * LANES:3 * LANES].T
    qi = jnp.concatenate([ziq[:, h * IDX_DIM:(h + 1) * IDX_DIM] for h in range(IDX_HEADS)], axis=0)
    qi_hi, qi_lo = _split2(qi * (IDX_DIM ** -0.5))
    w_rows = [w_t[IDX_DIM + h:IDX_DIM + h + 1, :] * (IDX_HEADS ** -0.5) for h in range(IDX_HEADS)]
    q_pos = j * QBLOCK + lax.broadcasted_iota(I32, (_KCH, QBLOCK), 1)
    row_i = lax.broadcasted_iota(I32, (_KCH, QBLOCK), 0)

    def idx_body(c, carry):
        r0 = pl.multiple_of(c * _KCH, _KCH)
        k_hi = kih_ref[pl.ds(r0, _KCH), :][:, :IDX_DIM]
        k_lo = kil_ref[pl.ds(r0, _KCH), :][:, :IDX_DIM]
        logit = _dot_nt(k_hi, qi_hi) + _dot_nt(k_hi, qi_lo) + _dot_nt(k_lo, qi_hi)
        score = jnp.zeros((_KCH, QBLOCK), F32)
        for h in range(IDX_HEADS):
            score = score + w_rows[h] * jnp.maximum(logit[:, h * QBLOCK:(h + 1) * QBLOCK], 0.0)
        keys_ref[pl.ds(r0, _KCH), :] = jnp.where(r0 + row_i <= q_pos, score, -jnp.inf)
        return carry
    lax.fori_loop(0, n_ch, idx_body, 0)

    def reduce_keys(fn, combine, init):
        def body(c, acc):
            r0 = pl.multiple_of(c * _KCH, _KCH)
            v = fn(keys_ref[pl.ds(r0, _KCH), :]).reshape(_KCH // SUBLANES, SUBLANES, QBLOCK)
            return combine(acc, v)
        return lax.fori_loop(0, n_ch, body, jnp.full((SUBLANES, QBLOCK), init, F32))

    def count(pred):
        acc = reduce_keys(lambda x: jnp.where(pred(x), 1.0, 0.0), lambda a, v: a + jnp.sum(v, axis=0), 0.0)
        return jnp.sum(acc, axis=0, keepdims=True)

    kf = float(topk)
    has_k = (q_pos[0:1, :] + 1).astype(F32) >= kf
    mx = jnp.max(reduce_keys(lambda x: x, lambda a, v: jnp.maximum(a, jnp.max(v, axis=0)), -jnp.inf),
                 axis=0, keepdims=True)
    mn = jnp.min(reduce_keys(lambda x: jnp.where(x == -jnp.inf, jnp.inf, x),
                             lambda a, v: jnp.minimum(a, jnp.min(v, axis=0)), jnp.inf), axis=0, keepdims=True)

    def bisect(_, bracket):
        lo, hi = bracket
        mid = lo + 0.5 * (hi - lo)
        up = count(lambda x: x >= mid) >= kf
        return jnp.where(up, mid, lo), jnp.where(up, hi, mid)

    def kth_of(bracket):
        hi = bracket[1]
        below = reduce_keys(lambda x: jnp.where(x < hi, x, -jnp.inf),
                            lambda a, v: jnp.maximum(a, jnp.max(v, axis=0)), -jnp.inf)
        return jnp.max(below, axis=0, keepdims=True)

    def refine(state):
        bracket = lax.fori_loop(0, 8, bisect, state[0])
        t = kth_of(bracket)
        unsettled = jnp.where(jnp.logical_and(has_k, count(lambda x: x >= t) < kf), 1.0, 0.0)
        return bracket, t, jnp.max(unsettled), state[3] + 1

    bracket0 = lax.fori_loop(0, 16, bisect, (mn, mx + (mx - mn) + 1.0))
    _, thr, _, _ = lax.while_loop(lambda st: jnp.logical_and(st[2] > 0.0, st[3] < 20), refine,
                                  refine((bracket0, mn, jnp.float32(1.0), jnp.int32(0))))
    thr = jnp.where(has_k, thr, -jnp.inf)
    need = kf - count(lambda x: x > thr)

    def sel_body(c, seen):
        r0 = pl.multiple_of(c * _KCH, _KCH)
        x = keys_ref[pl.ds(r0, _KCH), :]
        eq = jnp.where(x == thr, 1.0, 0.0)
        rank = seen + _dot(stri_ref[...], eq.astype(BF16))
        tie_ok = jnp.where(x == thr, jnp.where(rank < need, 0.0, NEG_BIG), NEG_BIG)
        bias = jnp.where(x > thr, 0.0, tie_ok)
        bias_ref[pl.ds(r0, _KCH), :] = jnp.where(x == -jnp.inf, NEG_BIG, bias)
        return seen + jnp.sum(eq, axis=0, keepdims=True)
    lax.fori_loop(0, n_ch, sel_body, jnp.zeros((1, QBLOCK), F32))

    qf = q_ref[...].astype(F32)
    qn = []
    for h in range(HEADS):
        qh = qf[:, h * HEAD_DIM:(h + 1) * HEAD_DIM]
        ms = jnp.mean(qh * qh, axis=-1, keepdims=True)
        qn.append((qh * lax.rsqrt(ms + EPS) * qg_ref[...] * (HEAD_DIM ** -0.5)).astype(BF16))
    acc_ref[...] = jnp.zeros_like(acc_ref)

    @pl.when(n_ch % 2 == 1)
    def _():
        bias_ref[pl.ds(pl.multiple_of(n_ch * _KCH, _KCH), _KCH), :] = jnp.full((_KCH, QBLOCK), NEG_BIG, F32)

    def att_body(c, carry):
        r0 = pl.multiple_of(c * (2 * _KCH), 2 * _KCH)
        bias = bias_ref[pl.ds(r0, 2 * _KCH), :]
        out = []
        for h in range(HEADS):
            m, l = carry[h]
            hd = slice(h * HEAD_DIM, (h + 1) * HEAD_DIM)
            sc = _dot_nt(kn_ref[pl.ds(r0, 2 * _KCH), hd], qn[h]) + bias
            m_new = jnp.maximum(m, jnp.max(sc, axis=0, keepdims=True))
            p = jnp.exp(sc - m_new)
            alpha = jnp.exp(m - m_new)
            pb = p.astype(BF16)
            pv = _dot(vt_ref[2 * c, hd, :], pb[:_KCH]) + _dot(vt_ref[2 * c + 1, hd, :], pb[_KCH:])
            acc_ref[h] = alpha * acc_ref[h] + pv
            out.append((m_new, alpha * l + jnp.sum(p, axis=0, keepdims=True)))
        return tuple(out)
    init = tuple((jnp.full((1, QBLOCK), NEG_BIG, F32), jnp.zeros((1, QBLOCK), F32)) for _ in range(HEADS))
    stats = lax.fori_loop(0, (n_ch + 1) // 2, att_body, init)
    for h in range(HEADS):
        o_ref[:, h * HEAD_DIM:(h + 1) * HEAD_DIM] = (acc_ref[h] / stats[h][1]).T.astype(o_ref.dtype)


def _dsa(zb, zi, q_gain, k_gain, bsz, seq):
    t = zb.shape[0]
    topk = min(DSA_MAX_TOPK, seq // 4)
    nq = seq // QBLOCK
    assert seq % (2 * _KCH) == 0
    idx = np.arange(_KCH)
    stri = jnp.asarray((idx[None, :] < idx[:, None]).astype(np.float32), dtype=BF16)
    const2 = lambda b, j: (0, 0)
    return pl.pallas_call(
        functools.partial(_dsa_kernel, topk),
        grid=(bsz, nq),
        in_specs=[
            pl.BlockSpec((QBLOCK, BRANCH_WIDTH), lambda b, j: (b * nq + j, 0)),
            pl.BlockSpec((seq, BRANCH_WIDTH), lambda b, j: (b, 1)),
            pl.BlockSpec((seq, BRANCH_WIDTH), lambda b, j: (b, 2)),
            pl.BlockSpec((QBLOCK, W_I), lambda b, j: (b * nq + j, 0)),
            pl.BlockSpec((seq, LANES), lambda b, j: (b, 2)),
            pl.BlockSpec((1, HEAD_DIM), const2),
            pl.BlockSpec((1, HEAD_DIM), const2),
            pl.BlockSpec((_KCH, _KCH), const2),
        ],
        out_specs=pl.BlockSpec((QBLOCK, BRANCH_WIDTH), lambda b, j: (b * nq + j, 0)),
        out_shape=jax.ShapeDtypeStruct((t, BRANCH_WIDTH), BF16),
        scratch_shapes=[
            pltpu.VMEM((seq, BRANCH_WIDTH), BF16),
            pltpu.VMEM((seq // _KCH, BRANCH_WIDTH, _KCH), BF16),
            pltpu.VMEM((seq, LANES), BF16),
            pltpu.VMEM((seq, LANES), BF16),
            pltpu.VMEM((seq, QBLOCK), F32),
            pltpu.VMEM((seq, QBLOCK), F32),
            pltpu.VMEM((HEADS, HEAD_DIM, QBLOCK), F32),
        ],
        compiler_params=_params("parallel", "arbitrary"),
        name="dsa",
    )(zb, zb, zb, zi, zi, q_gain, k_gain, stri)


def _merge_kernel(ya_ref, yb_ref, yc_ref, zm_ref, x_ref, gm_ref, wb_ref, wo_ref, o_ref):
    d = x_ref.shape[1]
    mixed = None
    for g, y_ref in enumerate((ya_ref, yb_ref, yc_ref)):
        up = _dot(y_ref[...], wb_ref[g])
        term = jax.nn.sigmoid(zm_ref[:, g * d:(g + 1) * d].astype(F32)) * up
        mixed = term if mixed is None else mixed + term
    o_ref[...] = x_ref[...] + gm_ref[0] * _dot(mixed.astype(BF16), wo_ref[...])


def _merge(ya, yb, yc, zm, x2, gate_m, w_branch, w_out, seq):
    t, d = x2.shape
    tm = 256
    per_b = seq // tm
    row = lambda i: (i, 0)
    return pl.pallas_call(
        _merge_kernel,
        grid=(t // tm,),
        in_specs=[
            pl.BlockSpec((tm, BRANCH_WIDTH), row), pl.BlockSpec((tm, BRANCH_WIDTH), row),
            pl.BlockSpec((tm, BRANCH_WIDTH), row), pl.BlockSpec((tm, W_M), row), pl.BlockSpec((tm, d), row),
            pl.BlockSpec((1, 1, d), lambda i: (i // per_b, 0, 0)),
            pl.BlockSpec(w_branch.shape, lambda i: (0, 0, 0)),
            pl.BlockSpec(w_out.shape, lambda i: (0, 0)),
        ],
        out_specs=pl.BlockSpec((tm, d), row),
        out_shape=jax.ShapeDtypeStruct((t, d), F32),
        compiler_params=_params("parallel"),
        name="merge",
    )(ya, yb, yc, zm, x2, gate_m, w_branch, w_out)


_CAND_ROWS = 72
_ROUTE_GROUP = 4


def _cand_layout():
    flat = np.zeros((_CAND_ROWS,), np.int32)
    live = np.zeros((_CAND_ROWS,), bool)
    r = 0
    for b in range(16):
        flat[r], live[r] = b, True
        r += 1
    for a in (1, 2, 3):
        for b in range(8):
            flat[r], live[r] = a * 16 + b, True
            r += 1
    for a in range(16):
        flat[r], live[r] = a * 16, a >= 4
        r += 1
    for a in range(8):
        flat[r], live[r] = a * 16 + 1, a >= 4
        r += 1
    for a in range(8):
        flat[r], live[r] = a * 16 + 2, a == 4
        r += 1
    flat = np.where(live, flat, 1000 + np.arange(_CAND_ROWS))
    return flat.astype(np.int32), live


def _candidates(v1, v2, live):
    cand = jnp.concatenate(
        [v1[0:1, :] + v2] + [v1[a:a + 1, :] + v2[0:8, :] for a in (1, 2, 3)]
        + [v1 + v2[0:1, :], v1[0:8, :] + v2[1:2, :], v1[0:8, :] + v2[2:3, :]], axis=0)
    return jnp.where(live, cand, -jnp.inf)


def _gone(work):
    return jnp.sum(jnp.where(work == -jnp.inf, 1.0, 0.0), axis=0, keepdims=True)


def _route_head_exact(s1, s2, row_i, flat, live, v1_ref, v2_ref, b2_ref, th_ref, x2_ref):
    ranks = []
    for s, vals_ref in ((s1, v1_ref), (s2, v2_ref)):
        work = s
        rank = jnp.full(s.shape, 127.0, F32)
        for a in range(PEER_TOPK):
            m = jnp.max(work, axis=0, keepdims=True)
            idx = jnp.min(jnp.where(work == m, row_i, N_KEYS), axis=0, keepdims=True)
            hit = row_i == idx
            work = jnp.where(hit, -jnp.inf, work)
            rank = jnp.where(hit, float(a), rank)
            vals_ref[a:a + 1, :] = m
        ranks.append(rank)
    r1, r2 = ranks
    v1 = v1_ref[...]
    v2 = v2_ref[...]
    cand = _candidates(v1, v2, live)
    sel = jnp.zeros(cand.shape, F32)
    work = cand
    for _ in range(PEER_TOPK):
        m = jnp.max(work, axis=0, keepdims=True)
        fi = jnp.min(jnp.where(work == m, flat, 9999), axis=0, keepdims=True)
        hit = flat == fi
        work = jnp.where(hit, -jnp.inf, work)
        sel = jnp.where(hit, 1.0, sel)
    z = jnp.sum(sel * jnp.exp(cand - (v1[0:1, :] + v2[0:1, :])), axis=0, keepdims=True)
    b2_ref[...] = jnp.exp(s2 - v2[0:1, :]) * (0.5 / z)

    zeros8 = jnp.zeros((8, s1.shape[1]), F32)
    cnt_hi = (sel[40:56, :] + jnp.concatenate([sel[56:64, :], zeros8], axis=0)
              + jnp.concatenate([sel[64:72, :], zeros8], axis=0))
    cnt_lo = [jnp.sum(sel[0:16, :], axis=0, keepdims=True)] + [
        jnp.sum(sel[16 + 8 * (a - 1):24 + 8 * (a - 1), :], axis=0, keepdims=True) for a in (1, 2, 3)]
    cnt = jnp.zeros(s1.shape, F32)
    for a in range(PEER_TOPK):
        ca = cnt_lo[a] if a < 4 else cnt_hi[a:a + 1, :]
        cnt = jnp.where(r1 == float(a), ca, cnt)
    th_ref[...] = 0.5 - cnt
    x2_ref[...] = -r2


def _route_kernel(x_ref, gain_ref, shift_ref, scale_ref, wq_ref, sk_ref, flat_ref, live_ref,
                  ht_ref, a1_ref, th_ref, b2_ref, x2_ref, qt_ref, v1_ref, v2_ref):
    tt = x_ref.shape[0]
    h2 = _modulate(x_ref[...], gain_ref[...], shift_ref[0], scale_ref[0])
    ht = h2.T.astype(BF16)
    for g in range(tt // _TG):
        ht_ref[g] = ht[:, g * _TG:(g + 1) * _TG]
    qt_ref[...] = _dot(wq_ref[...], ht)
    row_i = lax.broadcasted_iota(I32, (N_KEYS, tt), 0)
    flat = flat_ref[...]
    live = live_ref[...] > 0.0
    n_dead = float(_CAND_ROWS - int(_cand_layout()[1].sum()))

    group = v1_ref.shape[0]
    for h0 in range(0, PEER_HEADS, group):
        heads = range(h0, h0 + group)
        s1 = [_dot(sk_ref[2 * h], qt_ref[(2 * h) * N_KEYS:(2 * h + 1) * N_KEYS, :].astype(BF16)) for h in heads]
        s2 = [_dot(sk_ref[2 * h + 1], qt_ref[(2 * h + 1) * N_KEYS:(2 * h + 2) * N_KEYS, :].astype(BF16))
              for h in heads]

        work = [list(s1), list(s2)]
        for a in range(PEER_TOPK):
            for half, vals_ref in enumerate((v1_ref, v2_ref)):
                for g in range(group):
                    m = jnp.max(work[half][g], axis=0, keepdims=True)
                    work[half][g] = jnp.where(work[half][g] == m, -jnp.inf, work[half][g])
                    vals_ref[g, a:a + 1, :] = m
        v1 = [v1_ref[g] for g in range(group)]
        v2 = [v2_ref[g] for g in range(group)]
        cand = [_candidates(v1[g], v2[g], live) for g in range(group)]
        wc = list(cand)
        t16 = [None] * group
        for _ in range(PEER_TOPK):
            for g in range(group):
                t16[g] = jnp.max(wc[g], axis=0, keepdims=True)
                wc[g] = jnp.where(wc[g] == t16[g], -jnp.inf, wc[g])
        bad = []
        last = PEER_TOPK - 1
        for g, h in enumerate(heads):
            t17 = jnp.max(wc[g], axis=0, keepdims=True)
            top = v1[g][0:1, :] + v2[g][0:1, :]
            z = jnp.sum(jnp.where(cand[g] >= t16[g], jnp.exp(cand[g] - top), 0.0), axis=0, keepdims=True)
            a1_ref[h] = jnp.exp(s1[g] - v1[g][0:1, :])
            b2_ref[h] = jnp.exp(s2[g] - v2[g][0:1, :]) * (0.5 / z)
            th_ref[h] = jnp.where(s1[g] >= v1[g][last:last + 1, :], 0.5 * (t16[g] + t17) - s1[g], jnp.inf)
            x2_ref[h] = jnp.where(s2[g] >= v2[g][last:last + 1, :], s2[g], -jnp.inf)
            scale = (jnp.abs(v1[g][0:1, :]) + jnp.abs(v1[g][last:last + 1, :])
                     + jnp.abs(v2[g][0:1, :]) + jnp.abs(v2[g][last:last + 1, :]))
            close = jnp.where(t16[g] - t17 > scale * 2.0 ** -22, 0.0, 1.0)
            bad.append(jnp.max(jnp.abs(_gone(work[0][g]) - PEER_TOPK) + jnp.abs(_gone(work[1][g]) - PEER_TOPK)
                               + jnp.abs(_gone(wc[g]) - (PEER_TOPK + n_dead)) + close) > 0.0)

        for g, h in enumerate(heads):
            @pl.when(bad[g])
            def _(g=g, h=h):
                _route_head_exact(s1[g], s2[g], row_i, flat, live, v1_ref.at[g], v2_ref.at[g],
                                  b2_ref.at[h], th_ref.at[h], x2_ref.at[h])


def _route(x2, gain, shift, scale, wq_t, sub_keys, seq):
    t, d = x2.shape
    tt = 256
    per_b = seq // tt
    flat_np, live_np = _cand_layout()
    flat = jnp.asarray(np.broadcast_to(flat_np[:, None], (_CAND_ROWS, tt)).copy())
    live = jnp.asarray(np.broadcast_to(live_np[:, None], (_CAND_ROWS, tt)).astype(np.float32))
    dense = jax.ShapeDtypeStruct((PEER_HEADS, N_KEYS, t), F32)
    dense_spec = pl.BlockSpec((PEER_HEADS, N_KEYS, tt), lambda i: (0, 0, i))
    const2 = lambda i: (0, 0)
    return pl.pallas_call(
        _route_kernel,
        grid=(t // tt,),
        in_specs=[
            pl.BlockSpec((tt, d), lambda i: (i, 0)),
            pl.BlockSpec((1, d), const2),
            pl.BlockSpec((1, 1, d), lambda i: (i // per_b, 0, 0)),
            pl.BlockSpec((1, 1, d), lambda i: (i // per_b, 0, 0)),
            pl.BlockSpec(wq_t.shape, const2),
            pl.BlockSpec(sub_keys.shape, lambda i: (0, 0, 0)),
            pl.BlockSpec((_CAND_ROWS, tt), const2),
            pl.BlockSpec((_CAND_ROWS, tt), const2),
        ],
        out_specs=[pl.BlockSpec((tt // _TG, d, _TG), lambda i: (i, 0, 0)),
                   dense_spec, dense_spec, dense_spec, dense_spec],
        out_shape=[jax.ShapeDtypeStruct((t // _TG, d, _TG), BF16), dense, dense, dense, dense],
        scratch_shapes=[
            pltpu.VMEM((PEER_HEADS * 2 * N_KEYS, tt), F32),
            pltpu.VMEM((_ROUTE_GROUP, PEER_TOPK, tt), F32),
            pltpu.VMEM((_ROUTE_GROUP, PEER_TOPK, tt), F32),
        ],
        compiler_params=_params("parallel"),
        name="peer_route",
    )(x2, gain, shift, scale, wq_t, sub_keys, flat, live)


_JB = 4
_IB = 2
_TG = 256
_RB = _IB * N_KEYS


def _peer_kernel(n_tiles, ht_ref, u_ref, vt_ref, a1_ref, th_ref, b2_ref, x2_ref, x_ref, gf_ref, o_ref,
                 acc_ref, at_ref, gt_ref, bc_ref):
    s = pl.program_id(1)
    n_g, te, tg = at_ref.shape
    tt = n_g * tg
    n_i = te // N_KEYS
    n_jb = N_KEYS // (SUBLANES * _JB)
    assert n_i == SUBLANES
    assert _RB == _IB * N_KEYS

    def bcast_rows():
        i0 = pl.multiple_of(s * n_i, SUBLANES)
        for h in range(PEER_HEADS):
            a1g = a1_ref[h, pl.ds(i0, n_i), :]
            thg = th_ref[h, pl.ds(i0, n_i), :]
            for ii in range(n_i):
                bc_ref[2 * (h * n_i + ii)] = jnp.broadcast_to(a1g[ii:ii + 1, :], (SUBLANES, tt))
                bc_ref[2 * (h * n_i + ii) + 1] = jnp.broadcast_to(thg[ii:ii + 1, :], (SUBLANES, tt))

    def mask_block(tb, jb, ib):
        g, lane0 = divmod(tb * LANES, tg)
        lanes = slice(tb * LANES, (tb + 1) * LANES)
        lanes_g = slice(lane0, lane0 + LANES)
        j0 = jb * (SUBLANES * _JB)
        w = [[None] * _JB for _ in range(_IB)]
        for h in range(PEER_HEADS):
            x2 = [x2_ref[h, j0 + SUBLANES * k:j0 + SUBLANES * (k + 1), lanes] for k in range(_JB)]
            b2 = [b2_ref[h, j0 + SUBLANES * k:j0 + SUBLANES * (k + 1), lanes] for k in range(_JB)]
            for di in range(_IB):
                row = 2 * (h * n_i + ib * _IB + di)
                a1v = bc_ref[row, :, lanes]
                thv = bc_ref[row + 1, :, lanes]
                for k in range(_JB):
                    term = jnp.where(x2[k] >= thv, a1v * b2[k], 0.0)
                    w[di][k] = term if w[di][k] is None else w[di][k] + term
        for di in range(_IB):
            r0 = (ib * _IB + di) * N_KEYS + j0
            rows = slice(r0, r0 + SUBLANES * _JB)
            a = at_ref[g, rows, lanes_g]
            act2 = a + a * lax.erf(a * np.float32(np.sqrt(0.5)))
            gt_ref[g, rows, lanes_g] = (jnp.concatenate(w[di], axis=0) * act2).astype(BF16)

    @pl.when(s == 0)
    def _():
        acc_ref[...] = jnp.zeros_like(acc_ref)

    def pre_act(k):
        rows = slice(k * _RB, (k + 1) * _RB)
        for g in range(n_g):
            at_ref[g, rows, :] = _dot(u_ref[rows, :], ht_ref[g])

    def fold(k):
        rows = slice(k * _RB, (k + 1) * _RB)
        for g in range(n_g):
            acc_ref[g] += _dot(vt_ref[:, rows], gt_ref[g, rows, :])

    bcast_rows()
    n_slices = te // _RB
    pre_act(0)
    for k in range(n_slices):
        if k + 1 < n_slices:
            pre_act(k + 1)
        if k >= 1:
            fold(k - 1)
        for tb in range(tt // LANES):
            for jb in range(n_jb):
                mask_block(tb, jb, k)
    fold(n_slices - 1)

    @pl.when(s == n_tiles - 1)
    def _():
        for g in range(n_g):
            rows = slice(g * tg, (g + 1) * tg)
            o_ref[rows, :] = x_ref[rows, :] + gf_ref[0] * acc_ref[g].T


def _peer(ht, u, vt, a1, th, b2, xk, x2, gate_f, seq):
    t, d = x2.shape
    n_tiles = u.shape[0] // PEER_TE
    tt = 512
    per_b = seq // tt
    f32_spec = pl.BlockSpec((PEER_HEADS, N_KEYS, tt), lambda i, s: (0, 0, i))
    return pl.pallas_call(
        functools.partial(_peer_kernel, n_tiles),
        grid=(t // tt, n_tiles),
        in_specs=[
            pl.BlockSpec((tt // _TG, d, _TG), lambda i, s: (i, 0, 0)),
            pl.BlockSpec((PEER_TE, d), lambda i, s: (s, 0)),
            pl.BlockSpec((d, PEER_TE), lambda i, s: (0, s)),
            f32_spec, f32_spec, f32_spec, f32_spec,
            pl.BlockSpec((tt, d), lambda i, s: (i, 0)),
            pl.BlockSpec((1, 1, d), lambda i, s: (i // per_b, 0, 0)),
        ],
        out_specs=pl.BlockSpec((tt, d), lambda i, s: (i, 0)),
        out_shape=jax.ShapeDtypeStruct((t, d), F32),
        scratch_shapes=[
            pltpu.VMEM((tt // _TG, d, _TG), F32),
            pltpu.VMEM((tt // _TG, PEER_TE, _TG), F32),
            pltpu.VMEM((tt // _TG, PEER_TE, _TG), BF16),
            pltpu.VMEM((2 * PEER_HEADS * PEER_TE // N_KEYS, SUBLANES, tt), F32),
        ],
        compiler_params=_params("parallel", "arbitrary"),
        name="peer_experts",
    )(ht, u, vt, a1, th, b2, xk, x2, gate_f)


def _tcast_kernel(x_ref, o_ref):
    o_ref[...] = x_ref[...].T.astype(o_ref.dtype)


def _cast_kernel(x_ref, o_ref):
    o_ref[...] = x_ref[...].astype(o_ref.dtype)


def _layer_cast(x, layer, dtype, transpose):
    _, r, c = x.shape
    tr, tc = min(r, 1024), min(c, 1024)
    return pl.pallas_call(
        _tcast_kernel if transpose else _cast_kernel,
        grid=(r // tr, c // tc),
        in_specs=[pl.BlockSpec((None, tr, tc), lambda i, j: (layer, i, j))],
        out_specs=pl.BlockSpec((tc, tr), lambda i, j: (j, i)) if transpose else pl.BlockSpec((tr, tc), lambda i, j: (i, j)),
        out_shape=jax.ShapeDtypeStruct((c, r) if transpose else (r, c), dtype),
        compiler_params=_params("parallel", "parallel"),
        name="layer_cast",
    )(x)


def _pad_heads(w, width):
    d = w.shape[0]
    w = w.reshape(d, HEADS, width)
    return jnp.pad(w, ((0, 0), (0, 0), (0, HEAD_DIM - width))).reshape(d, HEADS * HEAD_DIM)


def _pack_segments():
    segs = [(0, W_A + W_B, 0)]
    n_idx = IDX_HEADS * IDX_DIM + IDX_DIM + IDX_HEADS
    src = W_A + W_B
    dst = W_A + W_B
    segs.append((src, n_idx, dst)); src += n_idx; dst += W_I
    for _ in range(2):
        for h in range(HEADS):
            segs.append((src + GLA_DK * h, GLA_DK, dst + HEAD_DIM * h))
        src += HEADS * GLA_DK; dst += BRANCH_WIDTH
    segs.append((src, BRANCH_WIDTH, dst)); src += BRANCH_WIDTH; dst += BRANCH_WIDTH
    code_src = src; src += GLA_GATE_RANK
    segs.append((src, BRANCH_WIDTH, dst)); src += BRANCH_WIDTH; dst += BRANCH_WIDTH
    segs.append((code_src, GLA_GATE_RANK, dst)); dst += LANES
    segs.append((src, W_M, dst))
    assert dst + W_M == W_PACK
    return segs


def _pack_kernel(w_ref, o_ref):
    o_ref[...] = jnp.zeros_like(o_ref)
    for src, width, dst in _pack_segments():
        o_ref[0, :, dst:dst + width] = w_ref[0, :, src:src + width].astype(o_ref.dtype)


def _pack_w_in(w_in):
    depth, d, n_in = w_in.shape
    tr = 128
    return pl.pallas_call(
        _pack_kernel,
        grid=(depth, d // tr),
        in_specs=[pl.BlockSpec((1, tr, n_in), lambda l, i: (l, i, 0))],
        out_specs=pl.BlockSpec((1, tr, W_PACK), lambda l, i: (l, i, 0)),
        out_shape=jax.ShapeDtypeStruct((depth, d, W_PACK), BF16),
        compiler_params=_params("parallel", "parallel"),
        name="pack_w_in",
    )(w_in)


def kernel(x, c, w_ada, b_ada, norm_mix, norm_ffn, w_in, hgrn_lb_logits, hgrn_out_norm, dsa_q_norm, dsa_k_norm,
           gla_gate_up, gla_gate_bias, gla_out_norm, w_branch, w_out, peer_w_query, peer_sub_keys, peer_u, peer_v):
    bsz, seq, d = x.shape
    depth = w_in.shape[0]
    t = bsz * seq
    x2 = x.reshape(t, d)
    mod = _ada(c, w_ada, b_ada)
    w_pack = _pack_w_in(w_in)

    for l in range(depth):
        shift_m, scale_m, gate_m, shift_f, scale_f, gate_f = [
            mod[l, :, k * d:(k + 1) * d].reshape(bsz, 1, d) for k in range(6)]
        za, zb, zi, zc, zm = _inproj(x2, norm_mix[l].reshape(1, d), shift_m, scale_m, w_pack, l, seq)

        ya = _scan("hgrn", l, za, (hgrn_lb_logits,), hgrn_out_norm[l].reshape(1, HEAD_DIM), bsz, seq)
        yb = _dsa(zb, zi, dsa_q_norm[l].reshape(1, HEAD_DIM), dsa_k_norm[l].reshape(1, HEAD_DIM), bsz, seq)
        gup = jnp.pad(_pad_heads(gla_gate_up[l], GLA_DK), ((0, LANES - GLA_GATE_RANK), (0, 0)))
        gb = _pad_heads(gla_gate_bias[l].reshape(1, HEADS * GLA_DK), GLA_DK)
        yc = _scan("gla", l, zc, (gup, gb), gla_out_norm[l].reshape(1, HEAD_DIM), bsz, seq)

        x2 = _merge(ya, yb, yc, zm, x2, gate_m, w_branch[l].astype(BF16), w_out[l].astype(BF16), seq)

        ht, a1, th, b2, xk = _route(
            x2, norm_ffn[l].reshape(1, d), shift_f, scale_f, _layer_cast(peer_w_query, l, BF16, True),
            peer_sub_keys[l].reshape(PEER_HEADS * 2, N_KEYS, -1).astype(BF16), seq)
        x2 = _peer(ht, _layer_cast(peer_u, l, BF16, False), _layer_cast(peer_v, l, BF16, True),
                   a1, th, b2, xk, x2, gate_f, seq)

    return x2.reshape(bsz, seq, d)
```

```python
import functools

import numpy as np
import jax
import jax.numpy as jnp
from jax import lax
from jax.experimental import pallas as pl
from jax.experimental.pallas import tpu as pltpu

F32, BF16, I32 = jnp.float32, jnp.bfloat16, jnp.int32

D_MODEL = 1024
HEADS = 4
HEAD_DIM = 128
BRANCH_WIDTH = HEADS * HEAD_DIM
IDX_HEADS = 4
IDX_DIM = 64
DSA_MAX_TOPK = 256
QBLOCK = 256
GLA_DK = 64
GLA_GATE_RANK = 16
GLA_TAU = 16.0
PEER_HEADS = 8
N_KEYS = 128
PEER_TOPK = 16
PEER_TE = 1024
EPS = 1e-6

LANES = 128
SUBLANES = 8
VMEM_LIMIT_BYTES = 56 * 1024 * 1024

NEG_BIG = -1e30

W_A = 4 * BRANCH_WIDTH
W_B = 3 * BRANCH_WIDTH
W_I = 3 * LANES
W_C = 4 * BRANCH_WIDTH + LANES
W_M = 3 * D_MODEL
W_PACK = W_A + W_B + W_I + W_C + W_M


def _dot(a, b):
    return jnp.dot(a, b, preferred_element_type=F32)


def _dot_nt(a, b):
    return lax.dot_general(a, b, (((1,), (1,)), ((), ())), preferred_element_type=F32)


def _dot_tn(a, b):
    return lax.dot_general(a, b, (((0,), (0,)), ((), ())), preferred_element_type=F32)


def _split2(x):
    hi = x.astype(BF16)
    lo = (x - hi.astype(F32)).astype(BF16)
    return hi, lo


def _split3(x):
    hi = x.astype(BF16)
    r = x - hi.astype(F32)
    mid = r.astype(BF16)
    lo = (r - mid.astype(F32)).astype(BF16)
    return hi, mid, lo


def _params(*sem):
    return pltpu.CompilerParams(dimension_semantics=sem, vmem_limit_bytes=VMEM_LIMIT_BYTES)


def _modulate(x, gain, shift, scale):
    ms = jnp.mean(x * x, axis=-1, keepdims=True)
    return x * lax.rsqrt(ms + EPS) * gain * (1.0 + scale) + shift


def _ada_kernel(c_ref, w_ref, b_ref, o_ref):
    c = c_ref[...]
    sc = c * jax.nn.sigmoid(c)
    a_hi, a_lo = _split2(sc)
    w_hi, w_lo = _split2(w_ref[0])
    o_ref[0] = _dot(a_hi, w_hi) + _dot(a_hi, w_lo) + _dot(a_lo, w_hi) + b_ref[0]


def _ada(c, w_ada, b_ada):
    depth, d, n = w_ada.shape
    bsz = c.shape[0]
    tn = 1536
    return pl.pallas_call(
        _ada_kernel,
        grid=(depth, n // tn),
        in_specs=[
            pl.BlockSpec((bsz, d), lambda l, j: (0, 0)),
            pl.BlockSpec((1, d, tn), lambda l, j: (l, 0, j)),
            pl.BlockSpec((1, 1, tn), lambda l, j: (l, 0, j)),
        ],
        out_specs=pl.BlockSpec((1, bsz, tn), lambda l, j: (l, 0, j)),
        out_shape=jax.ShapeDtypeStruct((depth, bsz, n), F32),
        compiler_params=_params("parallel", "parallel"),
        name="ada",
    )(c, w_ada, b_ada.reshape(depth, 1, n))


def _inproj_kernel(x_ref, gain_ref, shift_ref, scale_ref, w_ref, *out_refs):
    h = _modulate(x_ref[...], gain_ref[...], shift_ref[0], scale_ref[0]).astype(BF16)
    off = 0
    for o_ref in out_refs:
        width = o_ref.shape[1]
        for c0 in range(0, width, 512):
            c1 = min(c0 + 512, width)
            o_ref[:, c0:c1] = _dot(h, w_ref[:, off + c0:off + c1]).astype(o_ref.dtype)
        off += width


def _inproj(x2, gain, shift, scale, w_pack, layer, seq):
    t, d = x2.shape
    tm = 256
    per_b = seq // tm
    widths = (W_A, W_B, W_I, W_C, W_M)
    dtypes = (F32, BF16, F32, F32, BF16)
    return pl.pallas_call(
        _inproj_kernel,
        grid=(t // tm,),
        in_specs=[
            pl.BlockSpec((tm, d), lambda i: (i, 0)),
            pl.BlockSpec((1, d), lambda i: (0, 0)),
            pl.BlockSpec((1, 1, d), lambda i: (i // per_b, 0, 0)),
            pl.BlockSpec((1, 1, d), lambda i: (i // per_b, 0, 0)),
            pl.BlockSpec((None, d, W_PACK), lambda i: (layer, 0, 0), pipeline_mode=pl.Buffered(1)),
        ],
        out_specs=[pl.BlockSpec((tm, w), lambda i: (i, 0)) for w in widths],
        out_shape=[jax.ShapeDtypeStruct((t, w), dt) for w, dt in zip(widths, dtypes)],
        compiler_params=_params("parallel"),
        name="inproj",
    )(x2, gain, shift, scale, w_pack)


def _group_row_bcast(x, group, row):
    n, w = x.shape
    if group >= SUBLANES:
        x3 = x.reshape(n // group, group, w)
        return jnp.broadcast_to(x3[:, row:row + 1, :], x3.shape).reshape(n, w)
    x3 = x.reshape(n // SUBLANES, SUBLANES, w)
    sub = lax.broadcasted_iota(I32, x3.shape, 1)
    n_groups = SUBLANES // group
    res = None
    for g in reversed(range(n_groups)):
        r = g * group + row
        bc = jnp.broadcast_to(x3[:, r:r + 1, :], x3.shape)
        res = bc if res is None else jnp.where(sub < (g + 1) * group, bc, res)
    return res.reshape(n, w)


def _scan_kernel(mode, layer, *refs):
    if mode == "hgrn":
        q_ref, f_ref, v_ref, g_ref, lbl_ref, gain_ref, lv_ref, tri_ref, y_ref, st_ref = refs
    else:
        q_ref, k_ref, v_ref, g_ref, code_ref, gup_ref, gb_ref, gain_ref, lv_ref, tri_ref, y_ref, st_ref = refs
    lt = q_ref.shape[0]
    n_levels = lt.bit_length() - 1

    @pl.when(pl.program_id(1) == 0)
    def _():
        st_ref[...] = jnp.zeros_like(st_ref)

    tri = tri_ref[...]
    lv = lv_ref[...]
    for h in range(HEADS):
        cols = slice(h * HEAD_DIM, (h + 1) * HEAD_DIM)
        if mode == "hgrn":
            lbl = lbl_ref[:, cols]
            e = jnp.exp(lbl - jnp.max(lbl, axis=0, keepdims=True))
            p = e / jnp.sum(e, axis=0, keepdims=True)
            lb = jnp.zeros((1, p.shape[1]), F32)
            for l2 in range(1, layer + 1):
                lb = lb + p[l2:l2 + 1, :]
            f = lb + (1.0 - lb) * jax.nn.sigmoid(f_ref[:, cols])
            lg = jnp.log(f)
            kk = 1.0 - f
            q = q_ref[:, cols] * (HEAD_DIM ** -0.5)
        else:
            c_hi, c_lo = _split2(code_ref[...])
            u_hi, u_lo = _split2(gup_ref[:, cols])
            z = _dot(c_hi, u_hi) + _dot(c_hi, u_lo) + _dot(c_lo, u_hi) + gb_ref[:, cols]
            lg = (jnp.minimum(z, 0.0) - jnp.log1p(jnp.exp(-jnp.abs(z)))) * (1.0 / GLA_TAU)
            kk = k_ref[:, cols]
            q = q_ref[:, cols] * (GLA_DK ** -0.5)

        g_hi, g_mid, g_lo = _split3(lg)
        b = _dot(tri, g_hi) + _dot(tri, g_mid) + _dot(tri, g_lo)

        s = jnp.where(lv == -1, _dot_nt(q.astype(BF16), kk.astype(BF16)), 0.0)
        for l in range(n_levels):
            n = 1 << l
            decay = jnp.exp(-jnp.abs(b - _group_row_bcast(b, 2 * n, n - 1)))
            s = jnp.where(lv == l, _dot_nt((q * decay).astype(BF16), (kk * decay).astype(BF16)), s)

        vb = v_ref[:, cols].astype(BF16)
        st = st_ref[h]
        o = _dot(s.astype(BF16), vb) + _dot_nt((q * jnp.exp(b)).astype(BF16), st.astype(BF16))
        b_last = b[lt - 1:lt, :]
        kd = (kk * jnp.exp(b_last - b)).astype(BF16)
        st_ref[h] = st * jnp.exp(b_last) + _dot_tn(vb, kd)

        ms = jnp.mean(o * o, axis=-1, keepdims=True)
        g = g_ref[:, cols]
        y = o * lax.rsqrt(ms + EPS) * gain_ref[...] * (g * jax.nn.sigmoid(g))
        y_ref[:, cols] = y.astype(y_ref.dtype)


def _scan_consts(lt):
    idx = np.arange(lt)
    x = idx[:, None] ^ idx[None, :]
    lvl = np.floor(np.log2(np.maximum(x, 1))).astype(np.int32)
    lv = np.where(idx[None, :] < idx[:, None], lvl, np.where(x == 0, -1, -2)).astype(np.int32)
    tri = (idx[None, :] <= idx[:, None]).astype(np.float32)
    return jnp.asarray(lv), jnp.asarray(tri, dtype=BF16)


def _scan(mode, layer, z, extra, out_gain, bsz, seq):
    t = z.shape[0]
    lt = 128
    per_b = seq // lt
    lv, tri = _scan_consts(lt)

    def col(cb):
        return pl.BlockSpec((lt, BRANCH_WIDTH), lambda b, c: (b * per_b + c, cb))

    const2 = lambda b, c: (0, 0)
    in_specs = [col(0), col(1), col(2), col(3)]
    args = [z, z, z, z]
    if mode == "hgrn":
        (lb_logits,) = extra
        in_specs.append(pl.BlockSpec(lb_logits.shape, const2))
        args.append(lb_logits)
    else:
        gup, gb = extra
        in_specs += [pl.BlockSpec((lt, LANES), lambda b, c: (b * per_b + c, 4 * HEADS)),
                     pl.BlockSpec(gup.shape, const2), pl.BlockSpec(gb.shape, const2)]
        args += [z, gup, gb]
    in_specs += [pl.BlockSpec((1, HEAD_DIM), const2), pl.BlockSpec((lt, lt), const2), pl.BlockSpec((lt, lt), const2)]
    args += [out_gain, lv, tri]
    return pl.pallas_call(
        functools.partial(_scan_kernel, mode, layer),
        grid=(bsz, per_b),
        in_specs=in_specs,
        out_specs=pl.BlockSpec((lt, BRANCH_WIDTH), lambda b, c: (b * per_b + c, 0)),
        out_shape=jax.ShapeDtypeStruct((t, BRANCH_WIDTH), BF16),
        scratch_shapes=[pltpu.VMEM((HEADS, HEAD_DIM, HEAD_DIM), F32)],
        compiler_params=_params("parallel", "arbitrary"),
        name="scan_" + mode,
    )(*args)


_KCH = 256


def _dsa_kernel(topk, q_ref, k_ref, v_ref, ziq_ref, zik_ref, qg_ref, kg_ref, stri_ref, o_ref,
                kn_ref, vt_ref, kih_ref, kil_ref, keys_ref, bias_ref, acc_ref):
    j = pl.program_id(1)
    seq = k_ref.shape[0]

    @pl.when(j == 0)
    def _prep():
        def body(c, carry):
            r0 = pl.multiple_of(c * _KCH, _KCH)
            kc = k_ref[pl.ds(r0, _KCH), :].astype(F32)
            for h in range(HEADS):
                kh = kc[:, h * HEAD_DIM:(h + 1) * HEAD_DIM]
                ms = jnp.mean(kh * kh, axis=-1, keepdims=True)
                kn_ref[pl.ds(r0, _KCH), h * HEAD_DIM:(h + 1) * HEAD_DIM] = (
                    kh * lax.rsqrt(ms + EPS) * kg_ref[...]).astype(BF16)
            vt_ref[c] = v_ref[pl.ds(r0, _KCH), :].astype(F32).T.astype(BF16)
            ki = zik_ref[pl.ds(r0, _KCH), :]
            hi = ki.astype(BF16)
            kih_ref[pl.ds(r0, _KCH), :] = hi
            kil_ref[pl.ds(r0, _KCH), :] = (ki - hi.astype(F32)).astype(BF16)
            return carry
        lax.fori_loop(0, seq // _KCH, body, 0)

    n_ch = (j * QBLOCK + QBLOCK + _KCH - 1) // _KCH
    ziq = ziq_ref[...]
    w_t = ziq[:, 2 * LANES:3 * LANES].T
    qi = jnp.concatenate([ziq[:, h * IDX_DIM:(h + 1) * IDX_DIM] for h in range(IDX_HEADS)], axis=0)
    qi_hi, qi_lo = _split2(qi * (IDX_DIM ** -0.5))
    w_rows = [w_t[IDX_DIM + h:IDX_DIM + h + 1, :] * (IDX_HEADS ** -0.5) for h in range(IDX_HEADS)]
    q_pos = j * QBLOCK + lax.broadcasted_iota(I32, (_KCH, QBLOCK), 1)
    row_i = lax.broadcasted_iota(I32, (_KCH, QBLOCK), 0)

    def idx_body(c, carry):
        r0 = pl.multiple_of(c * _KCH, _KCH)
        k_hi = kih_ref[pl.ds(r0, _KCH), :][:, :IDX_DIM]
        k_lo = kil_ref[pl.ds(r0, _KCH), :][:, :IDX_DIM]
        logit = _dot_nt(k_hi, qi_hi) + _dot_nt(k_hi, qi_lo) + _dot_nt(k_lo, qi_hi)
        score = jnp.zeros((_KCH, QBLOCK), F32)
        for h in range(IDX_HEADS):
            score = score + w_rows[h] * jnp.maximum(logit[:, h * QBLOCK:(h + 1) * QBLOCK], 0.0)
        keys_ref[pl.ds(r0, _KCH), :] = jnp.where(r0 + row_i <= q_pos, score, -jnp.inf)
        return carry
    lax.fori_loop(0, n_ch, idx_body, 0)

    def reduce_keys(fn, combine, init):
        def body(c, acc):
            r0 = pl.multiple_of(c * _KCH, _KCH)
            v = fn(keys_ref[pl.ds(r0, _KCH), :]).reshape(_KCH // SUBLANES, SUBLANES, QBLOCK)
            return combine(acc, v)
        return lax.fori_loop(0, n_ch, body, jnp.full((SUBLANES, QBLOCK), init, F32))

    def count(pred):
        acc = reduce_keys(lambda x: jnp.where(pred(x), 1.0, 0.0), lambda a, v: a + jnp.sum(v, axis=0), 0.0)
        return jnp.sum(acc, axis=0, keepdims=True)

    kf = float(topk)
    has_k = (q_pos[0:1, :] + 1).astype(F32) >= kf
    mx = jnp.max(reduce_keys(lambda x: x, lambda a, v: jnp.maximum(a, jnp.max(v, axis=0)), -jnp.inf),
                 axis=0, keepdims=True)
    mn = jnp.min(reduce_keys(lambda x: jnp.where(x == -jnp.inf, jnp.inf, x),
                             lambda a, v: jnp.minimum(a, jnp.min(v, axis=0)), jnp.inf), axis=0, keepdims=True)

    def bisect(_, bracket):
        lo, hi = bracket
        mid = lo + 0.5 * (hi - lo)
        up = count(lambda x: x >= mid) >= kf
        return jnp.where(up, mid, lo), jnp.where(up, hi, mid)

    def kth_of(bracket):
        hi = bracket[1]
        below = reduce_keys(lambda x: jnp.where(x < hi, x, -jnp.inf),
                            lambda a, v: jnp.maximum(a, jnp.max(v, axis=0)), -jnp.inf)
        return jnp.max(below, axis=0, keepdims=True)

    def refine(state):
        bracket = lax.fori_loop(0, 8, bisect, state[0])
        t = kth_of(bracket)
        unsettled = jnp.where(jnp.logical_and(has_k, count(lambda x: x >= t) < kf), 1.0, 0.0)
        return bracket, t, jnp.max(unsettled), state[3] + 1

    bracket0 = lax.fori_loop(0, 16, bisect, (mn, mx + (mx - mn) + 1.0))
    _, thr, _, _ = lax.while_loop(lambda st: jnp.logical_and(st[2] > 0.0, st[3] < 20), refine,
                                  refine((bracket0, mn, jnp.float32(1.0), jnp.int32(0))))
    thr = jnp.where(has_k, thr, -jnp.inf)
    need = kf - count(lambda x: x > thr)

    def sel_body(c, seen):
        r0 = pl.multiple_of(c * _KCH, _KCH)
        x = keys_ref[pl.ds(r0, _KCH), :]
        eq = jnp.where(x == thr, 1.0, 0.0)
        rank = seen + _dot(stri_ref[...], eq.astype(BF16))
        tie_ok = jnp.where(x == thr, jnp.where(rank < need, 0.0, NEG_BIG), NEG_BIG)
        bias = jnp.where(x > thr, 0.0, tie_ok)
        bias_ref[pl.ds(r0, _KCH), :] = jnp.where(x == -jnp.inf, NEG_BIG, bias)
        return seen + jnp.sum(eq, axis=0, keepdims=True)
    lax.fori_loop(0, n_ch, sel_body, jnp.zeros((1, QBLOCK), F32))

    qf = q_ref[...].astype(F32)
    qn = []
    for h in range(HEADS):
        qh = qf[:, h * HEAD_DIM:(h + 1) * HEAD_DIM]
        ms = jnp.mean(qh * qh, axis=-1, keepdims=True)
        qn.append((qh * lax.rsqrt(ms + EPS) * qg_ref[...] * (HEAD_DIM ** -0.5)).astype(BF16))
    acc_ref[...] = jnp.zeros_like(acc_ref)

    @pl.when(n_ch % 2 == 1)
    def _():
        bias_ref[pl.ds(pl.multiple_of(n_ch * _KCH, _KCH), _KCH), :] = jnp.full((_KCH, QBLOCK), NEG_BIG, F32)

    def att_body(c, carry):
        r0 = pl.multiple_of(c * (2 * _KCH), 2 * _KCH)
        bias = bias_ref[pl.ds(r0, 2 * _KCH), :]
        out = []
        for h in range(HEADS):
            m, l = carry[h]
            hd = slice(h * HEAD_DIM, (h + 1) * HEAD_DIM)
            sc = _dot_nt(kn_ref[pl.ds(r0, 2 * _KCH), hd], qn[h]) + bias
            m_new = jnp.maximum(m, jnp.max(sc, axis=0, keepdims=True))
            p = jnp.exp(sc - m_new)
            alpha = jnp.exp(m - m_new)
            pb = p.astype(BF16)
            pv = _dot(vt_ref[2 * c, hd, :], pb[:_KCH]) + _dot(vt_ref[2 * c + 1, hd, :], pb[_KCH:])
            acc_ref[h] = alpha * acc_ref[h] + pv
            out.append((m_new, alpha * l + jnp.sum(p, axis=0, keepdims=True)))
        return tuple(out)
    init = tuple((jnp.full((1, QBLOCK), NEG_BIG, F32), jnp.zeros((1, QBLOCK), F32)) for _ in range(HEADS))
    stats = lax.fori_loop(0, (n_ch + 1) // 2, att_body, init)
    for h in range(HEADS):
        o_ref[:, h * HEAD_DIM:(h + 1) * HEAD_DIM] = (acc_ref[h] / stats[h][1]).T.astype(o_ref.dtype)


def _dsa(zb, zi, q_gain, k_gain, bsz, seq):
    t = zb.shape[0]
    topk = min(DSA_MAX_TOPK, seq // 4)
    nq = seq // QBLOCK
    assert seq % (2 * _KCH) == 0
    idx = np.arange(_KCH)
    stri = jnp.asarray((idx[None, :] < idx[:, None]).astype(np.float32), dtype=BF16)
    const2 = lambda b, j: (0, 0)
    return pl.pallas_call(
        functools.partial(_dsa_kernel, topk),
        grid=(bsz, nq),
        in_specs=[
            pl.BlockSpec((QBLOCK, BRANCH_WIDTH), lambda b, j: (b * nq + j, 0)),
            pl.BlockSpec((seq, BRANCH_WIDTH), lambda b, j: (b, 1)),
            pl.BlockSpec((seq, BRANCH_WIDTH), lambda b, j: (b, 2)),
            pl.BlockSpec((QBLOCK, W_I), lambda b, j: (b * nq + j, 0)),
            pl.BlockSpec((seq, LANES), lambda b, j: (b, 2)),
            pl.BlockSpec((1, HEAD_DIM), const2),
            pl.BlockSpec((1, HEAD_DIM), const2),
            pl.BlockSpec((_KCH, _KCH), const2),
        ],
        out_specs=pl.BlockSpec((QBLOCK, BRANCH_WIDTH), lambda b, j: (b * nq + j, 0)),
        out_shape=jax.ShapeDtypeStruct((t, BRANCH_WIDTH), BF16),
        scratch_shapes=[
            pltpu.VMEM((seq, BRANCH_WIDTH), BF16),
            pltpu.VMEM((seq // _KCH, BRANCH_WIDTH, _KCH), BF16),
            pltpu.VMEM((seq, LANES), BF16),
            pltpu.VMEM((seq, LANES), BF16),
            pltpu.VMEM((seq, QBLOCK), F32),
            pltpu.VMEM((seq, QBLOCK), F32),
            pltpu.VMEM((HEADS, HEAD_DIM, QBLOCK), F32),
        ],
        compiler_params=_params("parallel", "arbitrary"),
        name="dsa",
    )(zb, zb, zb, zi, zi, q_gain, k_gain, stri)


def _merge_kernel(ya_ref, yb_ref, yc_ref, zm_ref, x_ref, gm_ref, wb_ref, wo_ref, o_ref):
    d = x_ref.shape[1]
    mixed = None
    for g, y_ref in enumerate((ya_ref, yb_ref, yc_ref)):
        up = _dot(y_ref[...], wb_ref[g])
        term = jax.nn.sigmoid(zm_ref[:, g * d:(g + 1) * d].astype(F32)) * up
        mixed = term if mixed is None else mixed + term
    o_ref[...] = x_ref[...] + gm_ref[0] * _dot(mixed.astype(BF16), wo_ref[...])


def _merge(ya, yb, yc, zm, x2, gate_m, w_branch, w_out, seq):
    t, d = x2.shape
    tm = 256
    per_b = seq // tm
    row = lambda i: (i, 0)
    return pl.pallas_call(
        _merge_kernel,
        grid=(t // tm,),
        in_specs=[
            pl.BlockSpec((tm, BRANCH_WIDTH), row), pl.BlockSpec((tm, BRANCH_WIDTH), row),
            pl.BlockSpec((tm, BRANCH_WIDTH), row), pl.BlockSpec((tm, W_M), row), pl.BlockSpec((tm, d), row),
            pl.BlockSpec((1, 1, d), lambda i: (i // per_b, 0, 0)),
            pl.BlockSpec(w_branch.shape, lambda i: (0, 0, 0)),
            pl.BlockSpec(w_out.shape, lambda i: (0, 0)),
        ],
        out_specs=pl.BlockSpec((tm, d), row),
        out_shape=jax.ShapeDtypeStruct((t, d), F32),
        compiler_params=_params("parallel"),
        name="merge",
    )(ya, yb, yc, zm, x2, gate_m, w_branch, w_out)


_CAND_ROWS = 72
_ROUTE_GROUP = 4


def _cand_layout():
    flat = np.zeros((_CAND_ROWS,), np.int32)
    live = np.zeros((_CAND_ROWS,), bool)
    r = 0
    for b in range(16):
        flat[r], live[r] = b, True
        r += 1
    for a in (1, 2, 3):
        for b in range(8):
            flat[r], live[r] = a * 16 + b, True
            r += 1
    for a in range(16):
        flat[r], live[r] = a * 16, a >= 4
        r += 1
    for a in range(8):
        flat[r], live[r] = a * 16 + 1, a >= 4
        r += 1
    for a in range(8):
        flat[r], live[r] = a * 16 + 2, a == 4
        r += 1
    flat = np.where(live, flat, 1000 + np.arange(_CAND_ROWS))
    return flat.astype(np.int32), live


def _candidates(v1, v2, live):
    cand = jnp.concatenate(
        [v1[0:1, :] + v2] + [v1[a:a + 1, :] + v2[0:8, :] for a in (1, 2, 3)]
        + [v1 + v2[0:1, :], v1[0:8, :] + v2[1:2, :], v1[0:8, :] + v2[2:3, :]], axis=0)
    return jnp.where(live, cand, -jnp.inf)


def _gone(work):
    return jnp.sum(jnp.where(work == -jnp.inf, 1.0, 0.0), axis=0, keepdims=True)


def _route_head_exact(s1, s2, row_i, flat, live, v1_ref, v2_ref, b2_ref, th_ref, x2_ref):
    ranks = []
    for s, vals_ref in ((s1, v1_ref), (s2, v2_ref)):
        work = s
        rank = jnp.full(s.shape, 127.0, F32)
        for a in range(PEER_TOPK):
            m = jnp.max(work, axis=0, keepdims=True)
            idx = jnp.min(jnp.where(work == m, row_i, N_KEYS), axis=0, keepdims=True)
            hit = row_i == idx
            work = jnp.where(hit, -jnp.inf, work)
            rank = jnp.where(hit, float(a), rank)
            vals_ref[a:a + 1, :] = m
        ranks.append(rank)
    r1, r2 = ranks
    v1 = v1_ref[...]
    v2 = v2_ref[...]
    cand = _candidates(v1, v2, live)
    sel = jnp.zeros(cand.shape, F32)
    work = cand
    for _ in range(PEER_TOPK):
        m = jnp.max(work, axis=0, keepdims=True)
        fi = jnp.min(jnp.where(work == m, flat, 9999), axis=0, keepdims=True)
        hit = flat == fi
        work = jnp.where(hit, -jnp.inf, work)
        sel = jnp.where(hit, 1.0, sel)
    z = jnp.sum(sel * jnp.exp(cand - (v1[0:1, :] + v2[0:1, :])), axis=0, keepdims=True)
    b2_ref[...] = jnp.exp(s2 - v2[0:1, :]) * (0.5 / z)

    zeros8 = jnp.zeros((8, s1.shape[1]), F32)
    cnt_hi = (sel[40:56, :] + jnp.concatenate([sel[56:64, :], zeros8], axis=0)
              + jnp.concatenate([sel[64:72, :], zeros8], axis=0))
    cnt_lo = [jnp.sum(sel[0:16, :], axis=0, keepdims=True)] + [
        jnp.sum(sel[16 + 8 * (a - 1):24 + 8 * (a - 1), :], axis=0, keepdims=True) for a in (1, 2, 3)]
    cnt = jnp.zeros(s1.shape, F32)
    for a in range(PEER_TOPK):
        ca = cnt_lo[a] if a < 4 else cnt_hi[a:a + 1, :]
        cnt = jnp.where(r1 == float(a), ca, cnt)
    th_ref[...] = 0.5 - cnt
    x2_ref[...] = -r2


def _route_kernel(x_ref, gain_ref, shift_ref, scale_ref, wq_ref, sk_ref, flat_ref, live_ref,
                  ht_ref, a1_ref, th_ref, b2_ref, x2_ref, qt_ref, v1_ref, v2_ref):
    tt = x_ref.shape[0]
    h2 = _modulate(x_ref[...], gain_ref[...], shift_ref[0], scale_ref[0])
    ht = h2.T.astype(BF16)
    for g in range(tt // _TG):
        ht_ref[g] = ht[:, g * _TG:(g + 1) * _TG]
    qt_ref[...] = _dot(wq_ref[...], ht)
    row_i = lax.broadcasted_iota(I32, (N_KEYS, tt), 0)
    flat = flat_ref[...]
    live = live_ref[...] > 0.0
    n_dead = float(_CAND_ROWS - int(_cand_layout()[1].sum()))

    group = v1_ref.shape[0]
    for h0 in range(0, PEER_HEADS, group):
        heads = range(h0, h0 + group)
        s1 = [_dot(sk_ref[2 * h], qt_ref[(2 * h) * N_KEYS:(2 * h + 1) * N_KEYS, :].astype(BF16)) for h in heads]
        s2 = [_dot(sk_ref[2 * h + 1], qt_ref[(2 * h + 1) * N_KEYS:(2 * h + 2) * N_KEYS, :].astype(BF16))
              for h in heads]

        work = [list(s1), list(s2)]
        for a in range(PEER_TOPK):
            for half, vals_ref in enumerate((v1_ref, v2_ref)):
                for g in range(group):
                    m = jnp.max(work[half][g], axis=0, keepdims=True)
                    work[half][g] = jnp.where(work[half][g] == m, -jnp.inf, work[half][g])
                    vals_ref[g, a:a + 1, :] = m
        v1 = [v1_ref[g] for g in range(group)]
        v2 = [v2_ref[g] for g in range(group)]
        cand = [_candidates(v1[g], v2[g], live) for g in range(group)]
        wc = list(cand)
        t16 = [None] * group
        for _ in range(PEER_TOPK):
            for g in range(group):
                t16[g] = jnp.max(wc[g], axis=0, keepdims=True)
                wc[g] = jnp.where(wc[g] == t16[g], -jnp.inf, wc[g])
        bad = []
        last = PEER_TOPK - 1
        for g, h in enumerate(heads):
            t17 = jnp.max(wc[g], axis=0, keepdims=True)
            top = v1[g][0:1, :] + v2[g][0:1, :]
            z = jnp.sum(jnp.where(cand[g] >= t16[g], jnp.exp(cand[g] - top), 0.0), axis=0, keepdims=True)
            a1_ref[h] = jnp.exp(s1[g] - v1[g][0:1, :])
            b2_ref[h] = jnp.exp(s2[g] - v2[g][0:1, :]) * (0.5 / z)
            th_ref[h] = jnp.where(s1[g] >= v1[g][last:last + 1, :], 0.5 * (t16[g] + t17) - s1[g], jnp.inf)
            x2_ref[h] = jnp.where(s2[g] >= v2[g][last:last + 1, :], s2[g], -jnp.inf)
            scale = (jnp.abs(v1[g][0:1, :]) + jnp.abs(v1[g][last:last + 1, :])
                     + jnp.abs(v2[g][0:1, :]) + jnp.abs(v2[g][last:last + 1, :]))
            close = jnp.where(t16[g] - t17 > scale * 2.0 ** -22, 0.0, 1.0)
            bad.append(jnp.max(jnp.abs(_gone(work[0][g]) - PEER_TOPK) + jnp.abs(_gone(work[1][g]) - PEER_TOPK)
                               + jnp.abs(_gone(wc[g]) - (PEER_TOPK + n_dead)) + close) > 0.0)

        for g, h in enumerate(heads):
            @pl.when(bad[g])
            def _(g=g, h=h):
                _route_head_exact(s1[g], s2[g], row_i, flat, live, v1_ref.at[g], v2_ref.at[g],
                                  b2_ref.at[h], th_ref.at[h], x2_ref.at[h])


def _route(x2, gain, shift, scale, wq_t, sub_keys, seq):
    t, d = x2.shape
    tt = 256
    per_b = seq // tt
    flat_np, live_np = _cand_layout()
    flat = jnp.asarray(np.broadcast_to(flat_np[:, None], (_CAND_ROWS, tt)).copy())
    live = jnp.asarray(np.broadcast_to(live_np[:, None], (_CAND_ROWS, tt)).astype(np.float32))
    dense = jax.ShapeDtypeStruct((PEER_HEADS, N_KEYS, t), F32)
    dense_spec = pl.BlockSpec((PEER_HEADS, N_KEYS, tt), lambda i: (0, 0, i))
    const2 = lambda i: (0, 0)
    return pl.pallas_call(
        _route_kernel,
        grid=(t // tt,),
        in_specs=[
            pl.BlockSpec((tt, d), lambda i: (i, 0)),
            pl.BlockSpec((1, d), const2),
            pl.BlockSpec((1, 1, d), lambda i: (i // per_b, 0, 0)),
            pl.BlockSpec((1, 1, d), lambda i: (i // per_b, 0, 0)),
            pl.BlockSpec(wq_t.shape, const2),
            pl.BlockSpec(sub_keys.shape, lambda i: (0, 0, 0)),
            pl.BlockSpec((_CAND_ROWS, tt), const2),
            pl.BlockSpec((_CAND_ROWS, tt), const2),
        ],
        out_specs=[pl.BlockSpec((tt // _TG, d, _TG), lambda i: (i, 0, 0)),
                   dense_spec, dense_spec, dense_spec, dense_spec],
        out_shape=[jax.ShapeDtypeStruct((t // _TG, d, _TG), BF16), dense, dense, dense, dense],
        scratch_shapes=[
            pltpu.VMEM((PEER_HEADS * 2 * N_KEYS, tt), F32),
            pltpu.VMEM((_ROUTE_GROUP, PEER_TOPK, tt), F32),
            pltpu.VMEM((_ROUTE_GROUP, PEER_TOPK, tt), F32),
        ],
        compiler_params=_params("parallel"),
        name="peer_route",
    )(x2, gain, shift, scale, wq_t, sub_keys, flat, live)


_JB = 4
_IB = 4
_TG = 256
_RB = _IB * N_KEYS


def _peer_kernel(n_tiles, ht_ref, u_ref, vt_ref, a1_ref, th_ref, b2_ref, x2_ref, x_ref, gf_ref, o_ref,
                 acc_ref, at_ref, gt_ref, bc_ref):
    s = pl.program_id(1)
    n_g, te, tg = at_ref.shape
    tt = n_g * tg
    n_i = te // N_KEYS
    n_jb = N_KEYS // (SUBLANES * _JB)
    assert n_i == SUBLANES
    assert _RB == _IB * N_KEYS

    def bcast_rows():
        i0 = pl.multiple_of(s * n_i, SUBLANES)
        for h in range(PEER_HEADS):
            a1g = a1_ref[h, pl.ds(i0, n_i), :]
            thg = th_ref[h, pl.ds(i0, n_i), :]
            for ii in range(n_i):
                bc_ref[2 * (h * n_i + ii)] = jnp.broadcast_to(a1g[ii:ii + 1, :], (SUBLANES, tt))
                bc_ref[2 * (h * n_i + ii) + 1] = jnp.broadcast_to(thg[ii:ii + 1, :], (SUBLANES, tt))

    def mask_block(tb, jb, ib):
        g, lane0 = divmod(tb * LANES, tg)
        lanes = slice(tb * LANES, (tb + 1) * LANES)
        lanes_g = slice(lane0, lane0 + LANES)
        j0 = jb * (SUBLANES * _JB)
        w = [[None] * _JB for _ in range(_IB)]
        for h in range(PEER_HEADS):
            x2 = [x2_ref[h, j0 + SUBLANES * k:j0 + SUBLANES * (k + 1), lanes] for k in range(_JB)]
            b2 = [b2_ref[h, j0 + SUBLANES * k:j0 + SUBLANES * (k + 1), lanes] for k in range(_JB)]
            for di in range(_IB):
                row = 2 * (h * n_i + ib * _IB + di)
                a1v = bc_ref[row, :, lanes]
                thv = bc_ref[row + 1, :, lanes]
                for k in range(_JB):
                    term = jnp.where(x2[k] >= thv, a1v * b2[k], 0.0)
                    w[di][k] = term if w[di][k] is None else w[di][k] + term
        for di in range(_IB):
            r0 = (ib * _IB + di) * N_KEYS + j0
            rows = slice(r0, r0 + SUBLANES * _JB)
            a = at_ref[g, rows, lanes_g]
            act2 = a + a * lax.erf(a * np.float32(np.sqrt(0.5)))
            gt_ref[g, rows, lanes_g] = (jnp.concatenate(w[di], axis=0) * act2).astype(BF16)

    @pl.when(s == 0)
    def _():
        acc_ref[...] = jnp.zeros_like(acc_ref)

    def pre_act(k):
        rows = slice(k * _RB, (k + 1) * _RB)
        for g in range(n_g):
            at_ref[g, rows, :] = _dot(u_ref[rows, :], ht_ref[g])

    def fold(k):
        rows = slice(k * _RB, (k + 1) * _RB)
        for g in range(n_g):
            acc_ref[g] += _dot(vt_ref[:, rows], gt_ref[g, rows, :])

    bcast_rows()
    n_slices = te // _RB
    pre_act(0)
    for k in range(n_slices):
        if k + 1 < n_slices:
            pre_act(k + 1)
        if k >= 1:
            fold(k - 1)
        for tb in range(tt // LANES):
            for jb in range(n_jb):
                mask_block(tb, jb, k)
    fold(n_slices - 1)

    @pl.when(s == n_tiles - 1)
    def _():
        for g in range(n_g):
            rows = slice(g * tg, (g + 1) * tg)
            o_ref[rows, :] = x_ref[rows, :] + gf_ref[0] * acc_ref[g].T


def _peer(ht, u, vt, a1, th, b2, xk, x2, gate_f, seq):
    t, d = x2.shape
    n_tiles = u.shape[0] // PEER_TE
    tt = 512
    per_b = seq // tt
    f32_spec = pl.BlockSpec((PEER_HEADS, N_KEYS, tt), lambda i, s: (0, 0, i))
    return pl.pallas_call(
        functools.partial(_peer_kernel, n_tiles),
        grid=(t // tt, n_tiles),
        in_specs=[
            pl.BlockSpec((tt // _TG, d, _TG), lambda i, s: (i, 0, 0)),
            pl.BlockSpec((PEER_TE, d), lambda i, s: (s, 0)),
            pl.BlockSpec((d, PEER_TE), lambda i, s: (0, s)),
            f32_spec, f32_spec, f32_spec, f32_spec,
            pl.BlockSpec((tt, d), lambda i, s: (i, 0)),
            pl.BlockSpec((1, 1, d), lambda i, s: (i // per_b, 0, 0)),
        ],
        out_specs=pl.BlockSpec((tt, d), lambda i, s: (i, 0)),
        out_shape=jax.ShapeDtypeStruct((t, d), F32),
        scratch_shapes=[
            pltpu.VMEM((tt // _TG, d, _TG), F32),
            pltpu.VMEM((tt // _TG, PEER_TE, _TG), F32),
            pltpu.VMEM((tt // _TG, PEER_TE, _TG), BF16),
            pltpu.VMEM((2 * PEER_HEADS * PEER_TE // N_KEYS, SUBLANES, tt), F32),
        ],
        compiler_params=_params("parallel", "arbitrary"),
        name="peer_experts",
    )(ht, u, vt, a1, th, b2, xk, x2, gate_f)


def _tcast_kernel(x_ref, o_ref):
    o_ref[...] = x_ref[...].T.astype(o_ref.dtype)


def _cast_kernel(x_ref, o_ref):
    o_ref[...] = x_ref[...].astype(o_ref.dtype)


def _layer_cast(x, layer, dtype, transpose):
    _, r, c = x.shape
    tr, tc = min(r, 1024), min(c, 1024)
    return pl.pallas_call(
        _tcast_kernel if transpose else _cast_kernel,
        grid=(r // tr, c // tc),
        in_specs=[pl.BlockSpec((None, tr, tc), lambda i, j: (layer, i, j))],
        out_specs=pl.BlockSpec((tc, tr), lambda i, j: (j, i)) if transpose else pl.BlockSpec((tr, tc), lambda i, j: (i, j)),
        out_shape=jax.ShapeDtypeStruct((c, r) if transpose else (r, c), dtype),
        compiler_params=_params("parallel", "parallel"),
        name="layer_cast",
    )(x)


def _pad_heads(w, width):
    d = w.shape[0]
    w = w.reshape(d, HEADS, width)
    return jnp.pad(w, ((0, 0), (0, 0), (0, HEAD_DIM - width))).reshape(d, HEADS * HEAD_DIM)


def _pack_segments():
    segs = [(0, W_A + W_B, 0)]
    n_idx = IDX_HEADS * IDX_DIM + IDX_DIM + IDX_HEADS
    src = W_A + W_B
    dst = W_A + W_B
    segs.append((src, n_idx, dst)); src += n_idx; dst += W_I
    for _ in range(2):
        for h in range(HEADS):
            segs.append((src + GLA_DK * h, GLA_DK, dst + HEAD_DIM * h))
        src += HEADS * GLA_DK; dst += BRANCH_WIDTH
    segs.append((src, BRANCH_WIDTH, dst)); src += BRANCH_WIDTH; dst += BRANCH_WIDTH
    code_src = src; src += GLA_GATE_RANK
    segs.append((src, BRANCH_WIDTH, dst)); src += BRANCH_WIDTH; dst += BRANCH_WIDTH
    segs.append((code_src, GLA_GATE_RANK, dst)); dst += LANES
    segs.append((src, W_M, dst))
    assert dst + W_M == W_PACK
    return segs


def _pack_kernel(w_ref, o_ref):
    o_ref[...] = jnp.zeros_like(o_ref)
    for src, width, dst in _pack_segments():
        o_ref[0, :, dst:dst + width] = w_ref[0, :, src:src + width].astype(o_ref.dtype)


def _pack_w_in(w_in):
    depth, d, n_in = w_in.shape
    tr = 128
    return pl.pallas_call(
        _pack_kernel,
        grid=(depth, d // tr),
        in_specs=[pl.BlockSpec((1, tr, n_in), lambda l, i: (l, i, 0))],
        out_specs=pl.BlockSpec((1, tr, W_PACK), lambda l, i: (l, i, 0)),
        out_shape=jax.ShapeDtypeStruct((depth, d, W_PACK), BF16),
        compiler_params=_params("parallel", "parallel"),
        name="pack_w_in",
    )(w_in)


def kernel(x, c, w_ada, b_ada, norm_mix, norm_ffn, w_in, hgrn_lb_logits, hgrn_out_norm, dsa_q_norm, dsa_k_norm,
           gla_gate_up, gla_gate_bias, gla_out_norm, w_branch, w_out, peer_w_query, peer_sub_keys, peer_u, peer_v):
    bsz, seq, d = x.shape
    depth = w_in.shape[0]
    t = bsz * seq
    x2 = x.reshape(t, d)
    mod = _ada(c, w_ada, b_ada)
    w_pack = _pack_w_in(w_in)

    for l in range(depth):
        shift_m, scale_m, gate_m, shift_f, scale_f, gate_f = [
            mod[l, :, k * d:(k + 1) * d].reshape(bsz, 1, d) for k in range(6)]
        za, zb, zi, zc, zm = _inproj(x2, norm_mix[l].reshape(1, d), shift_m, scale_m, w_pack, l, seq)

        ya = _scan("hgrn", l, za, (hgrn_lb_logits,), hgrn_out_norm[l].reshape(1, HEAD_DIM), bsz, seq)
        yb = _dsa(zb, zi, dsa_q_norm[l].reshape(1, HEAD_DIM), dsa_k_norm[l].reshape(1, HEAD_DIM), bsz, seq)
        gup = jnp.pad(_pad_heads(gla_gate_up[l], GLA_DK), ((0, LANES - GLA_GATE_RANK), (0, 0)))
        gb = _pad_heads(gla_gate_bias[l].reshape(1, HEADS * GLA_DK), GLA_DK)
        yc = _scan("gla", l, zc, (gup, gb), gla_out_norm[l].reshape(1, HEAD_DIM), bsz, seq)

        x2 = _merge(ya, yb, yc, zm, x2, gate_m, w_branch[l].astype(BF16), w_out[l].astype(BF16), seq)

        ht, a1, th, b2, xk = _route(
            x2, norm_ffn[l].reshape(1, d), shift_f, scale_f, _layer_cast(peer_w_query, l, BF16, True),
            peer_sub_keys[l].reshape(PEER_HEADS * 2, N_KEYS, -1).astype(BF16), seq)
        x2 = _peer(ht, _layer_cast(peer_u, l, BF16, False), _layer_cast(peer_v, l, BF16, True),
                   a1, th, b2, xk, x2, gate_f, seq)

    return x2.reshape(bsz, seq, d)
```
